```python
import numpy as np
import jax
import jax.numpy as jnp
from jax import lax


D_MODEL = 1024
BATCH = 16
SEQ = 2048
DEPTH = 2

N_BRANCH = 4
BRANCH_WIDTH = 256
Q_BLOCK = 128
EPS = 1e-6
NEG_BIG = -1e30

DSA_HEADS = 4
DSA_HEAD_DIM = 64
DSA_Q_LATENT = 128
DSA_KV_LATENT = 128
IDX_HEADS = 8
IDX_DIM = 32
TOPK_MAX = 256

RWKV_HEADS = 4
RWKV_HEAD_DIM = 64
RWKV_LORA_W = 64
RWKV_LORA_A = 64
RWKV_LORA_G = 128
RWKV_GN_EPS = 64e-5

MLSTM_HEADS = 4
MLSTM_QK_DIM = 32
MLSTM_V_DIM = 64
MLSTM_CHUNK = 64
CONV_WIDTH = 4
GATE_SOFTCAP = 15.0

FOX_HEADS = 4
FOX_HEAD_DIM = 64

D_FF_DENSE = 2816
N_EXPERTS = 8
TOP_K_EXPERTS = 2
D_FF_EXPERT = 3584
N_DENSE_LAYERS = (DEPTH + 1) // 2
N_MOE_LAYERS = DEPTH // 2

DSA_SPLITS = (DSA_Q_LATENT, DSA_KV_LATENT, IDX_DIM, IDX_HEADS)
RWKV_WIDTH = RWKV_HEADS * RWKV_HEAD_DIM
RWKV_SPLITS = (RWKV_WIDTH, RWKV_WIDTH, RWKV_WIDTH, RWKV_LORA_W, RWKV_LORA_A, RWKV_LORA_G)
MLSTM_SPLITS = (MLSTM_HEADS * MLSTM_QK_DIM, MLSTM_HEADS * MLSTM_QK_DIM, MLSTM_HEADS * MLSTM_V_DIM, MLSTM_HEADS * MLSTM_V_DIM, MLSTM_HEADS, MLSTM_HEADS)
FOX_SPLITS = (FOX_HEADS * FOX_HEAD_DIM,) * 3 + (FOX_HEADS,)
GATE_SPLITS = (D_MODEL,) * N_BRANCH
GROUP_SPLITS = (sum(DSA_SPLITS), sum(RWKV_SPLITS), sum(MLSTM_SPLITS), sum(FOX_SPLITS), sum(GATE_SPLITS))
D_IN = sum(GROUP_SPLITS)

kernel_name = 'hybrid_gated_dsa_rwkv7_mlstm_fox_moe'


def _split(z, sizes):
    return jnp.split(z, np.cumsum(sizes)[:-1].tolist(), axis=-1)


def rmsnorm(x, g, eps=EPS):
    xf = x.astype(jnp.float32)
    y = xf * lax.rsqrt(jnp.mean(xf * xf, axis=-1, keepdims=True) + eps)
    return (y * g.astype(jnp.float32)).astype(x.dtype)


def head_rmsnorm(y, g, eps=EPS):
    b, s, nh, d = y.shape
    yf = y.astype(jnp.float32)
    yf = yf * lax.rsqrt(jnp.mean(yf * yf, axis=-1, keepdims=True) + eps)
    return yf.reshape(b, s, nh * d) * g.astype(jnp.float32)


def softcap(t, cap):
    return cap * jnp.tanh(t / cap)


def causal_conv(x, w, bias):
    s = x.shape[1]
    xp = jnp.pad(x, ((0, 0), (w.shape[0] - 1, 0), (0, 0)))
    y = bias
    for j in range(w.shape[0]):
        y = y + xp[:, j:j + s] * w[j]
    return y


def _block_starts(s):
    return jnp.arange(s // Q_BLOCK) * Q_BLOCK


def dsa_mixer(c_q, c_kv, k_idx, w_idx, g_cq, g_ckv, g_kidx, w_uq, w_qidx, w_uv):
    b, s, _ = c_q.shape
    topk = min(TOPK_MAX, s // 4)
    cq = rmsnorm(c_q, g_cq)
    ckv = rmsnorm(c_kv, g_ckv)
    q = jnp.einsum('bsc,chl->bshl', cq, w_uq)
    qi = jnp.einsum('bsc,chd->bshd', cq, w_qidx)
    ki = rmsnorm(k_idx, g_kidx)
    wi = w_idx.astype(jnp.float32) * IDX_HEADS ** -0.5
    spos = jnp.arange(s)

    def block(q0):
        qb = lax.dynamic_slice_in_dim(q, q0, Q_BLOCK, axis=1)
        qib = lax.dynamic_slice_in_dim(qi, q0, Q_BLOCK, axis=1)
        wib = lax.dynamic_slice_in_dim(wi, q0, Q_BLOCK, axis=1)
        tpos = q0 + jnp.arange(Q_BLOCK)
        dots = jnp.einsum('bqhd,bsd->bqhs', qib, ki).astype(jnp.float32) * IDX_DIM ** -0.5
        score = jnp.einsum('bqh,bqhs->bqs', wib, jax.nn.relu(dots))
        score = jnp.where(spos[None, None, :] <= tpos[None, :, None], score, -jnp.inf)
        _, sel = lax.top_k(score, topk)
        valid = sel <= tpos[None, :, None]
        kv_sel = jax.vmap(lambda c, i: c[i])(ckv, sel)
        logits = jnp.einsum('bqhl,bqkl->bhqk', qb, kv_sel).astype(jnp.float32) * DSA_KV_LATENT ** -0.5
        logits = jnp.where(valid[:, None], logits, -jnp.inf)
        p = jax.nn.softmax(logits, axis=-1).astype(ckv.dtype)
        return jnp.einsum('bhqk,bqkl->bqhl', p, kv_sel)

    o = lax.map(block, _block_starts(s))
    o = jnp.moveaxis(o, 0, 1).reshape(b, s, DSA_HEADS, DSA_KV_LATENT)
    y = jnp.einsum('bshl,hld->bshd', o, w_uv)
    return y.reshape(b, s, DSA_HEADS * DSA_HEAD_DIM)


def rwkv7_mixer(z, mu, w0, w_up, a0, a_up, g_up, k_k, k_a, r_k, ln_g, ln_b):
    b, s, _ = z.shape
    nh, n = RWKV_HEADS, RWKV_HEAD_DIM
    z_prev = jnp.pad(z[:, :-1], ((0, 0), (1, 0), (0, 0)))
    z = z + mu * (z_prev - z)
    r, k, v, wd, ad, gd = _split(z, RWKV_SPLITS)
    w_log = -jax.nn.softplus(-(w0 + jnp.tanh(wd) @ w_up)) - 0.5
    decay = jnp.exp(-jnp.exp(w_log.astype(jnp.float32)))
    a = jax.nn.sigmoid((a0 + ad @ a_up).astype(jnp.float32))
    g = jax.nn.sigmoid(gd) @ g_up
    heads = lambda t: t.astype(jnp.float32).reshape(b, s, nh, n)
    kk = heads(k * k_k)
    kk = kk / jnp.maximum(jnp.linalg.norm(kk, axis=-1, keepdims=True), 1e-12)
    k = k * (1.0 + (a - 1.0) * k_a)
    rh, kh, vh, wh, ah = heads(r), heads(k), heads(v), heads(decay), heads(a)

    def step(state, inp):
        r_t, w_t, k_t, v_t, kk_t, a_t = inp
        sa = jnp.einsum('bhvk,bhk->bhv', state, -kk_t)
        state = state * w_t[:, :, None, :] + sa[..., None] * (kk_t * a_t)[:, :, None, :] + v_t[..., None] * k_t[:, :, None, :]
        return state, jnp.einsum('bhvk,bhk->bhv', state, r_t)

    xs = tuple(jnp.moveaxis(t, 1, 0) for t in (rh, wh, kh, vh, kk, ah))
    _, y = lax.scan(step, jnp.zeros((b, nh, n, n), jnp.float32), xs)
    y = jnp.moveaxis(y, 0, 1)
    mean = jnp.mean(y, axis=-1, keepdims=True)
    var = jnp.mean(jnp.square(y - mean), axis=-1, keepdims=True)
    y = ((y - mean) * lax.rsqrt(var + RWKV_GN_EPS)).reshape(b, s, nh * n) * ln_g + ln_b
    bonus = (jnp.sum(rh * kh * r_k, axis=-1, keepdims=True) * vh).reshape(b, s, nh * n)
    return ((y + bonus) * g).astype(z.dtype)


def mlstm_mixer(q, k, v, o, i_pre, f_pre, conv_w, conv_b, b_i, b_f, norm_g):
    b, s, _ = q.shape
    nh, dk, dv, lc = MLSTM_HEADS, MLSTM_QK_DIM, MLSTM_V_DIM, MLSTM_CHUNK
    nc = s // lc
    qk = jax.nn.silu(causal_conv(jnp.concatenate([q, k], axis=-1), conv_w, conv_b))
    q, k = jnp.split(qk, 2, axis=-1)

    def chunked(t, d):
        return t.reshape(b, nc, lc, nh, d).transpose(0, 3, 1, 2, 4).astype(jnp.float32)

    def gate_chunks(t):
        return t.astype(jnp.float32).reshape(b, nc, lc, nh).transpose(0, 3, 1, 2)

    qc = chunked(q, dk) * dk ** -0.5
    kc = chunked(k, dk)
    vc = chunked(v, dv)
    li = gate_chunks(softcap(i_pre + b_i, GATE_SOFTCAP))
    lf = jax.nn.log_sigmoid(gate_chunks(softcap(f_pre + b_f, GATE_SOFTCAP)))
    bcum = jnp.cumsum(lf, axis=-1)
    gtot = bcum[..., -1]
    a_log = gtot[..., None] - bcum + li
    a_max = jnp.max(a_log, axis=-1)
    a_w = jnp.exp(a_log - a_max[..., None])
    kv_chunk = jnp.einsum('bhcl,bhclv,bhcld->bhcvd', a_w, vc, kc)
    n_chunk = jnp.einsum('bhcl,bhcld->bhcd', a_w, kc)

    def step(carry, inp):
        c_st, n_st, m_st = carry
        g_c, am_c, kv_c, nk_c = inp
        m_new = jnp.maximum(g_c + m_st, am_c)
        s_old = jnp.exp(g_c + m_st - m_new)
        s_new = jnp.exp(am_c - m_new)
        c_new = s_old[..., None, None] * c_st + s_new[..., None, None] * kv_c
        n_new = s_old[..., None] * n_st + s_new[..., None] * nk_c
        return (c_new, n_new, m_new), (c_st, n_st, m_st)

    init = (jnp.zeros((b, nh, dv, dk), jnp.float32), jnp.zeros((b, nh, dk), jnp.float32), jnp.full((b, nh), NEG_BIG, jnp.float32))
    xs = (jnp.moveaxis(gtot, 2, 0), jnp.moveaxis(a_max, 2, 0), jnp.moveaxis(kv_chunk, 2, 0), jnp.moveaxis(n_chunk, 2, 0))
    _, (c_prev, n_prev, m_prev) = lax.scan(step, init, xs)
    c_prev = jnp.moveaxis(c_prev, 0, 2)
    n_prev = jnp.moveaxis(n_prev, 0, 2)
    m_prev = jnp.moveaxis(m_prev, 0, 2)
    causal = jnp.tril(jnp.ones((lc, lc), dtype=bool))
    d_intra = jnp.where(causal, bcum[..., :, None] - bcum[..., None, :] + li[..., None, :], NEG_BIG)
    m_inter = bcum + m_prev[..., None]
    m_t = jnp.maximum(m_inter, jnp.max(d_intra, axis=-1))
    w_intra = jnp.exp(d_intra - m_t[..., None]) * jnp.einsum('bhctd,bhcsd->bhcts', qc, kc)
    s_inter = jnp.exp(m_inter - m_t)
    num = jnp.einsum('bhcts,bhcsv->bhctv', w_intra, vc) + s_inter[..., None] * jnp.einsum('bhcvd,bhctd->bhctv', c_prev, qc)
    den = jnp.sum(w_intra, axis=-1) + s_inter * jnp.einsum('bhcd,bhctd->bhct', n_prev, qc)
    hcell = num / jnp.maximum(jnp.abs(den), jnp.exp(-m_t))[..., None]
    hcell = hcell.transpose(0, 2, 3, 1, 4).reshape(b, s, nh, dv)
    y = head_rmsnorm(hcell, norm_g) * jax.nn.sigmoid(o.astype(jnp.float32))
    return y.astype(v.dtype)


def fox_mixer(q, k, v, f_pre, b_f):
    b, s, _ = q.shape
    nh, d = FOX_HEADS, FOX_HEAD_DIM
    qh = q.reshape(b, s, nh, d)
    kh = k.reshape(b, s, nh, d)
    vh = v.reshape(b, s, nh, d)
    lf = jax.nn.log_sigmoid((f_pre + b_f).astype(jnp.float32))
    fcum = jnp.cumsum(lf, axis=1).transpose(0, 2, 1)
    spos = jnp.arange(s)

    def block(q0):
        qb = lax.dynamic_slice_in_dim(qh, q0, Q_BLOCK, axis=1)
        fq = lax.dynamic_slice_in_dim(fcum, q0, Q_BLOCK, axis=2)
        tpos = q0 + jnp.arange(Q_BLOCK)
        logits = jnp.einsum('bqhd,bshd->bhqs', qb, kh).astype(jnp.float32) * d ** -0.5
        logits = logits + fq[..., None] - fcum[:, :, None, :]
        logits = jnp.where(spos[None, None, None, :] <= tpos[None, None, :, None], logits, -jnp.inf)
        p = jax.nn.softmax(logits, axis=-1).astype(vh.dtype)
        return jnp.einsum('bhqs,bshd->bqhd', p, vh)

    o = lax.map(block, _block_starts(s))
    return jnp.moveaxis(o, 0, 1).reshape(b, s, nh * d)


def swiglu(t, w1, w3, w2):
    return (jax.nn.silu(t @ w1) * (t @ w3)) @ w2


def moe_swiglu(h, router, w1, w3, w2):
    b, s, d = h.shape
    t = h.reshape(b * s, d)
    logits = (t @ router).astype(jnp.float32)
    top_v, top_i = lax.top_k(logits, TOP_K_EXPERTS)
    top_p = jax.nn.softmax(top_v, axis=-1)
    gate = jnp.sum(jax.nn.one_hot(top_i, N_EXPERTS, dtype=jnp.float32) * top_p[..., None], axis=1).astype(h.dtype)
    y = jnp.zeros_like(t)
    for e in range(N_EXPERTS):
        y = y + gate[:, e:e + 1] * swiglu(t, w1[e], w3[e], w2[e])
    return y.reshape(b, s, d)


def setup_inputs(seed: int = 0) -> dict:
    key = jax.random.key(seed)
    ks = iter(jax.random.split(key, 64))

    def nrm(shape, scale):
        return jax.random.normal(next(ks), shape, jnp.float32) * scale

    def unif(shape, lo, hi):
        return jax.random.uniform(next(ks), shape, jnp.float32, lo, hi)

    def gain(shape):
        return 1.0 + nrm(shape, 0.02)

    L = DEPTH
    qk_conv = 2 * MLSTM_HEADS * MLSTM_QK_DIM
    return {
        'x': nrm((BATCH, SEQ, D_MODEL), 1.0),
        'norm_mix_g': gain((L, D_MODEL)),
        'w_in': nrm((L, D_MODEL, D_IN), D_MODEL ** -0.5),
        'dsa_g_cq': gain((L, DSA_Q_LATENT)),
        'dsa_g_ckv': gain((L, DSA_KV_LATENT)),
        'dsa_g_kidx': gain((L, IDX_DIM)),
        'dsa_w_uq': nrm((L, DSA_Q_LATENT, DSA_HEADS, DSA_KV_LATENT), DSA_Q_LATENT ** -0.5),
        'dsa_w_qidx': nrm((L, DSA_Q_LATENT, IDX_HEADS, IDX_DIM), DSA_Q_LATENT ** -0.5),
        'dsa_w_uv': nrm((L, DSA_HEADS, DSA_KV_LATENT, DSA_HEAD_DIM), DSA_KV_LATENT ** -0.5),
        'rwkv_mu': unif((L, sum(RWKV_SPLITS)), 0.0, 1.0),
        'rwkv_w0': unif((L, RWKV_WIDTH), -5.0, -1.0),
        'rwkv_w_up': nrm((L, RWKV_LORA_W, RWKV_WIDTH), 0.1),
        'rwkv_a0': nrm((L, RWKV_WIDTH), 0.1),
        'rwkv_a_up': nrm((L, RWKV_LORA_A, RWKV_WIDTH), 0.1),
        'rwkv_g_up': nrm((L, RWKV_LORA_G, RWKV_WIDTH), RWKV_LORA_G ** -0.5),
        'rwkv_k_k': 0.85 + nrm((L, RWKV_WIDTH), 0.05),
        'rwkv_k_a': 1.0 + nrm((L, RWKV_WIDTH), 0.05),
        'rwkv_r_k': nrm((L, RWKV_HEADS, RWKV_HEAD_DIM), 0.1),
        'rwkv_ln_g': gain((L, RWKV_WIDTH)),
        'rwkv_ln_b': nrm((L, RWKV_WIDTH), 0.02),
        'mlstm_conv_w': nrm((L, CONV_WIDTH, qk_conv), 0.5),
        'mlstm_conv_b': nrm((L, qk_conv), 0.02),
        'mlstm_b_i': nrm((L, MLSTM_HEADS), 0.1),
        'mlstm_b_f': 3.0 + nrm((L, MLSTM_HEADS), 0.5),
        'mlstm_norm_g': gain((L, MLSTM_HEADS * MLSTM_V_DIM)),
        'fox_b_f': 4.0 + nrm((L, FOX_HEADS), 0.5),
        'w_branch': nrm((L, N_BRANCH, BRANCH_WIDTH, D_MODEL), BRANCH_WIDTH ** -0.5),
        'w_out': nrm((L, D_MODEL, D_MODEL), D_MODEL ** -0.5),
        'norm_ffn_g': gain((L, D_MODEL)),
        'ffn_w1': nrm((N_DENSE_LAYERS, D_MODEL, D_FF_DENSE), D_MODEL ** -0.5),
        'ffn_w3': nrm((N_DENSE_LAYERS, D_MODEL, D_FF_DENSE), D_MODEL ** -0.5),
        'ffn_w2': nrm((N_DENSE_LAYERS, D_FF_DENSE, D_MODEL), D_FF_DENSE ** -0.5),
        'moe_router': nrm((N_MOE_LAYERS, D_MODEL, N_EXPERTS), D_MODEL ** -0.5),
        'moe_w1': nrm((N_MOE_LAYERS, N_EXPERTS, D_MODEL, D_FF_EXPERT), D_MODEL ** -0.5),
        'moe_w3': nrm((N_MOE_LAYERS, N_EXPERTS, D_MODEL, D_FF_EXPERT), D_MODEL ** -0.5),
        'moe_w2': nrm((N_MOE_LAYERS, N_EXPERTS, D_FF_EXPERT, D_MODEL), D_FF_EXPERT ** -0.5),
        'final_norm_g': gain((D_MODEL,)),
    }


def reference(x, norm_mix_g, w_in, dsa_g_cq, dsa_g_ckv, dsa_g_kidx, dsa_w_uq, dsa_w_qidx, dsa_w_uv, rwkv_mu, rwkv_w0, rwkv_w_up, rwkv_a0, rwkv_a_up, rwkv_g_up, rwkv_k_k, rwkv_k_a, rwkv_r_k, rwkv_ln_g, rwkv_ln_b, mlstm_conv_w, mlstm_conv_b, mlstm_b_i, mlstm_b_f, mlstm_norm_g, fox_b_f, w_branch, w_out, norm_ffn_g, ffn_w1, ffn_w3, ffn_w2, moe_router, moe_w1, moe_w3, moe_w2, final_norm_g):
    for l in range(DEPTH):
        h = rmsnorm(x, norm_mix_g[l])
        z = h @ w_in[l]
        z_dsa, z_rwkv, z_mlstm, z_fox, z_gate = _split(z, GROUP_SPLITS)
        y_dsa = dsa_mixer(*_split(z_dsa, DSA_SPLITS), dsa_g_cq[l], dsa_g_ckv[l], dsa_g_kidx[l], dsa_w_uq[l], dsa_w_qidx[l], dsa_w_uv[l])
        y_rwkv = rwkv7_mixer(z_rwkv, rwkv_mu[l], rwkv_w0[l], rwkv_w_up[l], rwkv_a0[l], rwkv_a_up[l], rwkv_g_up[l], rwkv_k_k[l], rwkv_k_a[l], rwkv_r_k[l], rwkv_ln_g[l], rwkv_ln_b[l])
        y_mlstm = mlstm_mixer(*_split(z_mlstm, MLSTM_SPLITS), mlstm_conv_w[l], mlstm_conv_b[l], mlstm_b_i[l], mlstm_b_f[l], mlstm_norm_g[l])
        y_fox = fox_mixer(*_split(z_fox, FOX_SPLITS), fox_b_f[l])
        gates = _split(jax.nn.sigmoid(z_gate.astype(jnp.float32)).astype(x.dtype), GATE_SPLITS)
        merged = jnp.zeros_like(x)
        for n, y_n in enumerate((y_dsa, y_rwkv, y_mlstm, y_fox)):
            merged = merged + gates[n] * (y_n.astype(x.dtype) @ w_branch[l, n])
        x = x + merged @ w_out[l]
        h = rmsnorm(x, norm_ffn_g[l])
        j = l // 2
        if l % 2 == 0:
            x = x + swiglu(h, ffn_w1[j], ffn_w3[j], ffn_w2[j])
        else:
            x = x + moe_swiglu(h, moe_router[j], moe_w1[j], moe_w3[j], moe_w2[j])
    return rmsnorm(x, final_norm_g)
```

```python
import functools

import jax
import jax.numpy as jnp
from jax import lax
from jax.experimental import pallas as pl
from jax.experimental.pallas import tpu as pltpu

F32 = jnp.float32
BF16 = jnp.bfloat16
HIGHEST = lax.Precision.HIGHEST

EPS = 1e-6
NEG_BIG = -1e30

N_BRANCH = 4
BRANCH_WIDTH = 256
DSA_HEADS = 4
DSA_HEAD_DIM = 64
DSA_Q_LATENT = 128
DSA_KV_LATENT = 128
IDX_HEADS = 8
IDX_DIM = 32
TOPK_MAX = 256
DSA_Q_BLOCK = 128

RWKV_HEADS = 4
RWKV_HEAD_DIM = 64
RWKV_WIDTH = RWKV_HEADS * RWKV_HEAD_DIM
RWKV_LORA_W = 64
RWKV_LORA_A = 64
RWKV_LORA_G = 128
RWKV_GN_EPS = 64e-5
RWKV_IN = 3 * RWKV_WIDTH + RWKV_LORA_W + RWKV_LORA_A + RWKV_LORA_G

MLSTM_HEADS = 4
MLSTM_QK_DIM = 32
MLSTM_V_DIM = 64
MLSTM_CHUNK = 64
CONV_WIDTH = 4
GATE_SOFTCAP = 15.0
MLSTM_QK = MLSTM_HEADS * MLSTM_QK_DIM
MLSTM_V = MLSTM_HEADS * MLSTM_V_DIM

FOX_HEADS = 4
FOX_HEAD_DIM = 64
FOX_WIDTH = FOX_HEADS * FOX_HEAD_DIM

N_EXPERTS = 8

VMEM_LIMIT_BYTES = 56 * 1024 * 1024
LANES = 128


def _cparams(*sem):
    return pltpu.CompilerParams(dimension_semantics=sem, vmem_limit_bytes=VMEM_LIMIT_BYTES)


def _dot(a, b):
    return jnp.dot(a, b, preferred_element_type=F32)


def _dot_hi(a, b):
    return jnp.dot(a, b, preferred_element_type=F32, precision=HIGHEST)


def _dot_nt(a, b):
    return lax.dot_general(a, b, (((1,), (1,)), ((), ())), preferred_element_type=F32)


def _dot_nt_hi(a, b):
    return lax.dot_general(a, b, (((1,), (1,)), ((), ())), preferred_element_type=F32, precision=HIGHEST)


def _log_sigmoid(t):
    return jnp.minimum(t, 0.0) - jnp.log1p(jnp.exp(-jnp.abs(t)))


def _sigmoid(t):
    return 1.0 / (1.0 + jnp.exp(-t))


def _silu(t):
    return t * _sigmoid(t)


def _rms_rows(t, g_row):
    return t * lax.rsqrt(jnp.mean(t * t, axis=-1, keepdims=True) + EPS) * g_row


def _rms_cols(t, g_col):
    return t * lax.rsqrt(jnp.mean(t * t, axis=0, keepdims=True) + EPS) * g_col


def _full(shape):
    n = len(shape)
    return pl.BlockSpec(shape, lambda *_: (0,) * n)


T_CQ = 0
T_CKV = T_CQ + DSA_Q_LATENT
T_WIDX = T_CKV + DSA_KV_LATENT
T_FOXF = T_WIDX + IDX_HEADS
T_MK = T_FOXF + 8
T_MV = T_MK + MLSTM_QK
T_MG = T_MV + MLSTM_V
T_ROWS = T_MG + 8
D_COLS = 3 * LANES
M_COLS = 2 * MLSTM_QK + 2 * MLSTM_V + LANES


def _inproj_kernel(x_ref, g_ref, wr_ref, wm_ref, wf_ref, wd_ref, wt_ref,
                   gcq_ref, gckv_col_ref, gckv_row_ref, gki_ref, bf_row_ref, bf_col_ref,
                   zr_ref, zm_ref, q_ref, k_ref, v_ref, ckv_ref, ki_ref, fcol_ref,
                   cqt_ref, ckvt_ref, wit_ref, frow_ref, mkt_ref, mvt_ref, mgt_ref,
                   carry_row, carry_col):
    j = pl.program_id(1)
    tm = x_ref.shape[1]

    @pl.when(j == 0)
    def _():
        carry_row[...] = jnp.zeros_like(carry_row)
        carry_col[...] = jnp.zeros_like(carry_col)

    x = x_ref[0]
    h = _rms_rows(x, g_ref[...]).astype(BF16)

    zr_ref[0] = _dot(h, wr_ref[...])
    zm_ref[0] = _dot(h, wm_ref[...])

    zf = _dot(h, wf_ref[...])
    for hd in range(FOX_HEADS):
        lo = hd * FOX_HEAD_DIM
        q_ref[0, hd] = (zf[:, lo:lo + FOX_HEAD_DIM] * FOX_HEAD_DIM ** -0.5).astype(BF16)
        k_ref[0, hd] = zf[:, FOX_WIDTH + lo:FOX_WIDTH + lo + FOX_HEAD_DIM].astype(BF16)
        v_ref[0, hd] = zf[:, 2 * FOX_WIDTH + lo:2 * FOX_WIDTH + lo + FOX_HEAD_DIM].astype(BF16)

    zd = _dot(h, wd_ref[...])
    ckv_ref[0] = _rms_rows(zd[:, :LANES], gckv_row_ref[...]).astype(BF16)
    ki_ref[0] = _rms_rows(zd[:, LANES:LANES + IDX_DIM], gki_ref[...])
    lf_col = _log_sigmoid(zd[:, 2 * LANES:] + bf_row_ref[...])
    r_i = lax.broadcasted_iota(jnp.int32, (tm, tm), 0)
    c_i = lax.broadcasted_iota(jnp.int32, (tm, tm), 1)
    tri = jnp.where(c_i <= r_i, 1.0, 0.0).astype(F32)
    cum_col = _dot_hi(tri, lf_col) + carry_row[...]
    fcol_ref[0] = cum_col
    carry_row[...] = cum_col[tm - 1:tm, :]

    zt = _dot_nt(wt_ref[...], h)
    cqt_ref[0] = _rms_cols(zt[T_CQ:T_CKV], gcq_ref[...])
    ckvt_ref[0] = _rms_cols(zt[T_CKV:T_WIDX], gckv_col_ref[...]).astype(BF16)
    wit_ref[0] = zt[T_WIDX:T_FOXF] * IDX_HEADS ** -0.5
    lf_row = _log_sigmoid(zt[T_FOXF:T_MK] + bf_col_ref[...])
    cum_row = _dot_nt_hi(lf_row, tri) + carry_col[:, 0:1]
    frow_ref[0] = cum_row
    carry_col[...] = jnp.broadcast_to(cum_row[:, tm - 1:tm], carry_col.shape)
    mkt_ref[0] = zt[T_MK:T_MV]
    mvt_ref[0] = zt[T_MV:T_MG]
    mgt_ref[0] = zt[T_MG:T_ROWS]


def _inproj_weights(w_in_l):
    d = w_in_l.shape[0]
    o = 0
    w_cq = w_in_l[:, o:o + DSA_Q_LATENT]; o += DSA_Q_LATENT
    w_ckv = w_in_l[:, o:o + DSA_KV_LATENT]; o += DSA_KV_LATENT
    w_kidx = w_in_l[:, o:o + IDX_DIM]; o += IDX_DIM
    w_widx = w_in_l[:, o:o + IDX_HEADS]; o += IDX_HEADS
    w_rwkv = w_in_l[:, o:o + RWKV_IN]; o += RWKV_IN
    w_mq = w_in_l[:, o:o + MLSTM_QK]; o += MLSTM_QK
    w_mk = w_in_l[:, o:o + MLSTM_QK]; o += MLSTM_QK
    w_mv = w_in_l[:, o:o + MLSTM_V]; o += MLSTM_V
    w_mo = w_in_l[:, o:o + MLSTM_V]; o += MLSTM_V
    w_mi = w_in_l[:, o:o + MLSTM_HEADS]; o += MLSTM_HEADS
    w_mf = w_in_l[:, o:o + MLSTM_HEADS]; o += MLSTM_HEADS
    w_fox = w_in_l[:, o:o + 3 * FOX_WIDTH]; o += 3 * FOX_WIDTH
    w_ff = w_in_l[:, o:o + FOX_HEADS]; o += FOX_HEADS
    w_gate = w_in_l[:, o:]

    def padc(w, n):
        return jnp.pad(w, ((0, 0), (0, n - w.shape[1])))

    w_d = jnp.concatenate([w_ckv, padc(w_kidx, LANES), padc(w_ff, LANES)], axis=1)
    w_m = jnp.concatenate([w_mq, w_mk, w_mv, w_mo, padc(jnp.concatenate([w_mi, w_mf], axis=1), LANES)], axis=1)
    w_t = jnp.concatenate([w_cq, w_ckv, w_widx, padc(w_ff, 8), w_mk, w_mv, w_mi, w_mf], axis=1).T
    assert w_t.shape == (T_ROWS, d)
    cast = lambda w: w.astype(BF16)
    return dict(w_r=cast(w_rwkv), w_m=cast(w_m), w_f=cast(w_fox), w_d=cast(w_d), w_t=cast(w_t), w_gate=cast(w_gate))


def _inproj(x, g, wts, g_cq, g_ckv, g_kidx, fox_b_f, tm):
    b, s, d = x.shape
    nj = s // tm
    bf_row = jnp.pad(fox_b_f, (0, LANES - FOX_HEADS)).reshape(1, LANES)
    bf_col = jnp.pad(fox_b_f, (0, 8 - FOX_HEADS)).reshape(8, 1)
    row = lambda w: pl.BlockSpec((1, tm, w), lambda i, j: (i, j, 0))
    head = pl.BlockSpec((1, FOX_HEADS, tm, FOX_HEAD_DIM), lambda i, j: (i, 0, j, 0))
    col = lambda r: pl.BlockSpec((1, r, tm), lambda i, j: (i, 0, j))
    out_shape = [
        jax.ShapeDtypeStruct((b, s, RWKV_IN), F32),
        jax.ShapeDtypeStruct((b, s, M_COLS), F32),
        jax.ShapeDtypeStruct((b, FOX_HEADS, s, FOX_HEAD_DIM), BF16),
        jax.ShapeDtypeStruct((b, FOX_HEADS, s, FOX_HEAD_DIM), BF16),
        jax.ShapeDtypeStruct((b, FOX_HEADS, s, FOX_HEAD_DIM), BF16),
        jax.ShapeDtypeStruct((b, s, DSA_KV_LATENT), BF16),
        jax.ShapeDtypeStruct((b, s, IDX_DIM), F32),
        jax.ShapeDtypeStruct((b, s, LANES), F32),
        jax.ShapeDtypeStruct((b, DSA_Q_LATENT, s), F32),
        jax.ShapeDtypeStruct((b, DSA_KV_LATENT, s), BF16),
        jax.ShapeDtypeStruct((b, IDX_HEADS, s), F32),
        jax.ShapeDtypeStruct((b, 8, s), F32),
        jax.ShapeDtypeStruct((b, MLSTM_QK, s), F32),
        jax.ShapeDtypeStruct((b, MLSTM_V, s), F32),
        jax.ShapeDtypeStruct((b, 8, s), F32),
    ]
    out_specs = [row(RWKV_IN), row(M_COLS), head, head, head, row(DSA_KV_LATENT), row(IDX_DIM), row(LANES),
                 col(DSA_Q_LATENT), col(DSA_KV_LATENT), col(IDX_HEADS), col(8), col(MLSTM_QK), col(MLSTM_V), col(8)]
    in_specs = [row(d), _full((1, d)), _full(wts['w_r'].shape), _full(wts['w_m'].shape), _full(wts['w_f'].shape),
                _full(wts['w_d'].shape), _full(wts['w_t'].shape),
                _full((DSA_Q_LATENT, 1)), _full((DSA_KV_LATENT, 1)), _full((1, DSA_KV_LATENT)), _full((1, IDX_DIM)),
                _full((1, LANES)), _full((8, 1))]
    return pl.pallas_call(
        _inproj_kernel,
        out_shape=out_shape,
        grid=(b, nj),
        in_specs=in_specs,
        out_specs=out_specs,
        scratch_shapes=[pltpu.VMEM((1, LANES), F32), pltpu.VMEM((8, LANES), F32)],
        compiler_params=_cparams("arbitrary", "arbitrary"),
        name="inproj",
    )(x, g.reshape(1, d), wts['w_r'], wts['w_m'], wts['w_f'], wts['w_d'], wts['w_t'],
      g_cq.reshape(-1, 1), g_ckv.reshape(-1, 1), g_ckv.reshape(1, -1), g_kidx.reshape(1, -1), bf_row, bf_col)


def _fox_kernel(q_ref, k_ref, v_ref, fcol_ref, frow_ref, o_ref):
    qi = pl.program_id(1)
    tq = q_ref.shape[2]
    outs = []
    for hd in range(FOX_HEADS):
        q = q_ref[0, hd]
        fq = fcol_ref[0][:, hd:hd + 1]

        def scores(j):
            start = pl.multiple_of(j * tq, tq)
            k = k_ref[0, hd, pl.ds(start, tq), :]
            v = v_ref[0, hd, pl.ds(start, tq), :]
            fk = frow_ref[0, hd:hd + 1, pl.ds(start, tq)]
            return _dot_nt(q, k) + (fq - fk), v

        def update(carry, sc, v):
            m, l, acc = carry
            m_new = jnp.maximum(m, jnp.max(sc, axis=-1, keepdims=True))
            alpha = jnp.exp(m - m_new)
            p = jnp.exp(sc - m_new)
            l = alpha * l + jnp.sum(p, axis=-1, keepdims=True)
            acc = alpha * acc + _dot(p.astype(BF16), v)
            return m_new, l, acc

        def body(j, carry):
            sc, v = scores(j)
            return update(carry, sc, v)

        init = (jnp.full((tq, 1), NEG_BIG, F32), jnp.zeros((tq, 1), F32), jnp.zeros((tq, FOX_HEAD_DIM), F32))
        carry = lax.fori_loop(0, qi, body, init)
        sc, v = scores(qi)
        r_i = lax.broadcasted_iota(jnp.int32, (tq, tq), 0)
        c_i = lax.broadcasted_iota(jnp.int32, (tq, tq), 1)
        sc = jnp.where(c_i <= r_i, sc, NEG_BIG)
        m, l, acc = update(carry, sc, v)
        outs.append(acc / l)
    o_ref[0] = jnp.concatenate(outs, axis=-1).astype(o_ref.dtype)


def _fox(q, k, v, fcol, frow, tq):
    b, nh, s, d = q.shape
    return pl.pallas_call(
        _fox_kernel,
        out_shape=jax.ShapeDtypeStruct((b, s, nh * d), BF16),
        grid=(b, s // tq),
        in_specs=[
            pl.BlockSpec((1, nh, tq, d), lambda i, j: (i, 0, j, 0)),
            pl.BlockSpec((1, nh, s, d), lambda i, j: (i, 0, 0, 0)),
            pl.BlockSpec((1, nh, s, d), lambda i, j: (i, 0, 0, 0)),
            pl.BlockSpec((1, tq, LANES), lambda i, j: (i, j, 0)),
            pl.BlockSpec((1, 8, s), lambda i, j: (i, 0, 0)),
        ],
        out_specs=pl.BlockSpec((1, tq, nh * d), lambda i, j: (i, j, 0)),
        compiler_params=_cparams("arbitrary", "arbitrary"),
        name="fox",
    )(q, k, v, fcol, frow)


INT_MIN = -2 ** 31


def _dsa_kernel(cqt_ref, wit_ref, ki_ref, ckv_ref, ckvt_ref, wuqt_ref, wqit_ref, wuvt_ref, o_ref):
    qb = pl.program_id(1)
    s = ki_ref.shape[1]
    nq = cqt_ref.shape[2]
    topk = min(TOPK_MAX, s // 4)
    cq = cqt_ref[0]
    qi_t = _dot_hi(wqit_ref[...], cq)
    ki = ki_ref[0]
    wi = wit_ref[0]
    score = jnp.zeros((s, nq), F32)
    for hd in range(IDX_HEADS):
        dots = _dot_hi(ki, qi_t[hd * IDX_DIM:(hd + 1) * IDX_DIM, :]) * IDX_DIM ** -0.5
        score = score + wi[hd:hd + 1, :] * jnp.maximum(dots, 0.0)
    spos = lax.broadcasted_iota(jnp.int32, (s, nq), 0)
    tpos = qb * nq + lax.broadcasted_iota(jnp.int32, (s, nq), 1)
    causal = spos <= tpos

    bits = pltpu.bitcast(score, jnp.int32)
    key = jnp.where(bits < 0, bits ^ 0x7FFFFFFF, bits)
    key = jnp.where(score == 0.0, 0, key)
    key = jnp.where(causal, key, INT_MIN)

    def count(mask):
        return jnp.sum(jnp.where(mask, 1, 0), axis=0, keepdims=True)

    def value_bit(i, lo):
        cand = lo + jnp.left_shift(jnp.int32(1), 31 - i)
        return jnp.where(count(key >= cand) >= topk, cand, lo)

    thr = lax.fori_loop(0, 32, value_bit, jnp.full((1, nq), INT_MIN, jnp.int32))
    above = key > thr
    tied = key == thr
    need = topk - count(above)

    n_bits = s.bit_length()

    def index_bit(i, lo):
        cand = lo + jnp.left_shift(jnp.int32(1), n_bits - 1 - i)
        return jnp.where(count(tied & (spos < cand)) < need, cand, lo)

    last = lax.fori_loop(0, n_bits, index_bit, jnp.zeros((1, nq), jnp.int32))
    sel = (above | (tied & (spos <= last))) & causal

    q_t = _dot(wuqt_ref[...], cq.astype(BF16)).astype(BF16)
    ckv = ckv_ref[0]
    ckvt = ckvt_ref[0]
    lat = DSA_KV_LATENT
    outs = []
    for hd in range(DSA_HEADS):
        lg = _dot(ckv, q_t[hd * lat:(hd + 1) * lat, :]) * lat ** -0.5
        lg = jnp.where(sel, lg, NEG_BIG)
        p = jnp.exp(lg - jnp.max(lg, axis=0, keepdims=True))
        o_t = _dot(ckvt, p.astype(BF16)) / jnp.sum(p, axis=0, keepdims=True)
        outs.append(_dot(wuvt_ref[hd], o_t.astype(BF16)))
    o_ref[0] = jnp.concatenate(outs, axis=0).T.astype(o_ref.dtype)


def _dsa(cqt, wit, ki, ckv, ckvt, w_uq, w_qidx, w_uv):
    b, c, s = cqt.shape
    nq = DSA_Q_BLOCK
    wuqt = w_uq.reshape(c, -1).T.astype(BF16)
    wqit = w_qidx.reshape(c, -1).T
    wuvt = jnp.transpose(w_uv, (0, 2, 1)).astype(BF16)
    return pl.pallas_call(
        _dsa_kernel,
        out_shape=jax.ShapeDtypeStruct((b, s, DSA_HEADS * DSA_HEAD_DIM), BF16),
        grid=(b, s // nq),
        in_specs=[pl.BlockSpec((1, c, nq), lambda i, j: (i, 0, j)),
                  pl.BlockSpec((1, IDX_HEADS, nq), lambda i, j: (i, 0, j)),
                  pl.BlockSpec((1, s, IDX_DIM), lambda i, j: (i, 0, 0)),
                  pl.BlockSpec((1, s, DSA_KV_LATENT), lambda i, j: (i, 0, 0)),
                  pl.BlockSpec((1, DSA_KV_LATENT, s), lambda i, j: (i, 0, 0)),
                  _full(wuqt.shape), _full(wqit.shape), _full(wuvt.shape)],
        out_specs=pl.BlockSpec((1, nq, DSA_HEADS * DSA_HEAD_DIM), lambda i, j: (i, j, 0)),
        compiler_params=_cparams("arbitrary", "arbitrary"),
        name="dsa",
    )(cqt, wit, ki, ckv, ckvt, wuqt, wqit, wuvt)


def _head_ones(n, dtype):
    r_i = lax.broadcasted_iota(jnp.int32, (n, n), 0) // RWKV_HEAD_DIM
    c_i = lax.broadcasted_iota(jnp.int32, (n, n), 1) // RWKV_HEAD_DIM
    return jnp.where(r_i == c_i, 1.0, 0.0).astype(dtype)


def _split_bf16(t):
    hi = t.astype(BF16)
    lo = (t - hi.astype(F32)).astype(BF16)
    return hi, lo


def _rwkv_kernel(z_ref, mu_ref, w0_ref, wup_ref, a0_ref, aup_ref, gup_ref, kk_ref, ka_ref, rk_ref, lng_ref, lnb_ref,
                 o_ref, st_s, prev_s, step_s, vb_s, y_s, bonus_s, gate_s):
    c = pl.program_id(1)
    n_g, tc, _ = z_ref.shape
    hd, wd_ = RWKV_HEAD_DIM, RWKV_WIDTH

    @pl.when(c == 0)
    def _():
        st_s[...] = jnp.zeros_like(st_s)
        prev_s[...] = jnp.zeros_like(prev_s)

    ones_f = _head_ones(wd_, F32)
    ones_b = _head_ones(wd_, BF16)
    diag = jnp.where(lax.broadcasted_iota(jnp.int32, (hd, wd_), 0)
                     == lax.broadcasted_iota(jnp.int32, (hd, wd_), 1) % hd, 1.0, 0.0).astype(F32)

    for g in range(n_g):
        z = z_ref[g]
        row = lax.broadcasted_iota(jnp.int32, z.shape, 0)
        z_prev = jnp.where(row == 0, prev_s[g], pltpu.roll(z, 1, 0))
        prev_s[g] = z[tc - 1:tc, :]
        z = z + mu_ref[...] * (z_prev - z)
        r = z[:, 0:wd_]
        k = z[:, wd_:2 * wd_]
        v = z[:, 2 * wd_:3 * wd_]
        o = 3 * wd_
        w_lora = z[:, o:o + RWKV_LORA_W]
        a_lora = z[:, o + RWKV_LORA_W:o + RWKV_LORA_W + RWKV_LORA_A]
        g_lora = z[:, o + RWKV_LORA_W + RWKV_LORA_A:]
        w_log = _log_sigmoid(w0_ref[...] + _dot_hi(jnp.tanh(w_lora), wup_ref[...])) - 0.5
        decay = jnp.exp(-jnp.exp(w_log))
        a = _sigmoid(a0_ref[...] + _dot_hi(a_lora, aup_ref[...]))
        gate_s[g] = _dot_hi(_sigmoid(g_lora), gup_ref[...])
        kk = k * kk_ref[...]
        kk = kk / jnp.maximum(jnp.sqrt(_dot_hi(kk * kk, ones_f)), 1e-12)
        k = k * (1.0 + (a - 1.0) * ka_ref[...])
        step_s[g, 0] = decay
        step_s[g, 1] = -kk
        step_s[g, 2] = kk * a
        step_s[g, 3] = k
        step_s[g, 4] = r
        bonus_s[g] = _dot_hi(r * k * rk_ref[...], ones_f) * v
        vd = (v[:, None, :] * diag[None]).astype(BF16).reshape(tc * hd, wd_)
        vb_s[g] = _dot(vd, ones_b)

    def step(t, carry):
        for g in range(n_g):
            st = st_s[g]
            w_t = step_s[g, 0, pl.ds(t, 1), :]
            kkn_t = step_s[g, 1, pl.ds(t, 1), :]
            b_t = step_s[g, 2, pl.ds(t, 1), :]
            k_t = step_s[g, 3, pl.ds(t, 1), :]
            r_t = step_s[g, 4, pl.ds(t, 1), :]
            hi, lo = _split_bf16(st * kkn_t)
            sa = _dot(hi, ones_b) + _dot(lo, ones_b)
            vb_t = vb_s[g, pl.ds(pl.multiple_of(t * hd, hd), hd), :]
            st = st * w_t + sa * b_t + vb_t * k_t
            st_s[g] = st
            yb = _dot((st * r_t).astype(BF16), ones_b)
            y_s[g, pl.ds(t, 1), :] = jnp.sum(yb * diag, axis=0, keepdims=True)
        return carry

    lax.fori_loop(0, tc, step, 0)

    for g in range(n_g):
        y = y_s[g]
        mean = _dot_hi(y, ones_f) * (1.0 / hd)
        yc = y - mean
        var = _dot_hi(yc * yc, ones_f) * (1.0 / hd)
        yn = yc * lax.rsqrt(var + RWKV_GN_EPS) * lng_ref[...] + lnb_ref[...]
        o_ref[g] = ((yn + bonus_s[g]) * gate_s[g]).astype(o_ref.dtype)


def _rwkv(zr, mu, w0, w_up, a0, a_up, g_up, k_k, k_a, r_k, ln_g, ln_b, n_g, tc):
    b, s, zin = zr.shape
    wd_ = RWKV_WIDTH
    vec = lambda p: p.reshape(1, -1)
    params = [vec(mu), vec(w0), w_up, vec(a0), a_up, g_up, vec(k_k), vec(k_a), vec(r_k), vec(ln_g), vec(ln_b)]
    return pl.pallas_call(
        _rwkv_kernel,
        out_shape=jax.ShapeDtypeStruct((b, s, wd_), BF16),
        grid=(b // n_g, s // tc),
        in_specs=[pl.BlockSpec((n_g, tc, zin), lambda i, c: (i, c, 0))] + [_full(p.shape) for p in params],
        out_specs=pl.BlockSpec((n_g, tc, wd_), lambda i, c: (i, c, 0)),
        scratch_shapes=[pltpu.VMEM((n_g, RWKV_HEAD_DIM, wd_), F32), pltpu.VMEM((n_g, 1, zin), F32),
                        pltpu.VMEM((n_g, 5, tc, wd_), F32), pltpu.VMEM((n_g, tc * RWKV_HEAD_DIM, wd_), F32),
                        pltpu.VMEM((n_g, tc, wd_), F32), pltpu.VMEM((n_g, tc, wd_), F32),
                        pltpu.VMEM((n_g, tc, wd_), F32)],
        compiler_params=_cparams("arbitrary", "arbitrary"),
        name="rwkv",
    )(zr, *params)


def _softcap(t):
    return GATE_SOFTCAP * jnp.tanh(t / GATE_SOFTCAP)


def _mlstm_kernel(zm_ref, mkt_ref, mvt_ref, mgt_ref, cw_row_ref, cb_row_ref, cw_col_ref, cb_col_ref,
                  bg_row_ref, bg_col_ref, ng_ref, o_ref, q_s, k_s, kt_s, gc_s, gr_s, yt_s):
    s = zm_ref.shape[1]
    nh, dk, dv, lc = MLSTM_HEADS, MLSTM_QK_DIM, MLSTM_V_DIM, MLSTM_CHUNK
    pair = 2 * lc

    qk = zm_ref[0, :, 0:2 * MLSTM_QK]
    pos_r = lax.broadcasted_iota(jnp.int32, qk.shape, 0)
    acc = cb_row_ref[...] + qk * cw_row_ref[CONV_WIDTH - 1:CONV_WIDTH, :]
    for r in range(1, CONV_WIDTH):
        sh = jnp.where(pos_r >= r, pltpu.roll(qk, r, 0), 0.0)
        acc = acc + sh * cw_row_ref[CONV_WIDTH - 1 - r:CONV_WIDTH - r, :]
    acc = _silu(acc)
    q_s[...] = acc[:, :MLSTM_QK] * dk ** -0.5
    k_s[...] = acc[:, MLSTM_QK:]
    kt = mkt_ref[0]
    pos_c = lax.broadcasted_iota(jnp.int32, kt.shape, 1)
    acc_t = cb_col_ref[...] + kt * cw_col_ref[:, CONV_WIDTH - 1:CONV_WIDTH]
    for r in range(1, CONV_WIDTH):
        sh = jnp.where(pos_c >= r, pltpu.roll(kt, r, 1), 0.0)
        acc_t = acc_t + sh * cw_col_ref[:, CONV_WIDTH - 1 - r:CONV_WIDTH - r]
    kt_s[...] = _silu(acc_t)

    gcol = _softcap(zm_ref[0, :, 2 * MLSTM_QK + 2 * MLSTM_V:] + bg_row_ref[...])
    lane = lax.broadcasted_iota(jnp.int32, gcol.shape, 1)
    gc_s[...] = jnp.where(lane < nh, gcol, _log_sigmoid(gcol))
    grow = _softcap(mgt_ref[0] + bg_col_ref[...])
    sub = lax.broadcasted_iota(jnp.int32, grow.shape, 0)
    gr_s[...] = jnp.where(sub < nh, grow, _log_sigmoid(grow))

    r_i = lax.broadcasted_iota(jnp.int32, (lc, lc), 0)
    c_i = lax.broadcasted_iota(jnp.int32, (lc, lc), 1)
    tri = jnp.where(c_i <= r_i, 1.0, 0.0).astype(F32)
    causal_t = r_i <= c_i
    ones_rows = jnp.ones((8, lc), F32)

    def chunk_pair(p, carry):
        base = pl.multiple_of(p * pair, pair)
        gr_slab = gr_s[:, pl.ds(base, pair)]
        kt_slab = kt_s[:, pl.ds(base, pair)]
        vt_slab = mvt_ref[0, :, pl.ds(base, pair)]
        outs = [[] for _ in range(nh)]
        for sc in range(2):
            r0 = base + sc * lc
            gcc = gc_s[pl.ds(r0, lc), :]
            grc = gr_slab[:, sc * lc:(sc + 1) * lc]
            bcum_col = _dot_hi(tri, gcc)
            bcum_row = _dot_nt_hi(grc, tri)
            qc = q_s[pl.ds(r0, lc), :]
            kc = k_s[pl.ds(r0, lc), :]
            new_carry = []
            for hd in range(nh):
                c_aug, m_prev = carry[hd]
                bc_c = bcum_col[:, nh + hd:nh + hd + 1]
                li_c = gcc[:, hd:hd + 1]
                bc_r = bcum_row[nh + hd:nh + hd + 1, :]
                li_r = grc[hd:hd + 1, :]
                gtot = bc_r[:, lc - 1:lc]
                d_t = jnp.where(causal_t, bc_r - bc_c + li_c, NEG_BIG)
                m_inter = bc_r + m_prev
                m_t = jnp.maximum(m_inter, jnp.max(d_t, axis=0, keepdims=True))
                q_h = qc[:, hd * dk:(hd + 1) * dk].astype(BF16)
                k_h = kc[:, hd * dk:(hd + 1) * dk].astype(BF16)
                kt_h = kt_slab[hd * dk:(hd + 1) * dk, sc * lc:(sc + 1) * lc].astype(BF16)
                vt_h = vt_slab[hd * dv:(hd + 1) * dv, sc * lc:(sc + 1) * lc]
                vt_aug = jnp.concatenate([vt_h, ones_rows], axis=0)
                s_t = _dot_nt(k_h, q_h)
                w_t = jnp.exp(d_t - m_t) * s_t
                s_inter = jnp.exp(m_inter - m_t)
                numden = _dot(vt_aug.astype(BF16), w_t.astype(BF16)) + s_inter * _dot_nt(c_aug.astype(BF16), q_h)
                den = numden[dv:dv + 1, :]
                outs[hd].append(numden[:dv, :] / jnp.maximum(jnp.abs(den), jnp.exp(-m_t)))
                a_log = gtot - bc_r + li_r
                a_max = jnp.max(a_log, axis=-1, keepdims=True)
                a_w = jnp.exp(a_log - a_max)
                kvn = _dot_nt((vt_aug * a_w).astype(BF16), kt_h)
                m_new = jnp.maximum(gtot + m_prev, a_max)
                s_old = jnp.exp(gtot + m_prev - m_new)
                s_new = jnp.exp(a_max - m_new)
                new_carry.append((s_old * c_aug + s_new * kvn, m_new))
            carry = tuple(new_carry)
        for hd in range(nh):
            yt_s[hd * dv:(hd + 1) * dv, pl.ds(base, pair)] = jnp.concatenate(outs[hd], axis=-1)
        return carry

    init = tuple((jnp.zeros((dv + 8, dk), F32), jnp.full((1, 1), NEG_BIG, F32)) for _ in range(nh))
    lax.fori_loop(0, s // pair, chunk_pair, init)

    parts = []
    for hd in range(nh):
        blk = yt_s[hd * dv:(hd + 1) * dv, :]
        parts.append(blk * lax.rsqrt(jnp.mean(blk * blk, axis=0, keepdims=True) + EPS))
    y = jnp.concatenate(parts, axis=0).T
    o_gate = _sigmoid(zm_ref[0, :, 2 * MLSTM_QK + MLSTM_V:2 * MLSTM_QK + 2 * MLSTM_V])
    o_ref[0] = (y * ng_ref[...] * o_gate).astype(o_ref.dtype)


def _mlstm(zm, mkt, mvt, mgt, conv_w, conv_b, b_i, b_f, norm_g):
    b, s, _ = zm.shape
    cw_col = conv_w[:, MLSTM_QK:].T
    cb_col = conv_b[MLSTM_QK:].reshape(-1, 1)
    bg = jnp.concatenate([b_i, b_f])
    bg_row = jnp.pad(bg, (0, LANES - 2 * MLSTM_HEADS)).reshape(1, LANES)
    bg_col = bg.reshape(-1, 1)
    per_b = lambda r, c: pl.BlockSpec((1, r, c), lambda i: (i, 0, 0))
    return pl.pallas_call(
        _mlstm_kernel,
        out_shape=jax.ShapeDtypeStruct((b, s, MLSTM_V), BF16),
        grid=(b,),
        in_specs=[per_b(s, M_COLS), per_b(MLSTM_QK, s), per_b(MLSTM_V, s), per_b(8, s),
                  _full(conv_w.shape), _full((1, 2 * MLSTM_QK)), _full(cw_col.shape), _full(cb_col.shape),
                  _full((1, LANES)), _full((8, 1)), _full((1, MLSTM_V))],
        out_specs=per_b(s, MLSTM_V),
        scratch_shapes=[pltpu.VMEM((s, MLSTM_QK), F32), pltpu.VMEM((s, MLSTM_QK), F32), pltpu.VMEM((MLSTM_QK, s), F32),
                        pltpu.VMEM((s, LANES), F32), pltpu.VMEM((8, s), F32), pltpu.VMEM((MLSTM_V, s), F32)],
        compiler_params=_cparams("arbitrary"),
        name="mlstm",
    )(zm, mkt, mvt, mgt, conv_w, conv_b.reshape(1, -1), cw_col, cb_col, bg_row, bg_col, norm_g.reshape(1, -1))


def _merge_kernel(x_ref, g_ref, y0_ref, y1_ref, y2_ref, y3_ref, wg_ref, wb_ref, wo_ref, o_ref):
    x = x_ref[...]
    d = x.shape[1]
    h = _rms_rows(x, g_ref[...]).astype(BF16)
    merged = None
    for n, y_ref in enumerate((y0_ref, y1_ref, y2_ref, y3_ref)):
        gate = _sigmoid(_dot(h, wg_ref[:, n * d:(n + 1) * d]))
        term = gate * _dot(y_ref[...], wb_ref[n])
        merged = term if merged is None else merged + term
    o_ref[...] = x + _dot(merged.astype(BF16), wo_ref[...])


def _merge(x2, g, ys, w_gate, w_branch, w_out, tm):
    t, d = x2.shape
    row = lambda w: pl.BlockSpec((tm, w), lambda i: (i, 0))
    return pl.pallas_call(
        _merge_kernel,
        out_shape=jax.ShapeDtypeStruct((t, d), F32),
        grid=(t // tm,),
        in_specs=[row(d), _full((1, d))] + [row(BRANCH_WIDTH)] * N_BRANCH
                 + [_full(w_gate.shape), _full(w_branch.shape), _full(w_out.shape)],
        out_specs=row(d),
        compiler_params=_cparams("arbitrary"),
        name="merge",
    )(x2, g.reshape(1, d), *ys, w_gate, w_branch, w_out)


def _ffn_kernel(x_ref, g_ref, w1_ref, w3_ref, w2_ref, o_ref):
    x = x_ref[...]
    h = _rms_rows(x, g_ref[...]).astype(BF16)
    u = _silu(_dot(h, w1_ref[...])) * _dot(h, w3_ref[...])
    o_ref[...] = x + _dot(u.astype(BF16), w2_ref[...])


def _ffn(x2, g, w1, w3, w2, tm):
    t, d = x2.shape
    row = pl.BlockSpec((tm, d), lambda i: (i, 0))
    return pl.pallas_call(
        _ffn_kernel,
        out_shape=jax.ShapeDtypeStruct((t, d), F32),
        grid=(t // tm,),
        in_specs=[row, _full((1, d)), _full(w1.shape), _full(w3.shape), _full(w2.shape)],
        out_specs=row,
        compiler_params=_cparams("arbitrary"),
        name="ffn",
    )(x2, g.reshape(1, d), w1, w3, w2)


def _moe_kernel(x_ref, g_ref, wr_ref, w1_ref, w3_ref, w2_ref, o_ref, h_s, gate_s, acc_s):
    e = pl.program_id(1)
    c = pl.program_id(2)
    n_e = pl.num_programs(1)
    n_c = pl.num_programs(2)

    @pl.when((e == 0) & (c == 0))
    def _():
        x = x_ref[...]
        h = _rms_rows(x, g_ref[...])
        h_s[...] = h.astype(BF16)
        logits = _dot_hi(h, wr_ref[...])
        lane = lax.broadcasted_iota(jnp.int32, logits.shape, 1)
        logits = jnp.where(lane < N_EXPERTS, logits, -jnp.inf)
        v1 = jnp.max(logits, axis=-1, keepdims=True)
        i1 = jnp.min(jnp.where(logits == v1, lane, LANES), axis=-1, keepdims=True)
        rest = jnp.where(lane == i1, -jnp.inf, logits)
        v2 = jnp.max(rest, axis=-1, keepdims=True)
        i2 = jnp.min(jnp.where(rest == v2, lane, LANES), axis=-1, keepdims=True)
        e2 = jnp.exp(v2 - v1)
        p1 = 1.0 / (1.0 + e2)
        p2 = e2 / (1.0 + e2)
        gate_s[...] = jnp.where(lane == i1, p1, 0.0) + jnp.where(lane == i2, p2, 0.0)
        acc_s[...] = jnp.zeros_like(acc_s)

    h = h_s[...]
    lane = lax.broadcasted_iota(jnp.int32, gate_s.shape, 1)
    g_e = jnp.sum(jnp.where(lane == e, gate_s[...], 0.0), axis=-1, keepdims=True)
    u = _silu(_dot(h, w1_ref[0])) * _dot(h, w3_ref[0])
    acc_s[...] += g_e * _dot(u.astype(BF16), w2_ref[0])

    @pl.when((e == n_e - 1) & (c == n_c - 1))
    def _():
        o_ref[...] = x_ref[...] + acc_s[...]


def _moe(x2, g, router, w1, w3, w2, tm, tf):
    t, d = x2.shape
    n_e, _, f = w1.shape
    router_p = jnp.pad(router, ((0, 0), (0, LANES - n_e)))
    row = pl.BlockSpec((tm, d), lambda i, e, c: (i, 0))
    return pl.pallas_call(
        _moe_kernel,
        out_shape=jax.ShapeDtypeStruct((t, d), F32),
        grid=(t // tm, n_e, f // tf),
        in_specs=[row, pl.BlockSpec((1, d), lambda i, e, c: (0, 0)), pl.BlockSpec((d, LANES), lambda i, e, c: (0, 0)),
                  pl.BlockSpec((1, d, tf), lambda i, e, c: (e, 0, c)),
                  pl.BlockSpec((1, d, tf), lambda i, e, c: (e, 0, c)),
                  pl.BlockSpec((1, tf, d), lambda i, e, c: (e, c, 0))],
        out_specs=row,
        scratch_shapes=[pltpu.VMEM((tm, d), BF16), pltpu.VMEM((tm, LANES), F32), pltpu.VMEM((tm, d), F32)],
        compiler_params=_cparams("arbitrary", "arbitrary", "arbitrary"),
        name="moe",
    )(x2, g.reshape(1, d), router_p, w1, w3, w2)


def _final_norm_kernel(x_ref, g_ref, o_ref):
    o_ref[...] = _rms_rows(x_ref[...], g_ref[...])


def _final_norm(x2, g, tm):
    t, d = x2.shape
    row = pl.BlockSpec((tm, d), lambda i: (i, 0))
    return pl.pallas_call(
        _final_norm_kernel,
        out_shape=jax.ShapeDtypeStruct((t, d), F32),
        grid=(t // tm,),
        in_specs=[row, _full((1, d))],
        out_specs=row,
        compiler_params=_cparams("arbitrary"),
        name="final_norm",
    )(x2, g.reshape(1, d))


def kernel(x, norm_mix_g, w_in, dsa_g_cq, dsa_g_ckv, dsa_g_kidx, dsa_w_uq, dsa_w_qidx, dsa_w_uv, rwkv_mu, rwkv_w0, rwkv_w_up, rwkv_a0, rwkv_a_up, rwkv_g_up, rwkv_k_k, rwkv_k_a, rwkv_r_k, rwkv_ln_g, rwkv_ln_b, mlstm_conv_w, mlstm_conv_b, mlstm_b_i, mlstm_b_f, mlstm_norm_g, fox_b_f, w_branch, w_out, norm_ffn_g, ffn_w1, ffn_w3, ffn_w2, moe_router, moe_w1, moe_w3, moe_w2, final_norm_g):
    b, s, d = x.shape
    depth = w_in.shape[0]
    t = b * s
    tm = min(512, s)
    for l in range(depth):
        wts = _inproj_weights(w_in[l])
        (zr, zm, fq, fk, fv, ckv, ki, fcol, cqt, ckvt, wit, frow, mkt, mvt, mgt) = _inproj(
            x, norm_mix_g[l], wts, dsa_g_cq[l], dsa_g_ckv[l], dsa_g_kidx[l], fox_b_f[l], tm)
        y_fox = _fox(fq, fk, fv, fcol, frow, min(512, s))
        y_dsa = _dsa(cqt, wit, ki, ckv, ckvt, dsa_w_uq[l], dsa_w_qidx[l], dsa_w_uv[l]).reshape(t, BRANCH_WIDTH)
        y_rwkv = _rwkv(zr, rwkv_mu[l], rwkv_w0[l], rwkv_w_up[l], rwkv_a0[l], rwkv_a_up[l], rwkv_g_up[l], rwkv_k_k[l],
                       rwkv_k_a[l], rwkv_r_k[l], rwkv_ln_g[l], rwkv_ln_b[l], min(4, b), 64).reshape(t, BRANCH_WIDTH)
        y_mlstm = _mlstm(zm, mkt, mvt, mgt, mlstm_conv_w[l], mlstm_conv_b[l], mlstm_b_i[l], mlstm_b_f[l],
                         mlstm_norm_g[l]).reshape(t, BRANCH_WIDTH)
        ys = (y_dsa, y_rwkv, y_mlstm, y_fox.reshape(t, BRANCH_WIDTH))
        x2 = _merge(x.reshape(t, d), norm_mix_g[l], ys, wts['w_gate'], w_branch[l].astype(BF16),
                    w_out[l].astype(BF16), tm)
        j = l // 2
        if l % 2 == 0:
            x2 = _ffn(x2, norm_ffn_g[l], ffn_w1[j].astype(BF16), ffn_w3[j].astype(BF16), ffn_w2[j].astype(BF16), tm)
        else:
            x2 = _moe(x2, norm_ffn_g[l], moe_router[j], moe_w1[j].astype(BF16), moe_w3[j].astype(BF16),
                      moe_w2[j].astype(BF16), tm, moe_w1.shape[-1] // 2)
        x = x2.reshape(b, s, d)
    return _final_norm(x.reshape(t, d), final_norm_g, tm).reshape(b, s, d)
```

```python
import functools

import jax
import jax.numpy as jnp
from jax import lax
from jax.experimental import pallas as pl
from jax.experimental.pallas import tpu as pltpu

F32 = jnp.float32
BF16 = jnp.bfloat16
HIGHEST = lax.Precision.HIGHEST

EPS = 1e-6
NEG_BIG = -1e30

N_BRANCH = 4
BRANCH_WIDTH = 256
DSA_HEADS = 4
DSA_HEAD_DIM = 64
DSA_Q_LATENT = 128
DSA_KV_LATENT = 128
IDX_HEADS = 8
IDX_DIM = 32
TOPK_MAX = 256
DSA_Q_BLOCK = 128

RWKV_HEADS = 4
RWKV_HEAD_DIM = 64
RWKV_WIDTH = RWKV_HEADS * RWKV_HEAD_DIM
RWKV_LORA_W = 64
RWKV_LORA_A = 64
RWKV_LORA_G = 128
RWKV_GN_EPS = 64e-5
RWKV_IN = 3 * RWKV_WIDTH + RWKV_LORA_W + RWKV_LORA_A + RWKV_LORA_G

MLSTM_HEADS = 4
MLSTM_QK_DIM = 32
MLSTM_V_DIM = 64
MLSTM_CHUNK = 64
CONV_WIDTH = 4
GATE_SOFTCAP = 15.0
MLSTM_QK = MLSTM_HEADS * MLSTM_QK_DIM
MLSTM_V = MLSTM_HEADS * MLSTM_V_DIM

FOX_HEADS = 4
FOX_HEAD_DIM = 64
FOX_WIDTH = FOX_HEADS * FOX_HEAD_DIM

N_EXPERTS = 8

VMEM_LIMIT_BYTES = 56 * 1024 * 1024
LANES = 128


def _cparams(*sem):
    return pltpu.CompilerParams(dimension_semantics=sem, vmem_limit_bytes=VMEM_LIMIT_BYTES)


def _dot(a, b):
    return jnp.dot(a, b, preferred_element_type=F32)


def _dot_hi(a, b):
    return jnp.dot(a, b, preferred_element_type=F32, precision=HIGHEST)


def _dot_nt(a, b):
    return lax.dot_general(a, b, (((1,), (1,)), ((), ())), preferred_element_type=F32)


def _dot_nt_hi(a, b):
    return lax.dot_general(a, b, (((1,), (1,)), ((), ())), preferred_element_type=F32, precision=HIGHEST)


def _log_sigmoid(t):
    return jnp.minimum(t, 0.0) - jnp.log1p(jnp.exp(-jnp.abs(t)))


def _sigmoid(t):
    return 1.0 / (1.0 + jnp.exp(-t))


def _silu(t):
    return t * _sigmoid(t)


def _rms_rows(t, g_row):
    return t * lax.rsqrt(jnp.mean(t * t, axis=-1, keepdims=True) + EPS) * g_row


def _rms_cols(t, g_col):
    return t * lax.rsqrt(jnp.mean(t * t, axis=0, keepdims=True) + EPS) * g_col


def _full(shape):
    n = len(shape)
    return pl.BlockSpec(shape, lambda *_: (0,) * n)


T_CQ = 0
T_CKV = T_CQ + DSA_Q_LATENT
T_WIDX = T_CKV + DSA_KV_LATENT
T_FOXF = T_WIDX + IDX_HEADS
T_MK = T_FOXF + 8
T_MV = T_MK + MLSTM_QK
T_MG = T_MV + MLSTM_V
T_ROWS = T_MG + 8
D_COLS = 3 * LANES
M_COLS = 2 * MLSTM_QK + 2 * MLSTM_V + LANES


def _inproj_kernel(x_ref, g_ref, wr_ref, wm_ref, wf_ref, wd_ref, wt_ref,
                   gcq_ref, gckv_col_ref, gckv_row_ref, gki_ref, bf_row_ref, bf_col_ref,
                   zr_ref, zm_ref, q_ref, k_ref, v_ref, ckv_ref, ki_ref, fcol_ref,
                   cqt_ref, ckvt_ref, wit_ref, frow_ref, mkt_ref, mvt_ref, mgt_ref,
                   carry_row, carry_col):
    j = pl.program_id(1)
    tm = x_ref.shape[1]

    @pl.when(j == 0)
    def _():
        carry_row[...] = jnp.zeros_like(carry_row)
        carry_col[...] = jnp.zeros_like(carry_col)

    x = x_ref[0]
    h = _rms_rows(x, g_ref[...]).astype(BF16)

    zr_ref[0] = _dot(h, wr_ref[...])
    zm_ref[0] = _dot(h, wm_ref[...])

    zf = _dot(h, wf_ref[...])
    for hd in range(FOX_HEADS):
        lo = hd * FOX_HEAD_DIM
        q_ref[0, hd] = (zf[:, lo:lo + FOX_HEAD_DIM] * FOX_HEAD_DIM ** -0.5).astype(BF16)
        k_ref[0, hd] = zf[:, FOX_WIDTH + lo:FOX_WIDTH + lo + FOX_HEAD_DIM].astype(BF16)
        v_ref[0, hd] = zf[:, 2 * FOX_WIDTH + lo:2 * FOX_WIDTH + lo + FOX_HEAD_DIM].astype(BF16)

    zd = _dot(h, wd_ref[...])
    ckv_ref[0] = _rms_rows(zd[:, :LANES], gckv_row_ref[...]).astype(BF16)
    ki_pieces = [p.astype(F32) for p in _split3_bf16(_rms_rows(zd[:, LANES:LANES + IDX_DIM], gki_ref[...]))]
    ki_pad = jnp.zeros((tm, IDX_K_COLS - len(IDX_SPLIT_K) * IDX_DIM), F32)
    ki_ref[0] = jnp.concatenate([ki_pieces[kp] for kp, _ in IDX_SPLIT_K] + [ki_pad], axis=1).astype(BF16)
    lf_col = _log_sigmoid(zd[:, 2 * LANES:] + bf_row_ref[...])
    r_i = lax.broadcasted_iota(jnp.int32, (tm, tm), 0)
    c_i = lax.broadcasted_iota(jnp.int32, (tm, tm), 1)
    tri = jnp.where(c_i <= r_i, 1.0, 0.0).astype(F32)
    cum_col = _dot_hi(tri, lf_col) + carry_row[...]
    fcol_ref[0] = cum_col
    carry_row[...] = cum_col[tm - 1:tm, :]

    zt = _dot_nt(wt_ref[...], h)
    cqt_ref[0] = _rms_cols(zt[T_CQ:T_CKV], gcq_ref[...])
    ckvt_ref[0] = _rms_cols(zt[T_CKV:T_WIDX], gckv_col_ref[...]).astype(BF16)
    wit_ref[0] = zt[T_WIDX:T_FOXF] * IDX_HEADS ** -0.5
    lf_row = _log_sigmoid(zt[T_FOXF:T_MK] + bf_col_ref[...])
    cum_row = _dot_nt_hi(lf_row, tri) + carry_col[:, 0:1]
    frow_ref[0] = cum_row
    carry_col[...] = jnp.broadcast_to(cum_row[:, tm - 1:tm], carry_col.shape)
    mkt_ref[0] = zt[T_MK:T_MV]
    mvt_ref[0] = zt[T_MV:T_MG]
    mgt_ref[0] = zt[T_MG:T_ROWS]


def _inproj_weights(w_in_l):
    d = w_in_l.shape[0]
    o = 0
    w_cq = w_in_l[:, o:o + DSA_Q_LATENT]; o += DSA_Q_LATENT
    w_ckv = w_in_l[:, o:o + DSA_KV_LATENT]; o += DSA_KV_LATENT
    w_kidx = w_in_l[:, o:o + IDX_DIM]; o += IDX_DIM
    w_widx = w_in_l[:, o:o + IDX_HEADS]; o += IDX_HEADS
    w_rwkv = w_in_l[:, o:o + RWKV_IN]; o += RWKV_IN
    w_mq = w_in_l[:, o:o + MLSTM_QK]; o += MLSTM_QK
    w_mk = w_in_l[:, o:o + MLSTM_QK]; o += MLSTM_QK
    w_mv = w_in_l[:, o:o + MLSTM_V]; o += MLSTM_V
    w_mo = w_in_l[:, o:o + MLSTM_V]; o += MLSTM_V
    w_mi = w_in_l[:, o:o + MLSTM_HEADS]; o += MLSTM_HEADS
    w_mf = w_in_l[:, o:o + MLSTM_HEADS]; o += MLSTM_HEADS
    w_fox = w_in_l[:, o:o + 3 * FOX_WIDTH]; o += 3 * FOX_WIDTH
    w_ff = w_in_l[:, o:o + FOX_HEADS]; o += FOX_HEADS
    w_gate = w_in_l[:, o:]

    def padc(w, n):
        return jnp.pad(w, ((0, 0), (0, n - w.shape[1])))

    w_d = jnp.concatenate([w_ckv, padc(w_kidx, LANES), padc(w_ff, LANES)], axis=1)
    w_m = jnp.concatenate([w_mq, w_mk, w_mv, w_mo, padc(jnp.concatenate([w_mi, w_mf], axis=1), LANES)], axis=1)
    w_t = jnp.concatenate([w_cq, w_ckv, w_widx, padc(w_ff, 8), w_mk, w_mv, w_mi, w_mf], axis=1).T
    assert w_t.shape == (T_ROWS, d)
    cast = lambda w: w.astype(BF16)
    return dict(w_r=cast(w_rwkv), w_m=cast(w_m), w_f=cast(w_fox), w_d=cast(w_d), w_t=cast(w_t), w_gate=cast(w_gate))


def _inproj(x, g, wts, g_cq, g_ckv, g_kidx, fox_b_f, tm):
    b, s, d = x.shape
    nj = s // tm
    bf_row = jnp.pad(fox_b_f, (0, LANES - FOX_HEADS)).reshape(1, LANES)
    bf_col = jnp.pad(fox_b_f, (0, 8 - FOX_HEADS)).reshape(8, 1)
    row = lambda w: pl.BlockSpec((1, tm, w), lambda i, j: (i, j, 0))
    head = pl.BlockSpec((1, FOX_HEADS, tm, FOX_HEAD_DIM), lambda i, j: (i, 0, j, 0))
    col = lambda r: pl.BlockSpec((1, r, tm), lambda i, j: (i, 0, j))
    out_shape = [
        jax.ShapeDtypeStruct((b, s, RWKV_IN), F32),
        jax.ShapeDtypeStruct((b, s, M_COLS), F32),
        jax.ShapeDtypeStruct((b, FOX_HEADS, s, FOX_HEAD_DIM), BF16),
        jax.ShapeDtypeStruct((b, FOX_HEADS, s, FOX_HEAD_DIM), BF16),
        jax.ShapeDtypeStruct((b, FOX_HEADS, s, FOX_HEAD_DIM), BF16),
        jax.ShapeDtypeStruct((b, s, DSA_KV_LATENT), BF16),
        jax.ShapeDtypeStruct((b, s, IDX_K_COLS), BF16),
        jax.ShapeDtypeStruct((b, s, LANES), F32),
        jax.ShapeDtypeStruct((b, DSA_Q_LATENT, s), F32),
        jax.ShapeDtypeStruct((b, DSA_KV_LATENT, s), BF16),
        jax.ShapeDtypeStruct((b, IDX_HEADS, s), F32),
        jax.ShapeDtypeStruct((b, 8, s), F32),
        jax.ShapeDtypeStruct((b, MLSTM_QK, s), F32),
        jax.ShapeDtypeStruct((b, MLSTM_V, s), F32),
        jax.ShapeDtypeStruct((b, 8, s), F32),
    ]
    out_specs = [row(RWKV_IN), row(M_COLS), head, head, head, row(DSA_KV_LATENT), row(IDX_K_COLS), row(LANES),
                 col(DSA_Q_LATENT), col(DSA_KV_LATENT), col(IDX_HEADS), col(8), col(MLSTM_QK), col(MLSTM_V), col(8)]
    in_specs = [row(d), _full((1, d)), _full(wts['w_r'].shape), _full(wts['w_m'].shape), _full(wts['w_f'].shape),
                _full(wts['w_d'].shape), _full(wts['w_t'].shape),
                _full((DSA_Q_LATENT, 1)), _full((DSA_KV_LATENT, 1)), _full((1, DSA_KV_LATENT)), _full((1, IDX_DIM)),
                _full((1, LANES)), _full((8, 1))]
    return pl.pallas_call(
        _inproj_kernel,
        out_shape=out_shape,
        grid=(b, nj),
        in_specs=in_specs,
        out_specs=out_specs,
        scratch_shapes=[pltpu.VMEM((1, LANES), F32), pltpu.VMEM((8, LANES), F32)],
        compiler_params=_cparams("arbitrary", "arbitrary"),
        name="inproj",
    )(x, g.reshape(1, d), wts['w_r'], wts['w_m'], wts['w_f'], wts['w_d'], wts['w_t'],
      g_cq.reshape(-1, 1), g_ckv.reshape(-1, 1), g_ckv.reshape(1, -1), g_kidx.reshape(1, -1), bf_row, bf_col)


def _fox_kernel(q_ref, k_ref, v_ref, fcol_ref, frow_ref, o_ref):
    qi = pl.program_id(1)
    tq = q_ref.shape[2]
    outs = []
    for hd in range(FOX_HEADS):
        q = q_ref[0, hd]
        fq = fcol_ref[0][:, hd:hd + 1]

        def scores(j):
            start = pl.multiple_of(j * tq, tq)
            k = k_ref[0, hd, pl.ds(start, tq), :]
            v = v_ref[0, hd, pl.ds(start, tq), :]
            fk = frow_ref[0, hd:hd + 1, pl.ds(start, tq)]
            return _dot_nt(q, k) + (fq - fk), v

        def update(carry, sc, v):
            m, l, acc = carry
            m_new = jnp.maximum(m, jnp.max(sc, axis=-1, keepdims=True))
            alpha = jnp.exp(m - m_new)
            p = jnp.exp(sc - m_new)
            l = alpha * l + jnp.sum(p, axis=-1, keepdims=True)
            acc = alpha * acc + _dot(p.astype(BF16), v)
            return m_new, l, acc

        def body(j, carry):
            sc, v = scores(j)
            return update(carry, sc, v)

        init = (jnp.full((tq, 1), NEG_BIG, F32), jnp.zeros((tq, 1), F32), jnp.zeros((tq, FOX_HEAD_DIM), F32))
        carry = lax.fori_loop(0, qi, body, init)
        sc, v = scores(qi)
        r_i = lax.broadcasted_iota(jnp.int32, (tq, tq), 0)
        c_i = lax.broadcasted_iota(jnp.int32, (tq, tq), 1)
        sc = jnp.where(c_i <= r_i, sc, NEG_BIG)
        m, l, acc = update(carry, sc, v)
        outs.append(acc / l)
    o_ref[0] = jnp.concatenate(outs, axis=-1).astype(o_ref.dtype)


def _fox(q, k, v, fcol, frow, tq):
    b, nh, s, d = q.shape
    return pl.pallas_call(
        _fox_kernel,
        out_shape=jax.ShapeDtypeStruct((b, s, nh * d), BF16),
        grid=(b, s // tq),
        in_specs=[
            pl.BlockSpec((1, nh, tq, d), lambda i, j: (i, 0, j, 0)),
            pl.BlockSpec((1, nh, s, d), lambda i, j: (i, 0, 0, 0)),
            pl.BlockSpec((1, nh, s, d), lambda i, j: (i, 0, 0, 0)),
            pl.BlockSpec((1, tq, LANES), lambda i, j: (i, j, 0)),
            pl.BlockSpec((1, 8, s), lambda i, j: (i, 0, 0)),
        ],
        out_specs=pl.BlockSpec((1, tq, nh * d), lambda i, j: (i, j, 0)),
        compiler_params=_cparams("arbitrary", "arbitrary"),
        name="fox",
    )(q, k, v, fcol, frow)


INT_MIN = -2 ** 31


IDX_SPLIT_K = ((0, 0), (0, 1), (1, 0), (0, 2), (1, 1), (2, 0))
IDX_K_COLS = 2 * LANES


def _split3_bf16(t):
    p0 = t.astype(BF16)
    r1 = t - p0.astype(F32)
    p1 = r1.astype(BF16)
    p2 = (r1 - p1.astype(F32)).astype(BF16)
    return p0, p1, p2


def _tree_sum(parts):
    while len(parts) > 1:
        parts = [parts[i] + parts[i + 1] for i in range(0, len(parts) - 1, 2)] + (parts[-1:] if len(parts) % 2 else [])
    return parts[0]


def _dsa_kernel(cqt_ref, wit_ref, ki6_ref, ckv_ref, ckvt_ref, wuqt_ref, wqit_ref, wuvt_ref, o_ref, key_s, *, ck):
    qb = pl.program_id(1)
    s = ki6_ref.shape[1]
    nq = cqt_ref.shape[2]
    topk = min(TOPK_MAX, s // 4)
    n_ck = (qb * nq + nq + ck - 1) // ck
    lat = DSA_KV_LATENT

    cq = cqt_ref[0]
    qi_t = _dot_hi(wqit_ref[...], cq)
    zeros_pad = jnp.zeros((IDX_K_COLS - len(IDX_SPLIT_K) * IDX_DIM, nq), BF16)
    q_blocks = []
    for hd in range(IDX_HEADS):
        pieces = _split3_bf16(qi_t[hd * IDX_DIM:(hd + 1) * IDX_DIM, :])
        q_blocks.append(jnp.concatenate([pieces[qp] for _, qp in IDX_SPLIT_K] + [zeros_pad], axis=0))
    q6 = jnp.concatenate(q_blocks, axis=1)
    wi = wit_ref[0] * IDX_DIM ** -0.5
    tpos = qb * nq + lax.broadcasted_iota(jnp.int32, (ck, nq), 1)
    row = lax.broadcasted_iota(jnp.int32, (ck, nq), 0)

    def chunk_start(c):
        return pl.multiple_of(c * ck, ck)

    def score_chunk(c, carry):
        c0 = chunk_start(c)
        ki6 = ki6_ref[0, pl.ds(c0, ck), :]
        score = None
        for hp in range(0, IDX_HEADS, 2):
            dots = _dot(ki6, q6[:, hp * nq:(hp + 2) * nq])
            for i in range(2):
                term = wi[hp + i:hp + i + 1, :] * jnp.maximum(dots[:, i * nq:(i + 1) * nq], 0.0)
                score = term if score is None else score + term
        bits = pltpu.bitcast(score, jnp.int32)
        key = jnp.where(bits < 0, bits ^ 0x7FFFFFFF, bits)
        key = jnp.where(score == 0.0, 0, key)
        key_s[pl.ds(c0, ck), :] = jnp.where(c0 + row <= tpos, key, INT_MIN)
        return carry

    n_pair = (n_ck + 1) // 2
    lax.fori_loop(0, n_pair, lambda j, carry: score_chunk(2 * j + 1, score_chunk(2 * j, carry)), 0)

    def count(mask_fn):
        def body(c, acc):
            c0 = chunk_start(c)
            ones = jnp.where(mask_fn(key_s[pl.ds(c0, ck), :], c0 + row), 1, 0)
            return acc + _tree_sum([ones[i * 8:(i + 1) * 8] for i in range(ck // 8)])
        acc = lax.fori_loop(0, n_ck, body, jnp.zeros((8, nq), jnp.int32))
        return jnp.sum(acc, axis=0, keepdims=True)

    def value_bit(i, carry):
        lo, n_lo = carry
        cand = lo + jnp.left_shift(jnp.int32(1), 31 - i)
        n_cand = count(lambda key, pos: key >= cand)
        ok = n_cand >= topk
        return jnp.where(ok, cand, lo), jnp.where(ok, n_cand, n_lo)

    thr, n_ge = lax.fori_loop(0, 32, value_bit, (jnp.full((1, nq), INT_MIN, jnp.int32),
                                                 jnp.full((1, nq), s, jnp.int32)))

    n_bits = s.bit_length()

    def tie_search():
        need = topk - count(lambda key, pos: key > thr)

        def index_bit(i, lo):
            cand = lo + jnp.left_shift(jnp.int32(1), n_bits - 1 - i)
            return jnp.where(count(lambda key, pos: (key == thr) & (pos < cand)) < need, cand, lo)

        return lax.fori_loop(0, n_bits, index_bit, jnp.zeros((1, nq), jnp.int32))

    surplus = jnp.max(jnp.where((n_ge > topk) & (thr > INT_MIN), 1, 0))
    last = lax.cond(surplus > 0, tie_search, lambda: jnp.full((1, nq), s, jnp.int32))

    q_t = (_dot(wuqt_ref[...], cq.astype(BF16)) * lat ** -0.5).astype(BF16)

    def attend(c, carry):
        c0 = chunk_start(c)
        key = key_s[pl.ds(c0, ck), :]
        pos = c0 + row
        sel = ((key > thr) | ((key == thr) & (pos <= last))) & (pos <= tpos)
        bias = jnp.where(sel, 0.0, NEG_BIG)
        ckv = ckv_ref[0, pl.ds(c0, ck), :]
        ckvt = ckvt_ref[0, :, pl.ds(c0, ck)]
        new = []
        for hd in range(DSA_HEADS):
            m, l, acc = carry[hd]
            lg = _dot(ckv, q_t[hd * lat:(hd + 1) * lat, :]) + bias
            m_new = jnp.maximum(m, jnp.max(lg, axis=0, keepdims=True))
            alpha = jnp.exp(m - m_new)
            p = jnp.exp(lg - m_new)
            new.append((m_new, alpha * l + jnp.sum(p, axis=0, keepdims=True),
                        alpha * acc + _dot(ckvt, p.astype(BF16))))
        return tuple(new)

    init = tuple((jnp.full((1, nq), NEG_BIG, F32), jnp.zeros((1, nq), F32), jnp.zeros((lat, nq), F32))
                 for _ in range(DSA_HEADS))
    final = lax.fori_loop(0, n_pair, lambda j, carry: attend(2 * j + 1, attend(2 * j, carry)), init)
    outs = [_dot(wuvt_ref[hd], (acc / l).astype(BF16)) for hd, (_, l, acc) in enumerate(final)]
    o_ref[0] = jnp.concatenate(outs, axis=0).T.astype(o_ref.dtype)


def _dsa(cqt, wit, ki6, ckv, ckvt, w_uq, w_qidx, w_uv, ck):
    b, c, s = cqt.shape
    nq = DSA_Q_BLOCK
    wuqt = w_uq.reshape(c, -1).T.astype(BF16)
    wqit = w_qidx.reshape(c, -1).T
    wuvt = jnp.transpose(w_uv, (0, 2, 1)).astype(BF16)
    return pl.pallas_call(
        functools.partial(_dsa_kernel, ck=ck),
        out_shape=jax.ShapeDtypeStruct((b, s, DSA_HEADS * DSA_HEAD_DIM), BF16),
        grid=(b, s // nq),
        in_specs=[pl.BlockSpec((1, c, nq), lambda i, j: (i, 0, j)),
                  pl.BlockSpec((1, IDX_HEADS, nq), lambda i, j: (i, 0, j)),
                  pl.BlockSpec((1, s, IDX_K_COLS), lambda i, j: (i, 0, 0)),
                  pl.BlockSpec((1, s, DSA_KV_LATENT), lambda i, j: (i, 0, 0)),
                  pl.BlockSpec((1, DSA_KV_LATENT, s), lambda i, j: (i, 0, 0)),
                  _full(wuqt.shape), _full(wqit.shape), _full(wuvt.shape)],
        out_specs=pl.BlockSpec((1, nq, DSA_HEADS * DSA_HEAD_DIM), lambda i, j: (i, j, 0)),
        scratch_shapes=[pltpu.VMEM((s, nq), jnp.int32)],
        compiler_params=_cparams("arbitrary", "arbitrary"),
        name="dsa",
    )(cqt, wit, ki6, ckv, ckvt, wuqt, wqit, wuvt)


Y_PAD = 8


def _head_ones(n, dtype):
    r_i = lax.broadcasted_iota(jnp.int32, (n, n), 0) // RWKV_HEAD_DIM
    c_i = lax.broadcasted_iota(jnp.int32, (n, n), 1) // RWKV_HEAD_DIM
    return jnp.where(r_i == c_i, 1.0, 0.0).astype(dtype)


def _split_bf16(t):
    hi = t.astype(BF16)
    lo = (t - hi.astype(F32)).astype(BF16)
    return hi, lo


def _rwkv_kernel(z_ref, mu_ref, w0_ref, wup_ref, a0_ref, aup_ref, gup_ref, kk_ref, ka_ref, rk_ref, lng_ref, lnb_ref,
                 o_ref, st_s, prev_s, step_s, y_s, bonus_s, gate_s, *, grp):
    c = pl.program_id(1)
    n_g, tc, _ = z_ref.shape
    hd, wd_ = RWKV_HEAD_DIM, RWKV_WIDTH

    @pl.when(c == 0)
    def _():
        st_s[...] = jnp.zeros_like(st_s)
        prev_s[...] = jnp.zeros_like(prev_s)

    ones_b = _head_ones(wd_, BF16)
    diag = jnp.where(lax.broadcasted_iota(jnp.int32, (hd, wd_), 0)
                     == lax.broadcasted_iota(jnp.int32, (hd, wd_), 1) % hd, 1.0, 0.0).astype(F32)

    def head_sum(t):
        hi, lo = _split_bf16(t)
        return _dot(hi, ones_b) + _dot(lo, ones_b)

    zs = []
    for g in range(n_g):
        z = z_ref[g]
        row = lax.broadcasted_iota(jnp.int32, z.shape, 0)
        z_prev = jnp.where(row == 0, prev_s[g], pltpu.roll(z, 1, 0))
        prev_s[g] = z[tc - 1:tc, :]
        zs.append(z + mu_ref[...] * (z_prev - z))
    z = jnp.concatenate(zs, axis=0)
    r = z[:, 0:wd_]
    k = z[:, wd_:2 * wd_]
    v = z[:, 2 * wd_:3 * wd_]
    o = 3 * wd_
    w_lora = z[:, o:o + RWKV_LORA_W]
    a_lora = z[:, o + RWKV_LORA_W:o + RWKV_LORA_W + RWKV_LORA_A]
    g_lora = z[:, o + RWKV_LORA_W + RWKV_LORA_A:]
    w_log = _log_sigmoid(w0_ref[...] + _dot_hi(jnp.tanh(w_lora), wup_ref[...])) - 0.5
    a = _sigmoid(a0_ref[...] + _dot_hi(a_lora, aup_ref[...]))
    gate_s[...] = _dot_hi(_sigmoid(g_lora), gup_ref[...])
    kk = k * kk_ref[...]
    kk = kk / jnp.maximum(jnp.sqrt(head_sum(kk * kk)), 1e-12)
    k = k * (1.0 + (a - 1.0) * ka_ref[...])
    step_s[0] = jnp.exp(-jnp.exp(w_log))
    step_s[1] = -kk
    step_s[2] = kk * a
    step_s[3] = k
    step_s[4] = r
    step_s[5] = v
    bonus_s[...] = head_sum(r * k * rk_ref[...]) * v

    n_grp = n_g // grp

    def rows(kind, g0, t):
        return jnp.concatenate(
            [jnp.broadcast_to(step_s[kind, pl.ds((g0 + i) * tc + t, 1), :], (hd, wd_)) for i in range(grp)], axis=0)

    def emit_y(st, g0, t_read, t_write):
        yb = _dot((st * rows(4, g0, t_read)).astype(BF16), ones_b) * diag_g
        for i in range(grp):
            tile = jnp.sum(yb[i * hd:(i + 1) * hd].reshape(hd // 8, 8, wd_), axis=0)
            y_s[pl.ds(Y_PAD + (g0 + i) * tc + t_write, 1), :] = jnp.sum(tile, axis=0, keepdims=True)

    def step(t, carry):
        for q in range(n_grp):
            g0 = q * grp
            st = st_s[q]
            sa = _dot((st * rows(1, g0, t)).astype(BF16), ones_b)
            emit_y(st, g0, jnp.maximum(t - 1, 0), t - 1)
            vb = _dot((rows(5, g0, t) * diag_g).astype(BF16), ones_b)
            st_s[q] = st * rows(0, g0, t) + sa * rows(2, g0, t) + vb * rows(3, g0, t)
        return carry

    diag_g = jnp.concatenate([diag] * grp, axis=0)
    lax.fori_loop(0, tc, step, 0)
    for q in range(n_grp):
        emit_y(st_s[q], q * grp, tc - 1, tc - 1)

    y = y_s[pl.ds(Y_PAD, n_g * tc), :]
    mean = head_sum(y) * (1.0 / hd)
    yc = y - mean
    var = head_sum(yc * yc) * (1.0 / hd)
    yn = yc * lax.rsqrt(var + RWKV_GN_EPS) * lng_ref[...] + lnb_ref[...]
    out = ((yn + bonus_s[...]) * gate_s[...]).astype(o_ref.dtype)
    for g in range(n_g):
        o_ref[g] = out[g * tc:(g + 1) * tc]


def _rwkv(zr, mu, w0, w_up, a0, a_up, g_up, k_k, k_a, r_k, ln_g, ln_b, n_g, grp, tc):
    b, s, zin = zr.shape
    wd_ = RWKV_WIDTH
    vec = lambda p: p.reshape(1, -1)
    params = [vec(mu), vec(w0), w_up, vec(a0), a_up, g_up, vec(k_k), vec(k_a), vec(r_k), vec(ln_g), vec(ln_b)]
    return pl.pallas_call(
        functools.partial(_rwkv_kernel, grp=grp),
        out_shape=jax.ShapeDtypeStruct((b, s, wd_), BF16),
        grid=(b // n_g, s // tc),
        in_specs=[pl.BlockSpec((n_g, tc, zin), lambda i, c: (i, c, 0))] + [_full(p.shape) for p in params],
        out_specs=pl.BlockSpec((n_g, tc, wd_), lambda i, c: (i, c, 0)),
        scratch_shapes=[pltpu.VMEM((n_g // grp, grp * RWKV_HEAD_DIM, wd_), F32), pltpu.VMEM((n_g, 1, zin), F32),
                        pltpu.VMEM((6, n_g * tc, wd_), F32), pltpu.VMEM((Y_PAD + n_g * tc, wd_), F32),
                        pltpu.VMEM((n_g * tc, wd_), F32), pltpu.VMEM((n_g * tc, wd_), F32)],
        compiler_params=_cparams("arbitrary", "arbitrary"),
        name="rwkv",
    )(zr, *params)


def _softcap(t):
    return GATE_SOFTCAP * jnp.tanh(t / GATE_SOFTCAP)


def _mlstm_kernel(zm_ref, mkt_ref, mvt_ref, mgt_ref, cw_row_ref, cb_row_ref, cw_col_ref, cb_col_ref,
                  bg_row_ref, bg_col_ref, ng_ref, o_ref, q_s, k_s, kt_s, gc_s, gr_s, yt_s):
    s = zm_ref.shape[1]
    nh, dk, dv, lc = MLSTM_HEADS, MLSTM_QK_DIM, MLSTM_V_DIM, MLSTM_CHUNK
    pair = 2 * lc

    qk = zm_ref[0, :, 0:2 * MLSTM_QK]
    pos_r = lax.broadcasted_iota(jnp.int32, qk.shape, 0)
    acc = cb_row_ref[...] + qk * cw_row_ref[CONV_WIDTH - 1:CONV_WIDTH, :]
    for r in range(1, CONV_WIDTH):
        sh = jnp.where(pos_r >= r, pltpu.roll(qk, r, 0), 0.0)
        acc = acc + sh * cw_row_ref[CONV_WIDTH - 1 - r:CONV_WIDTH - r, :]
    acc = _silu(acc)
    q_s[...] = acc[:, :MLSTM_QK] * dk ** -0.5
    k_s[...] = acc[:, MLSTM_QK:]
    kt = mkt_ref[0]
    pos_c = lax.broadcasted_iota(jnp.int32, kt.shape, 1)
    acc_t = cb_col_ref[...] + kt * cw_col_ref[:, CONV_WIDTH - 1:CONV_WIDTH]
    for r in range(1, CONV_WIDTH):
        sh = jnp.where(pos_c >= r, pltpu.roll(kt, r, 1), 0.0)
        acc_t = acc_t + sh * cw_col_ref[:, CONV_WIDTH - 1 - r:CONV_WIDTH - r]
    kt_s[...] = _silu(acc_t)

    gcol = _softcap(zm_ref[0, :, 2 * MLSTM_QK + 2 * MLSTM_V:] + bg_row_ref[...])
    lane = lax.broadcasted_iota(jnp.int32, gcol.shape, 1)
    gc_s[...] = jnp.where(lane < nh, gcol, _log_sigmoid(gcol))
    grow = _softcap(mgt_ref[0] + bg_col_ref[...])
    sub = lax.broadcasted_iota(jnp.int32, grow.shape, 0)
    gr_s[...] = jnp.where(sub < nh, grow, _log_sigmoid(grow))

    r_i = lax.broadcasted_iota(jnp.int32, (lc, lc), 0)
    c_i = lax.broadcasted_iota(jnp.int32, (lc, lc), 1)
    tri = jnp.where(c_i <= r_i, 1.0, 0.0).astype(F32)
    causal_t = r_i <= c_i
    ones_rows = jnp.ones((8, lc), F32)

    def chunk_pair(p, carry):
        base = pl.multiple_of(p * pair, pair)
        gr_slab = gr_s[:, pl.ds(base, pair)]
        kt_slab = kt_s[:, pl.ds(base, pair)]
        vt_slab = mvt_ref[0, :, pl.ds(base, pair)]
        outs = [[] for _ in range(nh)]
        for sc in range(2):
            r0 = base + sc * lc
            gcc = gc_s[pl.ds(r0, lc), :]
            grc = gr_slab[:, sc * lc:(sc + 1) * lc]
            bcum_col = _dot_hi(tri, gcc)
            bcum_row = _dot_nt_hi(grc, tri)
            qc = q_s[pl.ds(r0, lc), :]
            kc = k_s[pl.ds(r0, lc), :]
            new_carry = []
            for hd in range(nh):
                c_aug, m_prev = carry[hd]
                bc_c = bcum_col[:, nh + hd:nh + hd + 1]
                li_c = gcc[:, hd:hd + 1]
                bc_r = bcum_row[nh + hd:nh + hd + 1, :]
                li_r = grc[hd:hd + 1, :]
                gtot = bc_r[:, lc - 1:lc]
                d_t = jnp.where(causal_t, bc_r - bc_c + li_c, NEG_BIG)
                m_inter = bc_r + m_prev
                m_t = jnp.maximum(m_inter, jnp.max(d_t, axis=0, keepdims=True))
                q_h = qc[:, hd * dk:(hd + 1) * dk].astype(BF16)
                k_h = kc[:, hd * dk:(hd + 1) * dk].astype(BF16)
                kt_h = kt_slab[hd * dk:(hd + 1) * dk, sc * lc:(sc + 1) * lc].astype(BF16)
                vt_h = vt_slab[hd * dv:(hd + 1) * dv, sc * lc:(sc + 1) * lc]
                vt_aug = jnp.concatenate([vt_h, ones_rows], axis=0)
                s_t = _dot_nt(k_h, q_h)
                w_t = jnp.exp(d_t - m_t) * s_t
                s_inter = jnp.exp(m_inter - m_t)
                numden = _dot(vt_aug.astype(BF16), w_t.astype(BF16)) + s_inter * _dot_nt(c_aug.astype(BF16), q_h)
                den = numden[dv:dv + 1, :]
                outs[hd].append(numden[:dv, :] / jnp.maximum(jnp.abs(den), jnp.exp(-m_t)))
                a_log = gtot - bc_r + li_r
                a_max = jnp.max(a_log, axis=-1, keepdims=True)
                a_w = jnp.exp(a_log - a_max)
                kvn = _dot_nt((vt_aug * a_w).astype(BF16), kt_h)
                m_new = jnp.maximum(gtot + m_prev, a_max)
                s_old = jnp.exp(gtot + m_prev - m_new)
                s_new = jnp.exp(a_max - m_new)
                new_carry.append((s_old * c_aug + s_new * kvn, m_new))
            carry = tuple(new_carry)
        for hd in range(nh):
            yt_s[hd * dv:(hd + 1) * dv, pl.ds(base, pair)] = jnp.concatenate(outs[hd], axis=-1)
        return carry

    init = tuple((jnp.zeros((dv + 8, dk), F32), jnp.full((1, 1), NEG_BIG, F32)) for _ in range(nh))
    lax.fori_loop(0, s // pair, chunk_pair, init)

    parts = []
    for hd in range(nh):
        blk = yt_s[hd * dv:(hd + 1) * dv, :]
        parts.append(blk * lax.rsqrt(jnp.mean(blk * blk, axis=0, keepdims=True) + EPS))
    y = jnp.concatenate(parts, axis=0).T
    o_gate = _sigmoid(zm_ref[0, :, 2 * MLSTM_QK + MLSTM_V:2 * MLSTM_QK + 2 * MLSTM_V])
    o_ref[0] = (y * ng_ref[...] * o_gate).astype(o_ref.dtype)


def _mlstm(zm, mkt, mvt, mgt, conv_w, conv_b, b_i, b_f, norm_g):
    b, s, _ = zm.shape
    cw_col = conv_w[:, MLSTM_QK:].T
    cb_col = conv_b[MLSTM_QK:].reshape(-1, 1)
    bg = jnp.concatenate([b_i, b_f])
    bg_row = jnp.pad(bg, (0, LANES - 2 * MLSTM_HEADS)).reshape(1, LANES)
    bg_col = bg.reshape(-1, 1)
    per_b = lambda r, c: pl.BlockSpec((1, r, c), lambda i: (i, 0, 0))
    return pl.pallas_call(
        _mlstm_kernel,
        out_shape=jax.ShapeDtypeStruct((b, s, MLSTM_V), BF16),
        grid=(b,),
        in_specs=[per_b(s, M_COLS), per_b(MLSTM_QK, s), per_b(MLSTM_V, s), per_b(8, s),
                  _full(conv_w.shape), _full((1, 2 * MLSTM_QK)), _full(cw_col.shape), _full(cb_col.shape),
                  _full((1, LANES)), _full((8, 1)), _full((1, MLSTM_V))],
        out_specs=per_b(s, MLSTM_V),
        scratch_shapes=[pltpu.VMEM((s, MLSTM_QK), F32), pltpu.VMEM((s, MLSTM_QK), F32), pltpu.VMEM((MLSTM_QK, s), F32),
                        pltpu.VMEM((s, LANES), F32), pltpu.VMEM((8, s), F32), pltpu.VMEM((MLSTM_V, s), F32)],
        compiler_params=_cparams("arbitrary"),
        name="mlstm",
    )(zm, mkt, mvt, mgt, conv_w, conv_b.reshape(1, -1), cw_col, cb_col, bg_row, bg_col, norm_g.reshape(1, -1))


def _merge_kernel(x_ref, g_ref, y0_ref, y1_ref, y2_ref, y3_ref, wg_ref, wb_ref, wo_ref, o_ref):
    x = x_ref[...]
    d = x.shape[1]
    h = _rms_rows(x, g_ref[...]).astype(BF16)
    merged = None
    for n, y_ref in enumerate((y0_ref, y1_ref, y2_ref, y3_ref)):
        gate = _sigmoid(_dot(h, wg_ref[:, n * d:(n + 1) * d]))
        term = gate * _dot(y_ref[...], wb_ref[n])
        merged = term if merged is None else merged + term
    o_ref[...] = x + _dot(merged.astype(BF16), wo_ref[...])


def _merge(x2, g, ys, w_gate, w_branch, w_out, tm):
    t, d = x2.shape
    row = lambda w: pl.BlockSpec((tm, w), lambda i: (i, 0))
    return pl.pallas_call(
        _merge_kernel,
        out_shape=jax.ShapeDtypeStruct((t, d), F32),
        grid=(t // tm,),
        in_specs=[row(d), _full((1, d))] + [row(BRANCH_WIDTH)] * N_BRANCH
                 + [_full(w_gate.shape), _full(w_branch.shape), _full(w_out.shape)],
        out_specs=row(d),
        compiler_params=_cparams("arbitrary"),
        name="merge",
    )(x2, g.reshape(1, d), *ys, w_gate, w_branch, w_out)


def _ffn_kernel(x_ref, g_ref, w1_ref, w3_ref, w2_ref, o_ref):
    x = x_ref[...]
    h = _rms_rows(x, g_ref[...]).astype(BF16)
    u = _silu(_dot(h, w1_ref[...])) * _dot(h, w3_ref[...])
    o_ref[...] = x + _dot(u.astype(BF16), w2_ref[...])


def _ffn(x2, g, w1, w3, w2, tm):
    t, d = x2.shape
    row = pl.BlockSpec((tm, d), lambda i: (i, 0))
    return pl.pallas_call(
        _ffn_kernel,
        out_shape=jax.ShapeDtypeStruct((t, d), F32),
        grid=(t // tm,),
        in_specs=[row, _full((1, d)), _full(w1.shape), _full(w3.shape), _full(w2.shape)],
        out_specs=row,
        compiler_params=_cparams("arbitrary"),
        name="ffn",
    )(x2, g.reshape(1, d), w1, w3, w2)


def _moe_kernel(x_ref, g_ref, wr_ref, w1_ref, w3_ref, w2_ref, o_ref, h_s, gate_s, acc_s):
    e = pl.program_id(1)
    c = pl.program_id(2)
    n_e = pl.num_programs(1)
    n_c = pl.num_programs(2)

    @pl.when((e == 0) & (c == 0))
    def _():
        x = x_ref[...]
        h = _rms_rows(x, g_ref[...])
        h_s[...] = h.astype(BF16)
        logits = _dot_hi(h, wr_ref[...])
        lane = lax.broadcasted_iota(jnp.int32, logits.shape, 1)
        logits = jnp.where(lane < N_EXPERTS, logits, -jnp.inf)
        v1 = jnp.max(logits, axis=-1, keepdims=True)
        i1 = jnp.min(jnp.where(logits == v1, lane, LANES), axis=-1, keepdims=True)
        rest = jnp.where(lane == i1, -jnp.inf, logits)
        v2 = jnp.max(rest, axis=-1, keepdims=True)
        i2 = jnp.min(jnp.where(rest == v2, lane, LANES), axis=-1, keepdims=True)
        e2 = jnp.exp(v2 - v1)
        p1 = 1.0 / (1.0 + e2)
        p2 = e2 / (1.0 + e2)
        gate_s[...] = jnp.where(lane == i1, p1, 0.0) + jnp.where(lane == i2, p2, 0.0)
        acc_s[...] = jnp.zeros_like(acc_s)

    h = h_s[...]
    lane = lax.broadcasted_iota(jnp.int32, gate_s.shape, 1)
    g_e = jnp.sum(jnp.where(lane == e, gate_s[...], 0.0), axis=-1, keepdims=True)
    u = _silu(_dot(h, w1_ref[0])) * _dot(h, w3_ref[0])
    acc_s[...] += g_e * _dot(u.astype(BF16), w2_ref[0])

    @pl.when((e == n_e - 1) & (c == n_c - 1))
    def _():
        o_ref[...] = x_ref[...] + acc_s[...]


def _moe(x2, g, router, w1, w3, w2, tm, tf):
    t, d = x2.shape
    n_e, _, f = w1.shape
    router_p = jnp.pad(router, ((0, 0), (0, LANES - n_e)))
    row = pl.BlockSpec((tm, d), lambda i, e, c: (i, 0))
    return pl.pallas_call(
        _moe_kernel,
        out_shape=jax.ShapeDtypeStruct((t, d), F32),
        grid=(t // tm, n_e, f // tf),
        in_specs=[row, pl.BlockSpec((1, d), lambda i, e, c: (0, 0)), pl.BlockSpec((d, LANES), lambda i, e, c: (0, 0)),
                  pl.BlockSpec((1, d, tf), lambda i, e, c: (e, 0, c)),
                  pl.BlockSpec((1, d, tf), lambda i, e, c: (e, 0, c)),
                  pl.BlockSpec((1, tf, d), lambda i, e, c: (e, c, 0))],
        out_specs=row,
        scratch_shapes=[pltpu.VMEM((tm, d), BF16), pltpu.VMEM((tm, LANES), F32), pltpu.VMEM((tm, d), F32)],
        compiler_params=_cparams("arbitrary", "arbitrary", "arbitrary"),
        name="moe",
    )(x2, g.reshape(1, d), router_p, w1, w3, w2)


def _final_norm_kernel(x_ref, g_ref, o_ref):
    o_ref[...] = _rms_rows(x_ref[...], g_ref[...])


def _final_norm(x2, g, tm):
    t, d = x2.shape
    row = pl.BlockSpec((tm, d), lambda i: (i, 0))
    return pl.pallas_call(
        _final_norm_kernel,
        out_shape=jax.ShapeDtypeStruct((t, d), F32),
        grid=(t // tm,),
        in_specs=[row, _full((1, d))],
        out_specs=row,
        compiler_params=_cparams("arbitrary"),
        name="final_norm",
    )(x2, g.reshape(1, d))


def kernel(x, norm_mix_g, w_in, dsa_g_cq, dsa_g_ckv, dsa_g_kidx, dsa_w_uq, dsa_w_qidx, dsa_w_uv, rwkv_mu, rwkv_w0, rwkv_w_up, rwkv_a0, rwkv_a_up, rwkv_g_up, rwkv_k_k, rwkv_k_a, rwkv_r_k, rwkv_ln_g, rwkv_ln_b, mlstm_conv_w, mlstm_conv_b, mlstm_b_i, mlstm_b_f, mlstm_norm_g, fox_b_f, w_branch, w_out, norm_ffn_g, ffn_w1, ffn_w3, ffn_w2, moe_router, moe_w1, moe_w3, moe_w2, final_norm_g):
    b, s, d = x.shape
    depth = w_in.shape[0]
    t = b * s
    tm = min(512, s)
    for l in range(depth):
        wts = _inproj_weights(w_in[l])
        (zr, zm, fq, fk, fv, ckv, ki, fcol, cqt, ckvt, wit, frow, mkt, mvt, mgt) = _inproj(
            x, norm_mix_g[l], wts, dsa_g_cq[l], dsa_g_ckv[l], dsa_g_kidx[l], fox_b_f[l], tm)
        y_fox = _fox(fq, fk, fv, fcol, frow, min(512, s))
        y_dsa = _dsa(cqt, wit, ki, ckv, ckvt, dsa_w_uq[l], dsa_w_qidx[l], dsa_w_uv[l], min(256, s // 2)
                     ).reshape(t, BRANCH_WIDTH)
        y_rwkv = _rwkv(zr, rwkv_mu[l], rwkv_w0[l], rwkv_w_up[l], rwkv_a0[l], rwkv_a_up[l], rwkv_g_up[l], rwkv_k_k[l],
                       rwkv_k_a[l], rwkv_r_k[l], rwkv_ln_g[l], rwkv_ln_b[l], min(16, b), min(4, b), 32
                       ).reshape(t, BRANCH_WIDTH)
        y_mlstm = _mlstm(zm, mkt, mvt, mgt, mlstm_conv_w[l], mlstm_conv_b[l], mlstm_b_i[l], mlstm_b_f[l],
                         mlstm_norm_g[l]).reshape(t, BRANCH_WIDTH)
        ys = (y_dsa, y_rwkv, y_mlstm, y_fox.reshape(t, BRANCH_WIDTH))
        x2 = _merge(x.reshape(t, d), norm_mix_g[l], ys, wts['w_gate'], w_branch[l].astype(BF16),
                    w_out[l].astype(BF16), tm)
        j = l // 2
        if l % 2 == 0:
            x2 = _ffn(x2, norm_ffn_g[l], ffn_w1[j].astype(BF16), ffn_w3[j].astype(BF16), ffn_w2[j].astype(BF16), tm)
        else:
            x2 = _moe(x2, norm_ffn_g[l], moe_router[j], moe_w1[j].astype(BF16), moe_w3[j].astype(BF16),
                      moe_w2[j].astype(BF16), tm, moe_w1.shape[-1] // 2)
        x = x2.reshape(b, s, d)
    return _final_norm(x.reshape(t, d), final_norm_g, tm).reshape(b, s, d)
```

```python
import functools

import jax
import jax.numpy as jnp
from jax import lax
from jax.experimental import pallas as pl
from jax.experimental.pallas import tpu as pltpu

F32 = jnp.float32
BF16 = jnp.bfloat16
HIGHEST = lax.Precision.HIGHEST

EPS = 1e-6
NEG_BIG = -1e30

N_BRANCH = 4
BRANCH_WIDTH = 256
DSA_HEADS = 4
DSA_HEAD_DIM = 64
DSA_Q_LATENT = 128
DSA_KV_LATENT = 128
IDX_HEADS = 8
IDX_DIM = 32
TOPK_MAX = 256
DSA_Q_BLOCK = 128

RWKV_HEADS = 4
RWKV_HEAD_DIM = 64
RWKV_WIDTH = RWKV_HEADS * RWKV_HEAD_DIM
RWKV_LORA_W = 64
RWKV_LORA_A = 64
RWKV_LORA_G = 128
RWKV_GN_EPS = 64e-5
RWKV_IN = 3 * RWKV_WIDTH + RWKV_LORA_W + RWKV_LORA_A + RWKV_LORA_G

MLSTM_HEADS = 4
MLSTM_QK_DIM = 32
MLSTM_V_DIM = 64
MLSTM_CHUNK = 64
CONV_WIDTH = 4
GATE_SOFTCAP = 15.0
MLSTM_QK = MLSTM_HEADS * MLSTM_QK_DIM
MLSTM_V = MLSTM_HEADS * MLSTM_V_DIM

FOX_HEADS = 4
FOX_HEAD_DIM = 64
FOX_WIDTH = FOX_HEADS * FOX_HEAD_DIM

N_EXPERTS = 8

VMEM_LIMIT_BYTES = 56 * 1024 * 1024
LANES = 128


def _cparams(*sem):
    return pltpu.CompilerParams(dimension_semantics=sem, vmem_limit_bytes=VMEM_LIMIT_BYTES)


def _dot(a, b):
    return jnp.dot(a, b, preferred_element_type=F32)


def _dot_hi(a, b):
    return jnp.dot(a, b, preferred_element_type=F32, precision=HIGHEST)


def _dot_nt(a, b):
    return lax.dot_general(a, b, (((1,), (1,)), ((), ())), preferred_element_type=F32)


def _dot_nt_hi(a, b):
    return lax.dot_general(a, b, (((1,), (1,)), ((), ())), preferred_element_type=F32, precision=HIGHEST)


def _log_sigmoid(t):
    return jnp.minimum(t, 0.0) - jnp.log1p(jnp.exp(-jnp.abs(t)))


def _sigmoid(t):
    return 1.0 / (1.0 + jnp.exp(-t))


def _silu(t):
    return t * _sigmoid(t)


def _rms_rows(t, g_row):
    return t * lax.rsqrt(jnp.mean(t * t, axis=-1, keepdims=True) + EPS) * g_row


def _rms_cols(t, g_col):
    return t * lax.rsqrt(jnp.mean(t * t, axis=0, keepdims=True) + EPS) * g_col


def _full(shape):
    n = len(shape)
    return pl.BlockSpec(shape, lambda *_: (0,) * n)


T_CQ = 0
T_CKV = T_CQ + DSA_Q_LATENT
T_WIDX = T_CKV + DSA_KV_LATENT
T_FOXF = T_WIDX + IDX_HEADS
T_MK = T_FOXF + 8
T_MV = T_MK + MLSTM_QK
T_MG = T_MV + MLSTM_V
T_ROWS = T_MG + 8
D_COLS = 3 * LANES
M_COLS = 2 * MLSTM_QK + 2 * MLSTM_V + LANES


def _inproj_kernel(x_ref, g_ref, wr_ref, wm_ref, wf_ref, wd_ref, wt_ref,
                   gcq_ref, gckv_col_ref, gckv_row_ref, gki_ref, bf_row_ref, bf_col_ref,
                   zr_ref, zm_ref, q_ref, k_ref, v_ref, ckv_ref, ki_ref, fcol_ref,
                   cqt_ref, ckvt_ref, wit_ref, frow_ref, mkt_ref, mvt_ref, mgt_ref,
                   carry_row, carry_col):
    j = pl.program_id(1)
    tm = x_ref.shape[1]

    @pl.when(j == 0)
    def _():
        carry_row[...] = jnp.zeros_like(carry_row)
        carry_col[...] = jnp.zeros_like(carry_col)

    x = x_ref[0]
    h = _rms_rows(x, g_ref[...]).astype(BF16)

    zr_ref[0] = _dot_nt(h, wr_ref[...])
    zm_ref[0] = _dot_nt(h, wm_ref[...])

    zf = _dot_nt(h, wf_ref[...])
    for hd in range(FOX_HEADS):
        lo = hd * FOX_HEAD_DIM
        q_ref[0, hd] = (zf[:, lo:lo + FOX_HEAD_DIM] * FOX_HEAD_DIM ** -0.5).astype(BF16)
        k_ref[0, hd] = zf[:, FOX_WIDTH + lo:FOX_WIDTH + lo + FOX_HEAD_DIM].astype(BF16)
        v_ref[0, hd] = zf[:, 2 * FOX_WIDTH + lo:2 * FOX_WIDTH + lo + FOX_HEAD_DIM].astype(BF16)

    zd = _dot_nt(h, wd_ref[...])
    ckv_ref[0] = _rms_rows(zd[:, :LANES], gckv_row_ref[...]).astype(BF16)
    ki_pieces = [p.astype(F32) for p in _split3_bf16(_rms_rows(zd[:, LANES:LANES + IDX_DIM], gki_ref[...]))]
    ki_pad = jnp.zeros((tm, IDX_K_COLS - len(IDX_SPLIT_K) * IDX_DIM), F32)
    ki_ref[0] = jnp.concatenate([ki_pieces[kp] for kp, _ in IDX_SPLIT_K] + [ki_pad], axis=1).astype(BF16)
    lf_col = _log_sigmoid(zd[:, 2 * LANES:] + bf_row_ref[...])
    r_i = lax.broadcasted_iota(jnp.int32, (tm, tm), 0)
    c_i = lax.broadcasted_iota(jnp.int32, (tm, tm), 1)
    tri = jnp.where(c_i <= r_i, 1.0, 0.0).astype(F32)
    cum_col = _dot_hi(tri, lf_col) + carry_row[...]
    fcol_ref[0] = cum_col
    carry_row[...] = cum_col[tm - 1:tm, :]

    zt = _dot_nt(wt_ref[...], h)
    cqt_ref[0] = _rms_cols(zt[T_CQ:T_CKV], gcq_ref[...])
    ckvt_ref[0] = _rms_cols(zt[T_CKV:T_WIDX], gckv_col_ref[...]).astype(BF16)
    wit_ref[0] = zt[T_WIDX:T_FOXF] * IDX_HEADS ** -0.5
    lf_row = _log_sigmoid(zt[T_FOXF:T_MK] + bf_col_ref[...])
    cum_row = _dot_nt_hi(lf_row, tri) + carry_col[:, 0:1]
    frow_ref[0] = cum_row
    carry_col[...] = jnp.broadcast_to(cum_row[:, tm - 1:tm], carry_col.shape)
    mkt_ref[0] = zt[T_MK:T_MV]
    mvt_ref[0] = zt[T_MV:T_MG]
    mgt_ref[0] = zt[T_MG:T_ROWS]


def _inproj_weights(w_in_l):
    wt = w_in_l.T
    o = 0
    w_cq = wt[o:o + DSA_Q_LATENT]; o += DSA_Q_LATENT
    w_ckv = wt[o:o + DSA_KV_LATENT]; o += DSA_KV_LATENT
    w_kidx = wt[o:o + IDX_DIM]; o += IDX_DIM
    w_widx = wt[o:o + IDX_HEADS]; o += IDX_HEADS
    w_rwkv = wt[o:o + RWKV_IN]; o += RWKV_IN
    w_mq = wt[o:o + MLSTM_QK]; o += MLSTM_QK
    w_mk = wt[o:o + MLSTM_QK]; o += MLSTM_QK
    w_mv = wt[o:o + MLSTM_V]; o += MLSTM_V
    w_mo = wt[o:o + MLSTM_V]; o += MLSTM_V
    w_mi = wt[o:o + MLSTM_HEADS]; o += MLSTM_HEADS
    w_mf = wt[o:o + MLSTM_HEADS]; o += MLSTM_HEADS
    w_fox = wt[o:o + 3 * FOX_WIDTH]; o += 3 * FOX_WIDTH
    w_ff = wt[o:o + FOX_HEADS]; o += FOX_HEADS
    w_gate = wt[o:]

    def padr(w, n):
        return jnp.pad(w, ((0, n - w.shape[0]), (0, 0)))

    w_d = jnp.concatenate([w_ckv, padr(w_kidx, LANES), padr(w_ff, LANES)], axis=0)
    w_m = jnp.concatenate([w_mq, w_mk, w_mv, w_mo, padr(jnp.concatenate([w_mi, w_mf], axis=0), LANES)], axis=0)
    w_t = jnp.concatenate([w_cq, w_ckv, w_widx, padr(w_ff, 8), w_mk, w_mv, w_mi, w_mf], axis=0)
    assert w_t.shape[0] == T_ROWS
    cast = lambda w: w.astype(BF16)
    return dict(w_r=cast(w_rwkv), w_m=cast(w_m), w_f=cast(w_fox), w_d=cast(w_d), w_t=cast(w_t), w_gate=cast(w_gate))


def _inproj(x, g, wts, g_cq, g_ckv, g_kidx, fox_b_f, tm):
    b, s, d = x.shape
    nj = s // tm
    bf_row = jnp.pad(fox_b_f, (0, LANES - FOX_HEADS)).reshape(1, LANES)
    bf_col = jnp.pad(fox_b_f, (0, 8 - FOX_HEADS)).reshape(8, 1)
    row = lambda w: pl.BlockSpec((1, tm, w), lambda i, j: (i, j, 0))
    head = pl.BlockSpec((1, FOX_HEADS, tm, FOX_HEAD_DIM), lambda i, j: (i, 0, j, 0))
    col = lambda r: pl.BlockSpec((1, r, tm), lambda i, j: (i, 0, j))
    out_shape = [
        jax.ShapeDtypeStruct((b, s, RWKV_IN), F32),
        jax.ShapeDtypeStruct((b, s, M_COLS), F32),
        jax.ShapeDtypeStruct((b, FOX_HEADS, s, FOX_HEAD_DIM), BF16),
        jax.ShapeDtypeStruct((b, FOX_HEADS, s, FOX_HEAD_DIM), BF16),
        jax.ShapeDtypeStruct((b, FOX_HEADS, s, FOX_HEAD_DIM), BF16),
        jax.ShapeDtypeStruct((b, s, DSA_KV_LATENT), BF16),
        jax.ShapeDtypeStruct((b, s, IDX_K_COLS), BF16),
        jax.ShapeDtypeStruct((b, s, LANES), F32),
        jax.ShapeDtypeStruct((b, DSA_Q_LATENT, s), F32),
        jax.ShapeDtypeStruct((b, DSA_KV_LATENT, s), BF16),
        jax.ShapeDtypeStruct((b, IDX_HEADS, s), F32),
        jax.ShapeDtypeStruct((b, 8, s), F32),
        jax.ShapeDtypeStruct((b, MLSTM_QK, s), F32),
        jax.ShapeDtypeStruct((b, MLSTM_V, s), F32),
        jax.ShapeDtypeStruct((b, 8, s), F32),
    ]
    out_specs = [row(RWKV_IN), row(M_COLS), head, head, head, row(DSA_KV_LATENT), row(IDX_K_COLS), row(LANES),
                 col(DSA_Q_LATENT), col(DSA_KV_LATENT), col(IDX_HEADS), col(8), col(MLSTM_QK), col(MLSTM_V), col(8)]
    in_specs = [row(d), _full((1, d)), _full(wts['w_r'].shape), _full(wts['w_m'].shape), _full(wts['w_f'].shape),
                _full(wts['w_d'].shape), _full(wts['w_t'].shape),
                _full((DSA_Q_LATENT, 1)), _full((DSA_KV_LATENT, 1)), _full((1, DSA_KV_LATENT)), _full((1, IDX_DIM)),
                _full((1, LANES)), _full((8, 1))]
    return pl.pallas_call(
        _inproj_kernel,
        out_shape=out_shape,
        grid=(b, nj),
        in_specs=in_specs,
        out_specs=out_specs,
        scratch_shapes=[pltpu.VMEM((1, LANES), F32), pltpu.VMEM((8, LANES), F32)],
        compiler_params=_cparams("arbitrary", "arbitrary"),
        name="inproj",
    )(x, g.reshape(1, d), wts['w_r'], wts['w_m'], wts['w_f'], wts['w_d'], wts['w_t'],
      g_cq.reshape(-1, 1), g_ckv.reshape(-1, 1), g_ckv.reshape(1, -1), g_kidx.reshape(1, -1), bf_row, bf_col)


def _fox_kernel(q_ref, k_ref, v_ref, fcol_ref, frow_ref, o_ref):
    qi = pl.program_id(1)
    tq = q_ref.shape[2]
    outs = []
    for hd in range(FOX_HEADS):
        q = q_ref[0, hd]
        fq = fcol_ref[0][:, hd:hd + 1]

        def scores(j):
            start = pl.multiple_of(j * tq, tq)
            k = k_ref[0, hd, pl.ds(start, tq), :]
            v = v_ref[0, hd, pl.ds(start, tq), :]
            fk = frow_ref[0, hd:hd + 1, pl.ds(start, tq)]
            return _dot_nt(q, k) + (fq - fk), v

        def update(carry, sc, v):
            m, l, acc = carry
            m_new = jnp.maximum(m, jnp.max(sc, axis=-1, keepdims=True))
            alpha = jnp.exp(m - m_new)
            p = jnp.exp(sc - m_new)
            l = alpha * l + jnp.sum(p, axis=-1, keepdims=True)
            acc = alpha * acc + _dot(p.astype(BF16), v)
            return m_new, l, acc

        def body(j, carry):
            sc, v = scores(j)
            return update(carry, sc, v)

        init = (jnp.full((tq, 1), NEG_BIG, F32), jnp.zeros((tq, 1), F32), jnp.zeros((tq, FOX_HEAD_DIM), F32))
        carry = lax.fori_loop(0, qi, body, init)
        sc, v = scores(qi)
        r_i = lax.broadcasted_iota(jnp.int32, (tq, tq), 0)
        c_i = lax.broadcasted_iota(jnp.int32, (tq, tq), 1)
        sc = jnp.where(c_i <= r_i, sc, NEG_BIG)
        m, l, acc = update(carry, sc, v)
        outs.append(acc / l)
    o_ref[0] = jnp.concatenate(outs, axis=-1).astype(o_ref.dtype)


def _fox(q, k, v, fcol, frow, tq):
    b, nh, s, d = q.shape
    return pl.pallas_call(
        _fox_kernel,
        out_shape=jax.ShapeDtypeStruct((b, s, nh * d), BF16),
        grid=(b, s // tq),
        in_specs=[
            pl.BlockSpec((1, nh, tq, d), lambda i, j: (i, 0, j, 0)),
            pl.BlockSpec((1, nh, s, d), lambda i, j: (i, 0, 0, 0)),
            pl.BlockSpec((1, nh, s, d), lambda i, j: (i, 0, 0, 0)),
            pl.BlockSpec((1, tq, LANES), lambda i, j: (i, j, 0)),
            pl.BlockSpec((1, 8, s), lambda i, j: (i, 0, 0)),
        ],
        out_specs=pl.BlockSpec((1, tq, nh * d), lambda i, j: (i, j, 0)),
        compiler_params=_cparams("arbitrary", "arbitrary"),
        name="fox",
    )(q, k, v, fcol, frow)


INT_MIN = -2 ** 31


IDX_SPLIT_K = ((0, 0), (0, 1), (1, 0), (0, 2), (1, 1), (2, 0))
IDX_K_COLS = 2 * LANES


def _split3_bf16(t):
    p0 = t.astype(BF16)
    r1 = t - p0.astype(F32)
    p1 = r1.astype(BF16)
    p2 = (r1 - p1.astype(F32)).astype(BF16)
    return p0, p1, p2


def _tree_sum(parts):
    while len(parts) > 1:
        parts = [parts[i] + parts[i + 1] for i in range(0, len(parts) - 1, 2)] + (parts[-1:] if len(parts) % 2 else [])
    return parts[0]


def _dsa_kernel(cqt_ref, wit_ref, ki6_ref, ckv_ref, ckvt_ref, wuqt_ref, wqit_ref, wuvt_ref, o_ref, key_s, *, ck):
    qb = pl.program_id(1)
    s = ki6_ref.shape[1]
    nq = cqt_ref.shape[2]
    topk = min(TOPK_MAX, s // 4)
    n_ck = (qb * nq + nq + ck - 1) // ck
    lat = DSA_KV_LATENT

    cq = cqt_ref[0]
    qi_t = _dot_hi(wqit_ref[...], cq)
    zeros_pad = jnp.zeros((IDX_K_COLS - len(IDX_SPLIT_K) * IDX_DIM, nq), BF16)
    q_blocks = []
    for hd in range(IDX_HEADS):
        pieces = _split3_bf16(qi_t[hd * IDX_DIM:(hd + 1) * IDX_DIM, :])
        q_blocks.append(jnp.concatenate([pieces[qp] for _, qp in IDX_SPLIT_K] + [zeros_pad], axis=0))
    q6 = jnp.concatenate(q_blocks, axis=1)
    wi = wit_ref[0] * IDX_DIM ** -0.5
    tpos = qb * nq + lax.broadcasted_iota(jnp.int32, (ck, nq), 1)
    row = lax.broadcasted_iota(jnp.int32, (ck, nq), 0)

    def chunk_start(c):
        return pl.multiple_of(c * ck, ck)

    def score_chunk(c, carry):
        c0 = chunk_start(c)
        ki6 = ki6_ref[0, pl.ds(c0, ck), :]
        score = None
        for hp in range(0, IDX_HEADS, 2):
            dots = _dot(ki6, q6[:, hp * nq:(hp + 2) * nq])
            for i in range(2):
                term = wi[hp + i:hp + i + 1, :] * jnp.maximum(dots[:, i * nq:(i + 1) * nq], 0.0)
                score = term if score is None else score + term
        bits = pltpu.bitcast(score, jnp.int32)
        key = jnp.where(bits < 0, bits ^ 0x7FFFFFFF, bits)
        key = jnp.where(score == 0.0, 0, key)
        key_s[pl.ds(c0, ck), :] = jnp.where(c0 + row <= tpos, key, INT_MIN)
        return carry

    n_pair = (n_ck + 1) // 2
    lax.fori_loop(0, n_pair, lambda j, carry: score_chunk(2 * j + 1, score_chunk(2 * j, carry)), 0)

    def count(mask_fn):
        def body(c, acc):
            c0 = chunk_start(c)
            ones = jnp.where(mask_fn(key_s[pl.ds(c0, ck), :], c0 + row), 1, 0)
            return acc + _tree_sum([ones[i * 8:(i + 1) * 8] for i in range(ck // 8)])
        acc = lax.fori_loop(0, n_ck, body, jnp.zeros((8, nq), jnp.int32))
        return jnp.sum(acc, axis=0, keepdims=True)

    def value_bit(i, carry):
        lo, n_lo = carry
        cand = lo + jnp.left_shift(jnp.int32(1), 31 - i)
        n_cand = count(lambda key, pos: key >= cand)
        ok = n_cand >= topk
        return jnp.where(ok, cand, lo), jnp.where(ok, n_cand, n_lo)

    thr, n_ge = lax.fori_loop(0, 32, value_bit, (jnp.full((1, nq), INT_MIN, jnp.int32),
                                                 jnp.full((1, nq), s, jnp.int32)))

    n_bits = s.bit_length()

    def tie_search():
        need = topk - count(lambda key, pos: key > thr)

        def index_bit(i, lo):
            cand = lo + jnp.left_shift(jnp.int32(1), n_bits - 1 - i)
            return jnp.where(count(lambda key, pos: (key == thr) & (pos < cand)) < need, cand, lo)

        return lax.fori_loop(0, n_bits, index_bit, jnp.zeros((1, nq), jnp.int32))

    surplus = jnp.max(jnp.where((n_ge > topk) & (thr > INT_MIN), 1, 0))
    last = lax.cond(surplus > 0, tie_search, lambda: jnp.full((1, nq), s, jnp.int32))

    q_t = (_dot(wuqt_ref[...], cq.astype(BF16)) * lat ** -0.5).astype(BF16)

    def attend(c, carry):
        c0 = chunk_start(c)
        key = key_s[pl.ds(c0, ck), :]
        pos = c0 + row
        sel = ((key > thr) | ((key == thr) & (pos <= last))) & (pos <= tpos)
        bias = jnp.where(sel, 0.0, NEG_BIG)
        ckv = ckv_ref[0, pl.ds(c0, ck), :]
        ckvt = ckvt_ref[0, :, pl.ds(c0, ck)]
        new = []
        for hd in range(DSA_HEADS):
            m, l, acc = carry[hd]
            lg = _dot(ckv, q_t[hd * lat:(hd + 1) * lat, :]) + bias
            m_new = jnp.maximum(m, jnp.max(lg, axis=0, keepdims=True))
            alpha = jnp.exp(m - m_new)
            p = jnp.exp(lg - m_new)
            new.append((m_new, alpha * l + jnp.sum(p, axis=0, keepdims=True),
                        alpha * acc + _dot(ckvt, p.astype(BF16))))
        return tuple(new)

    init = tuple((jnp.full((1, nq), NEG_BIG, F32), jnp.zeros((1, nq), F32), jnp.zeros((lat, nq), F32))
                 for _ in range(DSA_HEADS))
    final = lax.fori_loop(0, n_pair, lambda j, carry: attend(2 * j + 1, attend(2 * j, carry)), init)
    outs = [_dot(wuvt_ref[hd], (acc / l).astype(BF16)) for hd, (_, l, acc) in enumerate(final)]
    o_ref[0] = jnp.concatenate(outs, axis=0).T.astype(o_ref.dtype)


def _dsa(cqt, wit, ki6, ckv, ckvt, w_uq, w_qidx, w_uv, ck):
    b, c, s = cqt.shape
    nq = DSA_Q_BLOCK
    wuqt = w_uq.reshape(c, -1).T.astype(BF16)
    wqit = w_qidx.reshape(c, -1).T
    wuvt = jnp.transpose(w_uv, (0, 2, 1)).astype(BF16)
    return pl.pallas_call(
        functools.partial(_dsa_kernel, ck=ck),
        out_shape=jax.ShapeDtypeStruct((b, s, DSA_HEADS * DSA_HEAD_DIM), BF16),
        grid=(b, s // nq),
        in_specs=[pl.BlockSpec((1, c, nq), lambda i, j: (i, 0, j)),
                  pl.BlockSpec((1, IDX_HEADS, nq), lambda i, j: (i, 0, j)),
                  pl.BlockSpec((1, s, IDX_K_COLS), lambda i, j: (i, 0, 0)),
                  pl.BlockSpec((1, s, DSA_KV_LATENT), lambda i, j: (i, 0, 0)),
                  pl.BlockSpec((1, DSA_KV_LATENT, s), lambda i, j: (i, 0, 0)),
                  _full(wuqt.shape), _full(wqit.shape), _full(wuvt.shape)],
        out_specs=pl.BlockSpec((1, nq, DSA_HEADS * DSA_HEAD_DIM), lambda i, j: (i, j, 0)),
        scratch_shapes=[pltpu.VMEM((s, nq), jnp.int32)],
        compiler_params=_cparams("arbitrary", "arbitrary"),
        name="dsa",
    )(cqt, wit, ki6, ckv, ckvt, wuqt, wqit, wuvt)


Y_PAD = 8


def _head_ones(n, dtype):
    r_i = lax.broadcasted_iota(jnp.int32, (n, n), 0) // RWKV_HEAD_DIM
    c_i = lax.broadcasted_iota(jnp.int32, (n, n), 1) // RWKV_HEAD_DIM
    return jnp.where(r_i == c_i, 1.0, 0.0).astype(dtype)


def _split_bf16(t):
    hi = t.astype(BF16)
    lo = (t - hi.astype(F32)).astype(BF16)
    return hi, lo


def _rwkv_kernel(z_ref, mu_ref, w0_ref, wup_ref, a0_ref, aup_ref, gup_ref, kk_ref, ka_ref, rk_ref, lng_ref, lnb_ref,
                 o_ref, st_s, prev_s, step_s, y_s, bonus_s, gate_s, *, grp):
    c = pl.program_id(1)
    n_g, tc, _ = z_ref.shape
    hd, wd_ = RWKV_HEAD_DIM, RWKV_WIDTH

    @pl.when(c == 0)
    def _():
        st_s[...] = jnp.zeros_like(st_s)
        prev_s[...] = jnp.zeros_like(prev_s)

    ones_b = _head_ones(wd_, BF16)
    diag = jnp.where(lax.broadcasted_iota(jnp.int32, (hd, wd_), 0)
                     == lax.broadcasted_iota(jnp.int32, (hd, wd_), 1) % hd, 1.0, 0.0).astype(F32)

    def head_sum(t):
        hi, lo = _split_bf16(t)
        return _dot(hi, ones_b) + _dot(lo, ones_b)

    zs = []
    for g in range(n_g):
        z = z_ref[g]
        row = lax.broadcasted_iota(jnp.int32, z.shape, 0)
        z_prev = jnp.where(row == 0, prev_s[g], pltpu.roll(z, 1, 0))
        prev_s[g] = z[tc - 1:tc, :]
        zs.append(z + mu_ref[...] * (z_prev - z))
    z = jnp.concatenate(zs, axis=0)
    r = z[:, 0:wd_]
    k = z[:, wd_:2 * wd_]
    v = z[:, 2 * wd_:3 * wd_]
    o = 3 * wd_
    w_lora = z[:, o:o + RWKV_LORA_W]
    a_lora = z[:, o + RWKV_LORA_W:o + RWKV_LORA_W + RWKV_LORA_A]
    g_lora = z[:, o + RWKV_LORA_W + RWKV_LORA_A:]
    w_log = _log_sigmoid(w0_ref[...] + _dot_hi(jnp.tanh(w_lora), wup_ref[...])) - 0.5
    a = _sigmoid(a0_ref[...] + _dot_hi(a_lora, aup_ref[...]))
    gate_s[...] = _dot_hi(_sigmoid(g_lora), gup_ref[...])
    kk = k * kk_ref[...]
    kk = kk / jnp.maximum(jnp.sqrt(head_sum(kk * kk)), 1e-12)
    k = k * (1.0 + (a - 1.0) * ka_ref[...])
    step_s[0] = jnp.exp(-jnp.exp(w_log))
    step_s[1] = -kk
    step_s[2] = kk * a
    step_s[3] = k
    step_s[4] = r
    step_s[5] = v
    bonus_s[...] = head_sum(r * k * rk_ref[...]) * v

    n_grp = n_g // grp

    def rows(kind, g0, t):
        return jnp.concatenate(
            [jnp.broadcast_to(step_s[kind, pl.ds((g0 + i) * tc + t, 1), :], (hd, wd_)) for i in range(grp)], axis=0)

    def emit_y(st, g0, t_read, t_write):
        yb = _dot((st * rows(4, g0, t_read)).astype(BF16), ones_b) * diag_g
        for i in range(grp):
            tile = jnp.sum(yb[i * hd:(i + 1) * hd].reshape(hd // 8, 8, wd_), axis=0)
            y_s[pl.ds(Y_PAD + (g0 + i) * tc + t_write, 1), :] = jnp.sum(tile, axis=0, keepdims=True)

    def step(t, carry):
        for q in range(n_grp):
            g0 = q * grp
            st = st_s[q]
            sa = _dot((st * rows(1, g0, t)).astype(BF16), ones_b)
            emit_y(st, g0, jnp.maximum(t - 1, 0), t - 1)
            vb = _dot((rows(5, g0, t) * diag_g).astype(BF16), ones_b)
            st_s[q] = st * rows(0, g0, t) + sa * rows(2, g0, t) + vb * rows(3, g0, t)
        return carry

    diag_g = jnp.concatenate([diag] * grp, axis=0)
    lax.fori_loop(0, tc, step, 0)
    for q in range(n_grp):
        emit_y(st_s[q], q * grp, tc - 1, tc - 1)

    y = y_s[pl.ds(Y_PAD, n_g * tc), :]
    mean = head_sum(y) * (1.0 / hd)
    yc = y - mean
    var = head_sum(yc * yc) * (1.0 / hd)
    yn = yc * lax.rsqrt(var + RWKV_GN_EPS) * lng_ref[...] + lnb_ref[...]
    out = ((yn + bonus_s[...]) * gate_s[...]).astype(o_ref.dtype)
    for g in range(n_g):
        o_ref[g] = out[g * tc:(g + 1) * tc]


def _rwkv(zr, mu, w0, w_up, a0, a_up, g_up, k_k, k_a, r_k, ln_g, ln_b, n_g, grp, tc):
    b, s, zin = zr.shape
    wd_ = RWKV_WIDTH
    vec = lambda p: p.reshape(1, -1)
    params = [vec(mu), vec(w0), w_up, vec(a0), a_up, g_up, vec(k_k), vec(k_a), vec(r_k), vec(ln_g), vec(ln_b)]
    return pl.pallas_call(
        functools.partial(_rwkv_kernel, grp=grp),
        out_shape=jax.ShapeDtypeStruct((b, s, wd_), BF16),
        grid=(b // n_g, s // tc),
        in_specs=[pl.BlockSpec((n_g, tc, zin), lambda i, c: (i, c, 0))] + [_full(p.shape) for p in params],
        out_specs=pl.BlockSpec((n_g, tc, wd_), lambda i, c: (i, c, 0)),
        scratch_shapes=[pltpu.VMEM((n_g // grp, grp * RWKV_HEAD_DIM, wd_), F32), pltpu.VMEM((n_g, 1, zin), F32),
                        pltpu.VMEM((6, n_g * tc, wd_), F32), pltpu.VMEM((Y_PAD + n_g * tc, wd_), F32),
                        pltpu.VMEM((n_g * tc, wd_), F32), pltpu.VMEM((n_g * tc, wd_), F32)],
        compiler_params=_cparams("arbitrary", "arbitrary"),
        name="rwkv",
    )(zr, *params)


def _softcap(t):
    return GATE_SOFTCAP * jnp.tanh(t / GATE_SOFTCAP)


def _mlstm_kernel(zm_ref, mkt_ref, mvt_ref, mgt_ref, cw_row_ref, cb_row_ref, cw_col_ref, cb_col_ref,
                  bg_row_ref, bg_col_ref, ng_ref, o_ref, q_s, k_s, kt_s, gc_s, gr_s, yt_s):
    s = zm_ref.shape[1]
    nh, dk, dv, lc = MLSTM_HEADS, MLSTM_QK_DIM, MLSTM_V_DIM, MLSTM_CHUNK
    pair = 2 * lc

    qk = zm_ref[0, :, 0:2 * MLSTM_QK]
    pos_r = lax.broadcasted_iota(jnp.int32, qk.shape, 0)
    acc = cb_row_ref[...] + qk * cw_row_ref[CONV_WIDTH - 1:CONV_WIDTH, :]
    for r in range(1, CONV_WIDTH):
        sh = jnp.where(pos_r >= r, pltpu.roll(qk, r, 0), 0.0)
        acc = acc + sh * cw_row_ref[CONV_WIDTH - 1 - r:CONV_WIDTH - r, :]
    acc = _silu(acc)
    q_s[...] = acc[:, :MLSTM_QK] * dk ** -0.5
    k_s[...] = acc[:, MLSTM_QK:]
    kt = mkt_ref[0]
    pos_c = lax.broadcasted_iota(jnp.int32, kt.shape, 1)
    acc_t = cb_col_ref[...] + kt * cw_col_ref[:, CONV_WIDTH - 1:CONV_WIDTH]
    for r in range(1, CONV_WIDTH):
        sh = jnp.where(pos_c >= r, pltpu.roll(kt, r, 1), 0.0)
        acc_t = acc_t + sh * cw_col_ref[:, CONV_WIDTH - 1 - r:CONV_WIDTH - r]
    kt_s[...] = _silu(acc_t)

    gcol = _softcap(zm_ref[0, :, 2 * MLSTM_QK + 2 * MLSTM_V:] + bg_row_ref[...])
    lane = lax.broadcasted_iota(jnp.int32, gcol.shape, 1)
    gc_s[...] = jnp.where(lane < nh, gcol, _log_sigmoid(gcol))
    grow = _softcap(mgt_ref[0] + bg_col_ref[...])
    sub = lax.broadcasted_iota(jnp.int32, grow.shape, 0)
    gr_s[...] = jnp.where(sub < nh, grow, _log_sigmoid(grow))

    r_i = lax.broadcasted_iota(jnp.int32, (lc, lc), 0)
    c_i = lax.broadcasted_iota(jnp.int32, (lc, lc), 1)
    tri = jnp.where(c_i <= r_i, 1.0, 0.0).astype(F32)
    causal_t = r_i <= c_i
    ones_rows = jnp.ones((8, lc), F32)

    def chunk_pair(p, carry):
        base = pl.multiple_of(p * pair, pair)
        gr_slab = gr_s[:, pl.ds(base, pair)]
        kt_slab = kt_s[:, pl.ds(base, pair)]
        vt_slab = mvt_ref[0, :, pl.ds(base, pair)]
        outs = [[] for _ in range(nh)]
        for sc in range(2):
            r0 = base + sc * lc
            gcc = gc_s[pl.ds(r0, lc), :]
            grc = gr_slab[:, sc * lc:(sc + 1) * lc]
            bcum_col = _dot_hi(tri, gcc)
            bcum_row = _dot_nt_hi(grc, tri)
            qc = q_s[pl.ds(r0, lc), :]
            kc = k_s[pl.ds(r0, lc), :]
            new_carry = []
            for hd in range(nh):
                c_aug, m_prev = carry[hd]
                bc_c = bcum_col[:, nh + hd:nh + hd + 1]
                li_c = gcc[:, hd:hd + 1]
                bc_r = bcum_row[nh + hd:nh + hd + 1, :]
                li_r = grc[hd:hd + 1, :]
                gtot = bc_r[:, lc - 1:lc]
                d_t = jnp.where(causal_t, bc_r - bc_c + li_c, NEG_BIG)
                m_inter = bc_r + m_prev
                m_t = jnp.maximum(m_inter, jnp.max(d_t, axis=0, keepdims=True))
                q_h = qc[:, hd * dk:(hd + 1) * dk].astype(BF16)
                k_h = kc[:, hd * dk:(hd + 1) * dk].astype(BF16)
                kt_h = kt_slab[hd * dk:(hd + 1) * dk, sc * lc:(sc + 1) * lc].astype(BF16)
                vt_h = vt_slab[hd * dv:(hd + 1) * dv, sc * lc:(sc + 1) * lc]
                vt_aug = jnp.concatenate([vt_h, ones_rows], axis=0)
                s_t = _dot_nt(k_h, q_h)
                w_t = jnp.exp(d_t - m_t) * s_t
                s_inter = jnp.exp(m_inter - m_t)
                numden = _dot(vt_aug.astype(BF16), w_t.astype(BF16)) + s_inter * _dot_nt(c_aug.astype(BF16), q_h)
                den = numden[dv:dv + 1, :]
                outs[hd].append(numden[:dv, :] / jnp.maximum(jnp.abs(den), jnp.exp(-m_t)))
                a_log = gtot - bc_r + li_r
                a_max = jnp.max(a_log, axis=-1, keepdims=True)
                a_w = jnp.exp(a_log - a_max)
                kvn = _dot_nt((vt_aug * a_w).astype(BF16), kt_h)
                m_new = jnp.maximum(gtot + m_prev, a_max)
                s_old = jnp.exp(gtot + m_prev - m_new)
                s_new = jnp.exp(a_max - m_new)
                new_carry.append((s_old * c_aug + s_new * kvn, m_new))
            carry = tuple(new_carry)
        for hd in range(nh):
            yt_s[hd * dv:(hd + 1) * dv, pl.ds(base, pair)] = jnp.concatenate(outs[hd], axis=-1)
        return carry

    init = tuple((jnp.zeros((dv + 8, dk), F32), jnp.full((1, 1), NEG_BIG, F32)) for _ in range(nh))
    lax.fori_loop(0, s // pair, chunk_pair, init)

    parts = []
    for hd in range(nh):
        blk = yt_s[hd * dv:(hd + 1) * dv, :]
        parts.append(blk * lax.rsqrt(jnp.mean(blk * blk, axis=0, keepdims=True) + EPS))
    y = jnp.concatenate(parts, axis=0).T
    o_gate = _sigmoid(zm_ref[0, :, 2 * MLSTM_QK + MLSTM_V:2 * MLSTM_QK + 2 * MLSTM_V])
    o_ref[0] = (y * ng_ref[...] * o_gate).astype(o_ref.dtype)


def _mlstm(zm, mkt, mvt, mgt, conv_w, conv_b, b_i, b_f, norm_g):
    b, s, _ = zm.shape
    cw_col = conv_w[:, MLSTM_QK:].T
    cb_col = conv_b[MLSTM_QK:].reshape(-1, 1)
    bg = jnp.concatenate([b_i, b_f])
    bg_row = jnp.pad(bg, (0, LANES - 2 * MLSTM_HEADS)).reshape(1, LANES)
    bg_col = bg.reshape(-1, 1)
    per_b = lambda r, c: pl.BlockSpec((1, r, c), lambda i: (i, 0, 0))
    return pl.pallas_call(
        _mlstm_kernel,
        out_shape=jax.ShapeDtypeStruct((b, s, MLSTM_V), BF16),
        grid=(b,),
        in_specs=[per_b(s, M_COLS), per_b(MLSTM_QK, s), per_b(MLSTM_V, s), per_b(8, s),
                  _full(conv_w.shape), _full((1, 2 * MLSTM_QK)), _full(cw_col.shape), _full(cb_col.shape),
                  _full((1, LANES)), _full((8, 1)), _full((1, MLSTM_V))],
        out_specs=per_b(s, MLSTM_V),
        scratch_shapes=[pltpu.VMEM((s, MLSTM_QK), F32), pltpu.VMEM((s, MLSTM_QK), F32), pltpu.VMEM((MLSTM_QK, s), F32),
                        pltpu.VMEM((s, LANES), F32), pltpu.VMEM((8, s), F32), pltpu.VMEM((MLSTM_V, s), F32)],
        compiler_params=_cparams("arbitrary"),
        name="mlstm",
    )(zm, mkt, mvt, mgt, conv_w, conv_b.reshape(1, -1), cw_col, cb_col, bg_row, bg_col, norm_g.reshape(1, -1))


def _merge_kernel(x_ref, g_ref, y0_ref, y1_ref, y2_ref, y3_ref, wg_ref, wb_ref, wo_ref, o_ref):
    x = x_ref[...]
    d = x.shape[1]
    h = _rms_rows(x, g_ref[...]).astype(BF16)
    merged = None
    for n, y_ref in enumerate((y0_ref, y1_ref, y2_ref, y3_ref)):
        gate = _sigmoid(_dot_nt(h, wg_ref[n * d:(n + 1) * d, :]))
        term = gate * _dot(y_ref[...], wb_ref[n])
        merged = term if merged is None else merged + term
    o_ref[...] = x + _dot(merged.astype(BF16), wo_ref[...])


def _merge(x2, g, ys, w_gate, w_branch, w_out, tm):
    t, d = x2.shape
    row = lambda w: pl.BlockSpec((tm, w), lambda i: (i, 0))
    return pl.pallas_call(
        _merge_kernel,
        out_shape=jax.ShapeDtypeStruct((t, d), F32),
        grid=(t // tm,),
        in_specs=[row(d), _full((1, d))] + [row(BRANCH_WIDTH)] * N_BRANCH
                 + [_full(w_gate.shape), _full(w_branch.shape), _full(w_out.shape)],
        out_specs=row(d),
        compiler_params=_cparams("arbitrary"),
        name="merge",
    )(x2, g.reshape(1, d), *ys, w_gate, w_branch, w_out)


def _ffn_kernel(x_ref, g_ref, w1_ref, w3_ref, w2_ref, o_ref):
    x = x_ref[...]
    h = _rms_rows(x, g_ref[...]).astype(BF16)
    u = _silu(_dot(h, w1_ref[...])) * _dot(h, w3_ref[...])
    o_ref[...] = x + _dot(u.astype(BF16), w2_ref[...])


def _ffn(x2, g, w1, w3, w2, tm):
    t, d = x2.shape
    row = pl.BlockSpec((tm, d), lambda i: (i, 0))
    return pl.pallas_call(
        _ffn_kernel,
        out_shape=jax.ShapeDtypeStruct((t, d), F32),
        grid=(t // tm,),
        in_specs=[row, _full((1, d)), _full(w1.shape), _full(w3.shape), _full(w2.shape)],
        out_specs=row,
        compiler_params=_cparams("arbitrary"),
        name="ffn",
    )(x2, g.reshape(1, d), w1, w3, w2)


MOE_CAP = 160


def _moe_router_kernel(x_ref, g_ref, wr_ref, h_ref, rank_ref, gate_ref, rrow_ref, cnt_ref):
    tm = x_ref.shape[0]
    h = _rms_rows(x_ref[...], g_ref[...])
    h_ref[...] = h.astype(BF16)
    logits = _dot_hi(h, wr_ref[...])
    lane = lax.broadcasted_iota(jnp.int32, logits.shape, 1)
    logits = jnp.where(lane < N_EXPERTS, logits, -jnp.inf)
    v1 = jnp.max(logits, axis=-1, keepdims=True)
    i1 = jnp.min(jnp.where(logits == v1, lane, LANES), axis=-1, keepdims=True)
    rest = jnp.where(lane == i1, -jnp.inf, logits)
    v2 = jnp.max(rest, axis=-1, keepdims=True)
    i2 = jnp.min(jnp.where(rest == v2, lane, LANES), axis=-1, keepdims=True)
    e2 = jnp.exp(v2 - v1)
    gate_ref[...] = jnp.where(lane == i1, 1.0 / (1.0 + e2), 0.0) + jnp.where(lane == i2, e2 / (1.0 + e2), 0.0)
    routed = jnp.where((lane == i1) | (lane == i2), 1.0, 0.0)
    r_i = lax.broadcasted_iota(jnp.int32, (tm, tm), 0)
    c_i = lax.broadcasted_iota(jnp.int32, (tm, tm), 1)
    rank = _dot(jnp.where(c_i < r_i, 1.0, 0.0).astype(BF16), routed.astype(BF16))
    rank = jnp.where(routed > 0.0, rank, -1.0)
    rank_ref[...] = rank
    eye = jnp.where(lax.broadcasted_iota(jnp.int32, (LANES, LANES), 0)
                    == lax.broadcasted_iota(jnp.int32, (LANES, LANES), 1), 1.0, 0.0)
    rrow_ref[0] = _dot_nt_hi(eye, rank)[0:N_EXPERTS, :]
    cnt_ref[0] = jnp.broadcast_to(jnp.sum(routed, axis=0, keepdims=True), (8, LANES))


def _moe_expert_kernel(cnt_ref, y_ref, h_ref, rrow_ref, rank_ref, gate_ref, w1_ref, w3_ref, w2_ref, o_ref, acc_s):
    e = pl.program_id(0)
    i = pl.program_id(1)
    tm = h_ref.shape[0]
    cap = MOE_CAP
    lane = lax.broadcasted_iota(jnp.int32, (tm, LANES), 1)
    rank_col = jnp.sum(jnp.where(lane == e, rank_ref[...], 0.0), axis=-1, keepdims=True)
    gate_col = jnp.sum(jnp.where(lane == e, gate_ref[...], 0.0), axis=-1, keepdims=True)
    rank_row = rrow_ref[0, pl.ds(e, 1), :]
    slot_r = lax.broadcasted_iota(jnp.int32, (cap, tm), 0).astype(F32)
    slot_c = lax.broadcasted_iota(jnp.int32, (tm, cap), 1).astype(F32)
    acc_s[...] = jnp.zeros_like(acc_s)

    def one_pass(p, carry):
        base = (p * cap).astype(F32)
        gather = jnp.where(rank_row - base == slot_r, 1.0, 0.0).astype(BF16)
        xc = _dot(gather, h_ref[...]).astype(BF16)
        u = _silu(_dot(xc, w1_ref[0])) * _dot(xc, w3_ref[0])
        yc_hi, yc_lo = _split_bf16(_dot(u.astype(BF16), w2_ref[0]))
        scatter = jnp.where(rank_col - base == slot_c, 1.0, 0.0).astype(BF16)
        acc_s[...] += _dot(scatter, yc_hi) + _dot(scatter, yc_lo)
        return carry

    n_pass = (cnt_ref[i, e] + cap - 1) // cap
    lax.fori_loop(0, n_pass, one_pass, 0)
    o_ref[...] = y_ref[...] + gate_col * acc_s[...]


def _moe(x2, g, router, w1, w3, w2, tm):
    t, d = x2.shape
    n_e, _, f = w1.shape
    n_t = t // tm
    router_p = jnp.pad(router, ((0, 0), (0, LANES - n_e)))
    row = lambda w: pl.BlockSpec((tm, w), lambda i: (i, 0))
    h, rank, gate, rrow, cnt = pl.pallas_call(
        _moe_router_kernel,
        out_shape=[jax.ShapeDtypeStruct((t, d), BF16), jax.ShapeDtypeStruct((t, LANES), F32),
                   jax.ShapeDtypeStruct((t, LANES), F32), jax.ShapeDtypeStruct((n_t, N_EXPERTS, tm), F32),
                   jax.ShapeDtypeStruct((n_t, 8, LANES), F32)],
        grid=(n_t,),
        in_specs=[row(d), _full((1, d)), _full((d, LANES))],
        out_specs=[row(d), row(LANES), row(LANES), pl.BlockSpec((1, N_EXPERTS, tm), lambda i: (i, 0, 0)),
                   pl.BlockSpec((1, 8, LANES), lambda i: (i, 0, 0))],
        compiler_params=_cparams("arbitrary"),
        name="moe_router",
    )(x2, g.reshape(1, d), router_p)
    counts = cnt[:, 0, :n_e].astype(jnp.int32)
    tile = lambda w: pl.BlockSpec((tm, w), lambda e, i, c: (i, 0))
    weight = lambda r, c_: pl.BlockSpec((1, r, c_), lambda e, i, c: (e, 0, 0), pipeline_mode=pl.Buffered(1))
    return pl.pallas_call(
        _moe_expert_kernel,
        out_shape=jax.ShapeDtypeStruct((t, d), F32),
        grid_spec=pltpu.PrefetchScalarGridSpec(
            num_scalar_prefetch=1,
            grid=(n_e, n_t),
            in_specs=[tile(d), tile(d), pl.BlockSpec((1, N_EXPERTS, tm), lambda e, i, c: (i, 0, 0)),
                      tile(LANES), tile(LANES), weight(d, f), weight(d, f), weight(f, d)],
            out_specs=tile(d),
            scratch_shapes=[pltpu.VMEM((tm, d), F32)]),
        input_output_aliases={1: 0},
        compiler_params=_cparams("arbitrary", "arbitrary"),
        name="moe_experts",
    )(counts, x2, h, rrow, rank, gate, w1, w3, w2)


def _final_norm_kernel(x_ref, g_ref, o_ref):
    o_ref[...] = _rms_rows(x_ref[...], g_ref[...])


def _final_norm(x2, g, tm):
    t, d = x2.shape
    row = pl.BlockSpec((tm, d), lambda i: (i, 0))
    return pl.pallas_call(
        _final_norm_kernel,
        out_shape=jax.ShapeDtypeStruct((t, d), F32),
        grid=(t // tm,),
        in_specs=[row, _full((1, d))],
        out_specs=row,
        compiler_params=_cparams("arbitrary"),
        name="final_norm",
    )(x2, g.reshape(1, d))


def kernel(x, norm_mix_g, w_in, dsa_g_cq, dsa_g_ckv, dsa_g_kidx, dsa_w_uq, dsa_w_qidx, dsa_w_uv, rwkv_mu, rwkv_w0, rwkv_w_up, rwkv_a0, rwkv_a_up, rwkv_g_up, rwkv_k_k, rwkv_k_a, rwkv_r_k, rwkv_ln_g, rwkv_ln_b, mlstm_conv_w, mlstm_conv_b, mlstm_b_i, mlstm_b_f, mlstm_norm_g, fox_b_f, w_branch, w_out, norm_ffn_g, ffn_w1, ffn_w3, ffn_w2, moe_router, moe_w1, moe_w3, moe_w2, final_norm_g):
    b, s, d = x.shape
    depth = w_in.shape[0]
    t = b * s
    tm = min(512, s)
    for l in range(depth):
        wts = _inproj_weights(w_in[l])
        (zr, zm, fq, fk, fv, ckv, ki, fcol, cqt, ckvt, wit, frow, mkt, mvt, mgt) = _inproj(
            x, norm_mix_g[l], wts, dsa_g_cq[l], dsa_g_ckv[l], dsa_g_kidx[l], fox_b_f[l], tm)
        y_fox = _fox(fq, fk, fv, fcol, frow, min(512, s))
        y_dsa = _dsa(cqt, wit, ki, ckv, ckvt, dsa_w_uq[l], dsa_w_qidx[l], dsa_w_uv[l], min(256, s // 2)
                     ).reshape(t, BRANCH_WIDTH)
        y_rwkv = _rwkv(zr, rwkv_mu[l], rwkv_w0[l], rwkv_w_up[l], rwkv_a0[l], rwkv_a_up[l], rwkv_g_up[l], rwkv_k_k[l],
                       rwkv_k_a[l], rwkv_r_k[l], rwkv_ln_g[l], rwkv_ln_b[l], min(16, b), min(4, b), 32
                       ).reshape(t, BRANCH_WIDTH)
        y_mlstm = _mlstm(zm, mkt, mvt, mgt, mlstm_conv_w[l], mlstm_conv_b[l], mlstm_b_i[l], mlstm_b_f[l],
                         mlstm_norm_g[l]).reshape(t, BRANCH_WIDTH)
        ys = (y_dsa, y_rwkv, y_mlstm, y_fox.reshape(t, BRANCH_WIDTH))
        x2 = _merge(x.reshape(t, d), norm_mix_g[l], ys, wts['w_gate'], w_branch[l].astype(BF16),
                    w_out[l].astype(BF16), tm)
        j = l // 2
        if l % 2 == 0:
            x2 = _ffn(x2, norm_ffn_g[l], ffn_w1[j].astype(BF16), ffn_w3[j].astype(BF16), ffn_w2[j].astype(BF16), tm)
        else:
            x2 = _moe(x2, norm_ffn_g[l], moe_router[j], moe_w1[j].astype(BF16), moe_w3[j].astype(BF16),
                      moe_w2[j].astype(BF16), tm)
        x = x2.reshape(b, s, d)
    return _final_norm(x.reshape(t, d), final_norm_g, tm).reshape(b, s, d)
```

```python
import functools

import jax
import jax.numpy as jnp
from jax import lax
from jax.experimental import pallas as pl
from jax.experimental.pallas import tpu as pltpu

F32 = jnp.float32
BF16 = jnp.bfloat16
HIGHEST = lax.Precision.HIGHEST

EPS = 1e-6
NEG_BIG = -1e30

N_BRANCH = 4
BRANCH_WIDTH = 256
DSA_HEADS = 4
DSA_HEAD_DIM = 64
DSA_Q_LATENT = 128
DSA_KV_LATENT = 128
IDX_HEADS = 8
IDX_DIM = 32
TOPK_MAX = 256
DSA_Q_BLOCK = 128

RWKV_HEADS = 4
RWKV_HEAD_DIM = 64
RWKV_WIDTH = RWKV_HEADS * RWKV_HEAD_DIM
RWKV_LORA_W = 64
RWKV_LORA_A = 64
RWKV_LORA_G = 128
RWKV_GN_EPS = 64e-5
RWKV_IN = 3 * RWKV_WIDTH + RWKV_LORA_W + RWKV_LORA_A + RWKV_LORA_G

MLSTM_HEADS = 4
MLSTM_QK_DIM = 32
MLSTM_V_DIM = 64
MLSTM_CHUNK = 256
CONV_WIDTH = 4
GATE_SOFTCAP = 15.0
MLSTM_QK = MLSTM_HEADS * MLSTM_QK_DIM
MLSTM_V = MLSTM_HEADS * MLSTM_V_DIM

FOX_HEADS = 4
FOX_HEAD_DIM = 64
FOX_WIDTH = FOX_HEADS * FOX_HEAD_DIM

N_EXPERTS = 8

VMEM_LIMIT_BYTES = 56 * 1024 * 1024
LANES = 128


def _cparams(*sem):
    return pltpu.CompilerParams(dimension_semantics=sem, vmem_limit_bytes=VMEM_LIMIT_BYTES)


def _dot(a, b):
    return jnp.dot(a, b, preferred_element_type=F32)


def _dot_hi(a, b):
    return jnp.dot(a, b, preferred_element_type=F32, precision=HIGHEST)


def _dot_nt(a, b):
    return lax.dot_general(a, b, (((1,), (1,)), ((), ())), preferred_element_type=F32)


def _dot_nt_hi(a, b):
    return lax.dot_general(a, b, (((1,), (1,)), ((), ())), preferred_element_type=F32, precision=HIGHEST)


def _log_sigmoid(t):
    return jnp.minimum(t, 0.0) - jnp.log1p(jnp.exp(-jnp.abs(t)))


def _sigmoid(t):
    return 1.0 / (1.0 + jnp.exp(-t))


def _silu(t):
    return t * _sigmoid(t)


def _rms_rows(t, g_row):
    return t * lax.rsqrt(jnp.mean(t * t, axis=-1, keepdims=True) + EPS) * g_row


def _rms_cols(t, g_col):
    return t * lax.rsqrt(jnp.mean(t * t, axis=0, keepdims=True) + EPS) * g_col


def _full(shape):
    n = len(shape)
    return pl.BlockSpec(shape, lambda *_: (0,) * n)


T_CQ = 0
T_CKV = T_CQ + DSA_Q_LATENT
T_WIDX = T_CKV + DSA_KV_LATENT
T_FOXF = T_WIDX + IDX_HEADS
T_MK = T_FOXF + 8
T_MV = T_MK + MLSTM_QK
T_MG = T_MV + MLSTM_V
T_ROWS = T_MG + 8
D_COLS = 3 * LANES
M_COLS = 2 * MLSTM_QK + 2 * MLSTM_V + LANES


def _inproj_kernel(x_ref, g_ref, wr_ref, wm_ref, wf_ref, wd_ref, wt_ref,
                   gcq_ref, gckv_col_ref, gckv_row_ref, gki_ref, bf_row_ref, bf_col_ref,
                   zr_ref, zm_ref, q_ref, k_ref, v_ref, ckv_ref, ki_ref, fcol_ref,
                   cqt_ref, ckvt_ref, wit_ref, frow_ref, mkt_ref, mvt_ref, mgt_ref,
                   carry_row, carry_col):
    j = pl.program_id(1)
    tm = x_ref.shape[1]

    @pl.when(j == 0)
    def _():
        carry_row[...] = jnp.zeros_like(carry_row)
        carry_col[...] = jnp.zeros_like(carry_col)

    x = x_ref[0]
    h = _rms_rows(x, g_ref[...]).astype(BF16)

    zr_ref[0] = _dot_nt(h, wr_ref[...])
    zm_ref[0] = _dot_nt(h, wm_ref[...])

    zf = _dot_nt(h, wf_ref[...])
    for hd in range(FOX_HEADS):
        lo = hd * FOX_HEAD_DIM
        q_ref[0, hd] = (zf[:, lo:lo + FOX_HEAD_DIM] * FOX_HEAD_DIM ** -0.5).astype(BF16)
        k_ref[0, hd] = zf[:, FOX_WIDTH + lo:FOX_WIDTH + lo + FOX_HEAD_DIM].astype(BF16)
        v_ref[0, hd] = zf[:, 2 * FOX_WIDTH + lo:2 * FOX_WIDTH + lo + FOX_HEAD_DIM].astype(BF16)

    zd = _dot_nt(h, wd_ref[...])
    ckv_ref[0] = _rms_rows(zd[:, :LANES], gckv_row_ref[...]).astype(BF16)
    ki_pieces = [p.astype(F32) for p in _split3_bf16(_rms_rows(zd[:, LANES:LANES + IDX_DIM], gki_ref[...]))]
    ki_pad = jnp.zeros((tm, IDX_K_COLS - len(IDX_SPLIT_K) * IDX_DIM), F32)
    ki_ref[0] = jnp.concatenate([ki_pieces[kp] for kp, _ in IDX_SPLIT_K] + [ki_pad], axis=1).astype(BF16)
    lf_col = _log_sigmoid(zd[:, 2 * LANES:] + bf_row_ref[...])
    r_i = lax.broadcasted_iota(jnp.int32, (tm, tm), 0)
    c_i = lax.broadcasted_iota(jnp.int32, (tm, tm), 1)
    tri = jnp.where(c_i <= r_i, 1.0, 0.0).astype(F32)
    cum_col = _dot_hi(tri, lf_col) + carry_row[...]
    fcol_ref[0] = cum_col
    carry_row[...] = cum_col[tm - 1:tm, :]

    zt = _dot_nt(wt_ref[...], h)
    cqt_ref[0] = _rms_cols(zt[T_CQ:T_CKV], gcq_ref[...])
    ckvt_ref[0] = _rms_cols(zt[T_CKV:T_WIDX], gckv_col_ref[...]).astype(BF16)
    wit_ref[0] = zt[T_WIDX:T_FOXF] * IDX_HEADS ** -0.5
    lf_row = _log_sigmoid(zt[T_FOXF:T_MK] + bf_col_ref[...])
    cum_row = _dot_nt_hi(lf_row, tri) + carry_col[:, 0:1]
    frow_ref[0] = cum_row
    carry_col[...] = jnp.broadcast_to(cum_row[:, tm - 1:tm], carry_col.shape)
    mkt_ref[0] = zt[T_MK:T_MV]
    mvt_ref[0] = zt[T_MV:T_MG]
    mgt_ref[0] = zt[T_MG:T_ROWS]


def _inproj_weights(w_in_l):
    wt = w_in_l.T
    o = 0
    w_cq = wt[o:o + DSA_Q_LATENT]; o += DSA_Q_LATENT
    w_ckv = wt[o:o + DSA_KV_LATENT]; o += DSA_KV_LATENT
    w_kidx = wt[o:o + IDX_DIM]; o += IDX_DIM
    w_widx = wt[o:o + IDX_HEADS]; o += IDX_HEADS
    w_rwkv = wt[o:o + RWKV_IN]; o += RWKV_IN
    w_mq = wt[o:o + MLSTM_QK]; o += MLSTM_QK
    w_mk = wt[o:o + MLSTM_QK]; o += MLSTM_QK
    w_mv = wt[o:o + MLSTM_V]; o += MLSTM_V
    w_mo = wt[o:o + MLSTM_V]; o += MLSTM_V
    w_mi = wt[o:o + MLSTM_HEADS]; o += MLSTM_HEADS
    w_mf = wt[o:o + MLSTM_HEADS]; o += MLSTM_HEADS
    w_fox = wt[o:o + 3 * FOX_WIDTH]; o += 3 * FOX_WIDTH
    w_ff = wt[o:o + FOX_HEADS]; o += FOX_HEADS
    w_gate = wt[o:]

    def padr(w, n):
        return jnp.pad(w, ((0, n - w.shape[0]), (0, 0)))

    w_d = jnp.concatenate([w_ckv, padr(w_kidx, LANES), padr(w_ff, LANES)], axis=0)
    w_m = jnp.concatenate([w_mq, w_mk, w_mv, w_mo, padr(jnp.concatenate([w_mi, w_mf], axis=0), LANES)], axis=0)
    w_t = jnp.concatenate([w_cq, w_ckv, w_widx, padr(w_ff, 8), w_mk, w_mv, w_mi, w_mf], axis=0)
    assert w_t.shape[0] == T_ROWS
    cast = lambda w: w.astype(BF16)
    return dict(w_r=cast(w_rwkv), w_m=cast(w_m), w_f=cast(w_fox), w_d=cast(w_d), w_t=cast(w_t), w_gate=cast(w_gate))


def _inproj(x, g, wts, g_cq, g_ckv, g_kidx, fox_b_f, tm):
    b, s, d = x.shape
    nj = s // tm
    bf_row = jnp.pad(fox_b_f, (0, LANES - FOX_HEADS)).reshape(1, LANES)
    bf_col = jnp.pad(fox_b_f, (0, 8 - FOX_HEADS)).reshape(8, 1)
    row = lambda w: pl.BlockSpec((1, tm, w), lambda i, j: (i, j, 0))
    head = pl.BlockSpec((1, FOX_HEADS, tm, FOX_HEAD_DIM), lambda i, j: (i, 0, j, 0))
    col = lambda r: pl.BlockSpec((1, r, tm), lambda i, j: (i, 0, j))
    out_shape = [
        jax.ShapeDtypeStruct((b, s, RWKV_IN), F32),
        jax.ShapeDtypeStruct((b, s, M_COLS), F32),
        jax.ShapeDtypeStruct((b, FOX_HEADS, s, FOX_HEAD_DIM), BF16),
        jax.ShapeDtypeStruct((b, FOX_HEADS, s, FOX_HEAD_DIM), BF16),
        jax.ShapeDtypeStruct((b, FOX_HEADS, s, FOX_HEAD_DIM), BF16),
        jax.ShapeDtypeStruct((b, s, DSA_KV_LATENT), BF16),
        jax.ShapeDtypeStruct((b, s, IDX_K_COLS), BF16),
        jax.ShapeDtypeStruct((b, s, LANES), F32),
        jax.ShapeDtypeStruct((b, DSA_Q_LATENT, s), F32),
        jax.ShapeDtypeStruct((b, DSA_KV_LATENT, s), BF16),
        jax.ShapeDtypeStruct((b, IDX_HEADS, s), F32),
        jax.ShapeDtypeStruct((b, 8, s), F32),
        jax.ShapeDtypeStruct((b, MLSTM_QK, s), F32),
        jax.ShapeDtypeStruct((b, MLSTM_V, s), F32),
        jax.ShapeDtypeStruct((b, 8, s), F32),
    ]
    out_specs = [row(RWKV_IN), row(M_COLS), head, head, head, row(DSA_KV_LATENT), row(IDX_K_COLS), row(LANES),
                 col(DSA_Q_LATENT), col(DSA_KV_LATENT), col(IDX_HEADS), col(8), col(MLSTM_QK), col(MLSTM_V), col(8)]
    in_specs = [row(d), _full((1, d)), _full(wts['w_r'].shape), _full(wts['w_m'].shape), _full(wts['w_f'].shape),
                _full(wts['w_d'].shape), _full(wts['w_t'].shape),
                _full((DSA_Q_LATENT, 1)), _full((DSA_KV_LATENT, 1)), _full((1, DSA_KV_LATENT)), _full((1, IDX_DIM)),
                _full((1, LANES)), _full((8, 1))]
    return pl.pallas_call(
        _inproj_kernel,
        out_shape=out_shape,
        grid=(b, nj),
        in_specs=in_specs,
        out_specs=out_specs,
        scratch_shapes=[pltpu.VMEM((1, LANES), F32), pltpu.VMEM((8, LANES), F32)],
        compiler_params=_cparams("arbitrary", "arbitrary"),
        name="inproj",
    )(x, g.reshape(1, d), wts['w_r'], wts['w_m'], wts['w_f'], wts['w_d'], wts['w_t'],
      g_cq.reshape(-1, 1), g_ckv.reshape(-1, 1), g_ckv.reshape(1, -1), g_kidx.reshape(1, -1), bf_row, bf_col)


def _fox_kernel(q_ref, k_ref, v_ref, fcol_ref, frow_ref, o_ref):
    qi = pl.program_id(1)
    tq = q_ref.shape[2]
    outs = []
    for hd in range(FOX_HEADS):
        q = q_ref[0, hd]
        fq = fcol_ref[0][:, hd:hd + 1]

        def scores(j):
            start = pl.multiple_of(j * tq, tq)
            k = k_ref[0, hd, pl.ds(start, tq), :]
            v = v_ref[0, hd, pl.ds(start, tq), :]
            fk = frow_ref[0, hd:hd + 1, pl.ds(start, tq)]
            return _dot_nt(q, k) + (fq - fk), v

        def update(carry, sc, v):
            m, l, acc = carry
            m_new = jnp.maximum(m, jnp.max(sc, axis=-1, keepdims=True))
            alpha = jnp.exp(m - m_new)
            p = jnp.exp(sc - m_new)
            l = alpha * l + jnp.sum(p, axis=-1, keepdims=True)
            acc = alpha * acc + _dot(p.astype(BF16), v)
            return m_new, l, acc

        def body(j, carry):
            sc, v = scores(j)
            return update(carry, sc, v)

        init = (jnp.full((tq, 1), NEG_BIG, F32), jnp.zeros((tq, 1), F32), jnp.zeros((tq, FOX_HEAD_DIM), F32))
        carry = lax.fori_loop(0, qi, body, init)
        sc, v = scores(qi)
        r_i = lax.broadcasted_iota(jnp.int32, (tq, tq), 0)
        c_i = lax.broadcasted_iota(jnp.int32, (tq, tq), 1)
        sc = jnp.where(c_i <= r_i, sc, NEG_BIG)
        m, l, acc = update(carry, sc, v)
        outs.append(acc / l)
    o_ref[0] = jnp.concatenate(outs, axis=-1).astype(o_ref.dtype)


def _fox(q, k, v, fcol, frow, tq):
    b, nh, s, d = q.shape
    return pl.pallas_call(
        _fox_kernel,
        out_shape=jax.ShapeDtypeStruct((b, s, nh * d), BF16),
        grid=(b, s // tq),
        in_specs=[
            pl.BlockSpec((1, nh, tq, d), lambda i, j: (i, 0, j, 0)),
            pl.BlockSpec((1, nh, s, d), lambda i, j: (i, 0, 0, 0)),
            pl.BlockSpec((1, nh, s, d), lambda i, j: (i, 0, 0, 0)),
            pl.BlockSpec((1, tq, LANES), lambda i, j: (i, j, 0)),
            pl.BlockSpec((1, 8, s), lambda i, j: (i, 0, 0)),
        ],
        out_specs=pl.BlockSpec((1, tq, nh * d), lambda i, j: (i, j, 0)),
        compiler_params=_cparams("arbitrary", "arbitrary"),
        name="fox",
    )(q, k, v, fcol, frow)


INT_MIN = -2 ** 31


IDX_SPLIT_K = ((0, 0), (0, 1), (1, 0), (0, 2), (1, 1), (2, 0))
IDX_K_COLS = 2 * LANES


def _split3_bf16(t):
    p0 = t.astype(BF16)
    r1 = t - p0.astype(F32)
    p1 = r1.astype(BF16)
    p2 = (r1 - p1.astype(F32)).astype(BF16)
    return p0, p1, p2


def _tree_sum(parts):
    while len(parts) > 1:
        parts = [parts[i] + parts[i + 1] for i in range(0, len(parts) - 1, 2)] + (parts[-1:] if len(parts) % 2 else [])
    return parts[0]


def _dsa_kernel(cqt_ref, wit_ref, ki6_ref, ckv_ref, ckvt_ref, wuqt_ref, wqit_ref, wuvt_ref, o_ref, key_s, *, ck):
    qb = pl.program_id(1)
    s = ki6_ref.shape[1]
    nq = cqt_ref.shape[2]
    topk = min(TOPK_MAX, s // 4)
    n_ck = (qb * nq + nq + ck - 1) // ck
    lat = DSA_KV_LATENT

    cq = cqt_ref[0]
    qi_t = _dot_hi(wqit_ref[...], cq)
    zeros_pad = jnp.zeros((IDX_K_COLS - len(IDX_SPLIT_K) * IDX_DIM, nq), BF16)
    q_blocks = []
    for hd in range(IDX_HEADS):
        pieces = _split3_bf16(qi_t[hd * IDX_DIM:(hd + 1) * IDX_DIM, :])
        q_blocks.append(jnp.concatenate([pieces[qp] for _, qp in IDX_SPLIT_K] + [zeros_pad], axis=0))
    q6 = jnp.concatenate(q_blocks, axis=1)
    wi = wit_ref[0] * IDX_DIM ** -0.5
    tpos = qb * nq + lax.broadcasted_iota(jnp.int32, (ck, nq), 1)
    row = lax.broadcasted_iota(jnp.int32, (ck, nq), 0)

    def chunk_start(c):
        return pl.multiple_of(c * ck, ck)

    def score_chunk(c, carry):
        c0 = chunk_start(c)
        ki6 = ki6_ref[0, pl.ds(c0, ck), :]
        score = None
        for hp in range(0, IDX_HEADS, 2):
            dots = _dot(ki6, q6[:, hp * nq:(hp + 2) * nq])
            for i in range(2):
                term = wi[hp + i:hp + i + 1, :] * jnp.maximum(dots[:, i * nq:(i + 1) * nq], 0.0)
                score = term if score is None else score + term
        bits = pltpu.bitcast(score, jnp.int32)
        key = jnp.where(bits < 0, bits ^ 0x7FFFFFFF, bits)
        key = jnp.where(score == 0.0, 0, key)
        key_s[pl.ds(c0, ck), :] = jnp.where(c0 + row <= tpos, key, INT_MIN)
        return carry

    n_pair = (n_ck + 1) // 2
    lax.fori_loop(0, n_pair, lambda j, carry: score_chunk(2 * j + 1, score_chunk(2 * j, carry)), 0)

    def count(mask_fn):
        def body(j, acc):
            parts = []
            for c in (2 * j, 2 * j + 1):
                c0 = chunk_start(c)
                ones = jnp.where(mask_fn(key_s[pl.ds(c0, ck), :], c0 + row), 1, 0)
                parts += [ones[i * 8:(i + 1) * 8] for i in range(ck // 8)]
            return acc + _tree_sum(parts)
        acc = lax.fori_loop(0, n_pair, body, jnp.zeros((8, nq), jnp.int32))
        return jnp.sum(acc, axis=0, keepdims=True)

    def value_bit(i, carry):
        lo, n_lo = carry
        cand = lo + jnp.left_shift(jnp.int32(1), 31 - i)
        n_cand = count(lambda key, pos: key >= cand)
        ok = n_cand >= topk
        return jnp.where(ok, cand, lo), jnp.where(ok, n_cand, n_lo)

    thr, n_ge = lax.fori_loop(0, 32, value_bit, (jnp.full((1, nq), INT_MIN, jnp.int32),
                                                 jnp.full((1, nq), s, jnp.int32)))

    n_bits = s.bit_length()

    def tie_search():
        need = topk - count(lambda key, pos: key > thr)

        def index_bit(i, lo):
            cand = lo + jnp.left_shift(jnp.int32(1), n_bits - 1 - i)
            return jnp.where(count(lambda key, pos: (key == thr) & (pos < cand)) < need, cand, lo)

        return lax.fori_loop(0, n_bits, index_bit, jnp.zeros((1, nq), jnp.int32))

    surplus = jnp.max(jnp.where((n_ge > topk) & (thr > INT_MIN), 1, 0))
    last = lax.cond(surplus > 0, tie_search, lambda: jnp.full((1, nq), s, jnp.int32))

    q_t = (_dot(wuqt_ref[...], cq.astype(BF16)) * lat ** -0.5).astype(BF16)

    def attend(c, carry):
        c0 = chunk_start(c)
        key = key_s[pl.ds(c0, ck), :]
        pos = c0 + row
        sel = ((key > thr) | ((key == thr) & (pos <= last))) & (pos <= tpos)
        bias = jnp.where(sel, 0.0, NEG_BIG)
        ckv = ckv_ref[0, pl.ds(c0, ck), :]
        ckvt = ckvt_ref[0, :, pl.ds(c0, ck)]
        new = []
        for hd in range(DSA_HEADS):
            m, l, acc = carry[hd]
            lg = _dot(ckv, q_t[hd * lat:(hd + 1) * lat, :]) + bias
            m_new = jnp.maximum(m, jnp.max(lg, axis=0, keepdims=True))
            alpha = jnp.exp(m - m_new)
            p = jnp.exp(lg - m_new)
            new.append((m_new, alpha * l + jnp.sum(p, axis=0, keepdims=True),
                        alpha * acc + _dot(ckvt, p.astype(BF16))))
        return tuple(new)

    init = tuple((jnp.full((1, nq), NEG_BIG, F32), jnp.zeros((1, nq), F32), jnp.zeros((lat, nq), F32))
                 for _ in range(DSA_HEADS))
    final = lax.fori_loop(0, n_pair, lambda j, carry: attend(2 * j + 1, attend(2 * j, carry)), init)
    outs = [_dot(wuvt_ref[hd], (acc / l).astype(BF16)) for hd, (_, l, acc) in enumerate(final)]
    o_ref[0] = jnp.concatenate(outs, axis=0).T.astype(o_ref.dtype)


def _dsa(cqt, wit, ki6, ckv, ckvt, w_uq, w_qidx, w_uv, ck):
    b, c, s = cqt.shape
    nq = DSA_Q_BLOCK
    wuqt = w_uq.reshape(c, -1).T.astype(BF16)
    wqit = w_qidx.reshape(c, -1).T
    wuvt = jnp.transpose(w_uv, (0, 2, 1)).astype(BF16)
    return pl.pallas_call(
        functools.partial(_dsa_kernel, ck=ck),
        out_shape=jax.ShapeDtypeStruct((b, s, DSA_HEADS * DSA_HEAD_DIM), BF16),
        grid=(b, s // nq),
        in_specs=[pl.BlockSpec((1, c, nq), lambda i, j: (i, 0, j)),
                  pl.BlockSpec((1, IDX_HEADS, nq), lambda i, j: (i, 0, j)),
                  pl.BlockSpec((1, s, IDX_K_COLS), lambda i, j: (i, 0, 0)),
                  pl.BlockSpec((1, s, DSA_KV_LATENT), lambda i, j: (i, 0, 0)),
                  pl.BlockSpec((1, DSA_KV_LATENT, s), lambda i, j: (i, 0, 0)),
                  _full(wuqt.shape), _full(wqit.shape), _full(wuvt.shape)],
        out_specs=pl.BlockSpec((1, nq, DSA_HEADS * DSA_HEAD_DIM), lambda i, j: (i, j, 0)),
        scratch_shapes=[pltpu.VMEM((s, nq), jnp.int32)],
        compiler_params=_cparams("arbitrary", "arbitrary"),
        name="dsa",
    )(cqt, wit, ki6, ckv, ckvt, wuqt, wqit, wuvt)


Y_PAD = 8


def _head_ones(n, dtype):
    r_i = lax.broadcasted_iota(jnp.int32, (n, n), 0) // RWKV_HEAD_DIM
    c_i = lax.broadcasted_iota(jnp.int32, (n, n), 1) // RWKV_HEAD_DIM
    return jnp.where(r_i == c_i, 1.0, 0.0).astype(dtype)


def _split_bf16(t):
    hi = t.astype(BF16)
    lo = (t - hi.astype(F32)).astype(BF16)
    return hi, lo


def _rwkv_kernel(z_ref, mu_ref, w0_ref, wup_ref, a0_ref, aup_ref, gup_ref, kk_ref, ka_ref, rk_ref, lng_ref, lnb_ref,
                 o_ref, st_s, prev_s, step_s, y_s, bonus_s, gate_s, *, grp):
    c = pl.program_id(1)
    n_g, tc, _ = z_ref.shape
    hd, wd_ = RWKV_HEAD_DIM, RWKV_WIDTH

    @pl.when(c == 0)
    def _():
        st_s[...] = jnp.zeros_like(st_s)
        prev_s[...] = jnp.zeros_like(prev_s)

    ones_b = _head_ones(wd_, BF16)
    diag = jnp.where(lax.broadcasted_iota(jnp.int32, (hd, wd_), 0)
                     == lax.broadcasted_iota(jnp.int32, (hd, wd_), 1) % hd, 1.0, 0.0).astype(F32)

    def head_sum(t):
        hi, lo = _split_bf16(t)
        return _dot(hi, ones_b) + _dot(lo, ones_b)

    zs = []
    for g in range(n_g):
        z = z_ref[g]
        row = lax.broadcasted_iota(jnp.int32, z.shape, 0)
        z_prev = jnp.where(row == 0, prev_s[g], pltpu.roll(z, 1, 0))
        prev_s[g] = z[tc - 1:tc, :]
        zs.append(z + mu_ref[...] * (z_prev - z))
    z = jnp.concatenate(zs, axis=0)
    r = z[:, 0:wd_]
    k = z[:, wd_:2 * wd_]
    v = z[:, 2 * wd_:3 * wd_]
    o = 3 * wd_
    w_lora = z[:, o:o + RWKV_LORA_W]
    a_lora = z[:, o + RWKV_LORA_W:o + RWKV_LORA_W + RWKV_LORA_A]
    g_lora = z[:, o + RWKV_LORA_W + RWKV_LORA_A:]
    w_log = _log_sigmoid(w0_ref[...] + _dot_hi(jnp.tanh(w_lora), wup_ref[...])) - 0.5
    a = _sigmoid(a0_ref[...] + _dot_hi(a_lora, aup_ref[...]))
    gate_s[...] = _dot_hi(_sigmoid(g_lora), gup_ref[...])
    kk = k * kk_ref[...]
    kk = kk / jnp.maximum(jnp.sqrt(head_sum(kk * kk)), 1e-12)
    k = k * (1.0 + (a - 1.0) * ka_ref[...])
    step_s[0] = jnp.exp(-jnp.exp(w_log))
    step_s[1] = -kk
    step_s[2] = kk * a
    step_s[3] = k
    step_s[4] = r
    step_s[5] = v
    bonus_s[...] = head_sum(r * k * rk_ref[...]) * v

    n_grp = n_g // grp

    def rows(kind, g0, t):
        return jnp.concatenate(
            [jnp.broadcast_to(step_s[kind, pl.ds((g0 + i) * tc + t, 1), :], (hd, wd_)) for i in range(grp)], axis=0)

    def store_y(yb, g0, t_write):
        yb = yb * diag_g
        for i in range(grp):
            tile = jnp.sum(yb[i * hd:(i + 1) * hd].reshape(hd // 8, 8, wd_), axis=0)
            y_s[pl.ds(Y_PAD + (g0 + i) * tc + t_write, 1), :] = jnp.sum(tile, axis=0, keepdims=True)

    def step(t, carry):
        n = grp * hd
        for q in range(n_grp):
            g0 = q * grp
            st = st_s[q]
            lhs = jnp.concatenate([st * rows(1, g0, t),
                                   rows(5, g0, t) * diag_g,
                                   st * rows(4, g0, jnp.maximum(t - 1, 0))], axis=0)
            sums = _dot(lhs.astype(BF16), ones_b)
            store_y(sums[2 * n:], g0, t - 1)
            st_s[q] = st * rows(0, g0, t) + sums[:n] * rows(2, g0, t) + sums[n:2 * n] * rows(3, g0, t)
        return carry

    diag_g = jnp.concatenate([diag] * grp, axis=0)
    lax.fori_loop(0, tc, step, 0)
    for q in range(n_grp):
        store_y(_dot((st_s[q] * rows(4, q * grp, tc - 1)).astype(BF16), ones_b), q * grp, tc - 1)

    y = y_s[pl.ds(Y_PAD, n_g * tc), :]
    mean = head_sum(y) * (1.0 / hd)
    yc = y - mean
    var = head_sum(yc * yc) * (1.0 / hd)
    yn = yc * lax.rsqrt(var + RWKV_GN_EPS) * lng_ref[...] + lnb_ref[...]
    out = ((yn + bonus_s[...]) * gate_s[...]).astype(o_ref.dtype)
    for g in range(n_g):
        o_ref[g] = out[g * tc:(g + 1) * tc]


def _rwkv(zr, mu, w0, w_up, a0, a_up, g_up, k_k, k_a, r_k, ln_g, ln_b, n_g, grp, tc):
    b, s, zin = zr.shape
    wd_ = RWKV_WIDTH
    vec = lambda p: p.reshape(1, -1)
    params = [vec(mu), vec(w0), w_up, vec(a0), a_up, g_up, vec(k_k), vec(k_a), vec(r_k), vec(ln_g), vec(ln_b)]
    return pl.pallas_call(
        functools.partial(_rwkv_kernel, grp=grp),
        out_shape=jax.ShapeDtypeStruct((b, s, wd_), BF16),
        grid=(b // n_g, s // tc),
        in_specs=[pl.BlockSpec((n_g, tc, zin), lambda i, c: (i, c, 0))] + [_full(p.shape) for p in params],
        out_specs=pl.BlockSpec((n_g, tc, wd_), lambda i, c: (i, c, 0)),
        scratch_shapes=[pltpu.VMEM((n_g // grp, grp * RWKV_HEAD_DIM, wd_), F32), pltpu.VMEM((n_g, 1, zin), F32),
                        pltpu.VMEM((6, n_g * tc, wd_), F32), pltpu.VMEM((Y_PAD + n_g * tc, wd_), F32),
                        pltpu.VMEM((n_g * tc, wd_), F32), pltpu.VMEM((n_g * tc, wd_), F32)],
        compiler_params=_cparams("arbitrary", "arbitrary"),
        name="rwkv",
    )(zr, *params)


def _softcap(t):
    return GATE_SOFTCAP * jnp.tanh(t / GATE_SOFTCAP)


def _mlstm_kernel(zm_ref, mkt_ref, mvt_ref, mgt_ref, cw_row_ref, cb_row_ref, cw_col_ref, cb_col_ref,
                  bg_row_ref, bg_col_ref, ng_ref, o_ref, q_s, k_s, kt_s, gc_s, gr_s, yt_s):
    s = zm_ref.shape[1]
    nh, dk, dv, lc = MLSTM_HEADS, MLSTM_QK_DIM, MLSTM_V_DIM, MLSTM_CHUNK
    pair = 2 * lc

    qk = zm_ref[0, :, 0:2 * MLSTM_QK]
    pos_r = lax.broadcasted_iota(jnp.int32, qk.shape, 0)
    acc = cb_row_ref[...] + qk * cw_row_ref[CONV_WIDTH - 1:CONV_WIDTH, :]
    for r in range(1, CONV_WIDTH):
        sh = jnp.where(pos_r >= r, pltpu.roll(qk, r, 0), 0.0)
        acc = acc + sh * cw_row_ref[CONV_WIDTH - 1 - r:CONV_WIDTH - r, :]
    acc = _silu(acc)
    q_s[...] = acc[:, :MLSTM_QK] * dk ** -0.5
    k_s[...] = acc[:, MLSTM_QK:]
    kt = mkt_ref[0]
    pos_c = lax.broadcasted_iota(jnp.int32, kt.shape, 1)
    acc_t = cb_col_ref[...] + kt * cw_col_ref[:, CONV_WIDTH - 1:CONV_WIDTH]
    for r in range(1, CONV_WIDTH):
        sh = jnp.where(pos_c >= r, pltpu.roll(kt, r, 1), 0.0)
        acc_t = acc_t + sh * cw_col_ref[:, CONV_WIDTH - 1 - r:CONV_WIDTH - r]
    kt_s[...] = _silu(acc_t)

    gcol = _softcap(zm_ref[0, :, 2 * MLSTM_QK + 2 * MLSTM_V:] + bg_row_ref[...])
    lane = lax.broadcasted_iota(jnp.int32, gcol.shape, 1)
    gc_s[...] = jnp.where(lane < nh, gcol, _log_sigmoid(gcol))
    grow = _softcap(mgt_ref[0] + bg_col_ref[...])
    sub = lax.broadcasted_iota(jnp.int32, grow.shape, 0)
    gr_s[...] = jnp.where(sub < nh, grow, _log_sigmoid(grow))

    r_i = lax.broadcasted_iota(jnp.int32, (lc, lc), 0)
    c_i = lax.broadcasted_iota(jnp.int32, (lc, lc), 1)
    tri = jnp.where(c_i <= r_i, 1.0, 0.0).astype(F32)
    causal_t = r_i <= c_i
    ones_rows = jnp.ones((8, lc), F32)

    def chunk_pair(p, carry):
        base = pl.multiple_of(p * pair, pair)
        gr_slab = gr_s[:, pl.ds(base, pair)]
        kt_slab = kt_s[:, pl.ds(base, pair)]
        vt_slab = mvt_ref[0, :, pl.ds(base, pair)]
        outs = [[] for _ in range(nh)]
        for sc in range(2):
            r0 = base + sc * lc
            gcc = gc_s[pl.ds(r0, lc), :]
            grc = gr_slab[:, sc * lc:(sc + 1) * lc]
            bcum_col = _dot_hi(tri, gcc)
            bcum_row = _dot_nt_hi(grc, tri)
            qc = q_s[pl.ds(r0, lc), :]
            kc = k_s[pl.ds(r0, lc), :]
            new_carry = []
            for hd in range(nh):
                c_aug, m_prev = carry[hd]
                bc_c = bcum_col[:, nh + hd:nh + hd + 1]
                li_c = gcc[:, hd:hd + 1]
                bc_r = bcum_row[nh + hd:nh + hd + 1, :]
                li_r = grc[hd:hd + 1, :]
                gtot = bc_r[:, lc - 1:lc]
                d_t = jnp.where(causal_t, bc_r - bc_c + li_c, NEG_BIG)
                m_inter = bc_r + m_prev
                m_t = jnp.maximum(m_inter, jnp.max(d_t, axis=0, keepdims=True))
                q_h = qc[:, hd * dk:(hd + 1) * dk].astype(BF16)
                k_h = kc[:, hd * dk:(hd + 1) * dk].astype(BF16)
                kt_h = kt_slab[hd * dk:(hd + 1) * dk, sc * lc:(sc + 1) * lc].astype(BF16)
                vt_h = vt_slab[hd * dv:(hd + 1) * dv, sc * lc:(sc + 1) * lc]
                vt_aug = jnp.concatenate([vt_h, ones_rows], axis=0)
                s_t = _dot_nt(k_h, q_h)
                w_t = jnp.exp(d_t - m_t) * s_t
                s_inter = jnp.exp(m_inter - m_t)
                numden = _dot(vt_aug.astype(BF16), w_t.astype(BF16)) + s_inter * _dot_nt(c_aug.astype(BF16), q_h)
                den = numden[dv:dv + 1, :]
                outs[hd].append(numden[:dv, :] / jnp.maximum(jnp.abs(den), jnp.exp(-m_t)))
                a_log = gtot - bc_r + li_r
                a_max = jnp.max(a_log, axis=-1, keepdims=True)
                a_w = jnp.exp(a_log - a_max)
                kvn = _dot_nt((vt_aug * a_w).astype(BF16), kt_h)
                m_new = jnp.maximum(gtot + m_prev, a_max)
                s_old = jnp.exp(gtot + m_prev - m_new)
                s_new = jnp.exp(a_max - m_new)
                new_carry.append((s_old * c_aug + s_new * kvn, m_new))
            carry = tuple(new_carry)
        for hd in range(nh):
            yt_s[hd * dv:(hd + 1) * dv, pl.ds(base, pair)] = jnp.concatenate(outs[hd], axis=-1)
        return carry

    init = tuple((jnp.zeros((dv + 8, dk), F32), jnp.full((1, 1), NEG_BIG, F32)) for _ in range(nh))
    lax.fori_loop(0, s // pair, chunk_pair, init)

    parts = []
    for hd in range(nh):
        blk = yt_s[hd * dv:(hd + 1) * dv, :]
        parts.append(blk * lax.rsqrt(jnp.mean(blk * blk, axis=0, keepdims=True) + EPS))
    y = jnp.concatenate(parts, axis=0).T
    o_gate = _sigmoid(zm_ref[0, :, 2 * MLSTM_QK + MLSTM_V:2 * MLSTM_QK + 2 * MLSTM_V])
    o_ref[0] = (y * ng_ref[...] * o_gate).astype(o_ref.dtype)


def _mlstm(zm, mkt, mvt, mgt, conv_w, conv_b, b_i, b_f, norm_g):
    b, s, _ = zm.shape
    cw_col = conv_w[:, MLSTM_QK:].T
    cb_col = conv_b[MLSTM_QK:].reshape(-1, 1)
    bg = jnp.concatenate([b_i, b_f])
    bg_row = jnp.pad(bg, (0, LANES - 2 * MLSTM_HEADS)).reshape(1, LANES)
    bg_col = bg.reshape(-1, 1)
    per_b = lambda r, c: pl.BlockSpec((1, r, c), lambda i: (i, 0, 0))
    return pl.pallas_call(
        _mlstm_kernel,
        out_shape=jax.ShapeDtypeStruct((b, s, MLSTM_V), BF16),
        grid=(b,),
        in_specs=[per_b(s, M_COLS), per_b(MLSTM_QK, s), per_b(MLSTM_V, s), per_b(8, s),
                  _full(conv_w.shape), _full((1, 2 * MLSTM_QK)), _full(cw_col.shape), _full(cb_col.shape),
                  _full((1, LANES)), _full((8, 1)), _full((1, MLSTM_V))],
        out_specs=per_b(s, MLSTM_V),
        scratch_shapes=[pltpu.VMEM((s, MLSTM_QK), F32), pltpu.VMEM((s, MLSTM_QK), F32), pltpu.VMEM((MLSTM_QK, s), F32),
                        pltpu.VMEM((s, LANES), F32), pltpu.VMEM((8, s), F32), pltpu.VMEM((MLSTM_V, s), F32)],
        compiler_params=_cparams("arbitrary"),
        name="mlstm",
    )(zm, mkt, mvt, mgt, conv_w, conv_b.reshape(1, -1), cw_col, cb_col, bg_row, bg_col, norm_g.reshape(1, -1))


def _merge_kernel(x_ref, g_ref, y0_ref, y1_ref, y2_ref, y3_ref, wg_ref, wb_ref, wo_ref, o_ref):
    x = x_ref[...]
    d = x.shape[1]
    h = _rms_rows(x, g_ref[...]).astype(BF16)
    merged = None
    for n, y_ref in enumerate((y0_ref, y1_ref, y2_ref, y3_ref)):
        gate = _sigmoid(_dot_nt(h, wg_ref[n * d:(n + 1) * d, :]))
        term = gate * _dot(y_ref[...], wb_ref[n])
        merged = term if merged is None else merged + term
    o_ref[...] = x + _dot(merged.astype(BF16), wo_ref[...])


def _merge(x2, g, ys, w_gate, w_branch, w_out, tm):
    t, d = x2.shape
    row = lambda w: pl.BlockSpec((tm, w), lambda i: (i, 0))
    return pl.pallas_call(
        _merge_kernel,
        out_shape=jax.ShapeDtypeStruct((t, d), F32),
        grid=(t // tm,),
        in_specs=[row(d), _full((1, d))] + [row(BRANCH_WIDTH)] * N_BRANCH
                 + [_full(w_gate.shape), _full(w_branch.shape), _full(w_out.shape)],
        out_specs=row(d),
        compiler_params=_cparams("arbitrary"),
        name="merge",
    )(x2, g.reshape(1, d), *ys, w_gate, w_branch, w_out)


def _ffn_kernel(x_ref, g_ref, w1_ref, w3_ref, w2_ref, o_ref):
    x = x_ref[...]
    h = _rms_rows(x, g_ref[...]).astype(BF16)
    u = _silu(_dot(h, w1_ref[...])) * _dot(h, w3_ref[...])
    o_ref[...] = x + _dot(u.astype(BF16), w2_ref[...])


def _ffn(x2, g, w1, w3, w2, tm):
    t, d = x2.shape
    row = pl.BlockSpec((tm, d), lambda i: (i, 0))
    return pl.pallas_call(
        _ffn_kernel,
        out_shape=jax.ShapeDtypeStruct((t, d), F32),
        grid=(t // tm,),
        in_specs=[row, _full((1, d)), _full(w1.shape), _full(w3.shape), _full(w2.shape)],
        out_specs=row,
        compiler_params=_cparams("arbitrary"),
        name="ffn",
    )(x2, g.reshape(1, d), w1, w3, w2)


MOE_CAP = 160


def _moe_router_kernel(x_ref, g_ref, wr_ref, h_ref, rank_ref, gate_ref, rrow_ref, cnt_ref):
    tm = x_ref.shape[0]
    h = _rms_rows(x_ref[...], g_ref[...])
    h_ref[...] = h.astype(BF16)
    logits = _dot_hi(h, wr_ref[...])
    lane = lax.broadcasted_iota(jnp.int32, logits.shape, 1)
    logits = jnp.where(lane < N_EXPERTS, logits, -jnp.inf)
    v1 = jnp.max(logits, axis=-1, keepdims=True)
    i1 = jnp.min(jnp.where(logits == v1, lane, LANES), axis=-1, keepdims=True)
    rest = jnp.where(lane == i1, -jnp.inf, logits)
    v2 = jnp.max(rest, axis=-1, keepdims=True)
    i2 = jnp.min(jnp.where(rest == v2, lane, LANES), axis=-1, keepdims=True)
    e2 = jnp.exp(v2 - v1)
    gate_ref[...] = jnp.where(lane == i1, 1.0 / (1.0 + e2), 0.0) + jnp.where(lane == i2, e2 / (1.0 + e2), 0.0)
    routed = jnp.where((lane == i1) | (lane == i2), 1.0, 0.0)
    r_i = lax.broadcasted_iota(jnp.int32, (tm, tm), 0)
    c_i = lax.broadcasted_iota(jnp.int32, (tm, tm), 1)
    rank = _dot(jnp.where(c_i < r_i, 1.0, 0.0).astype(BF16), routed.astype(BF16))
    rank = jnp.where(routed > 0.0, rank, -1.0)
    rank_ref[...] = rank
    eye = jnp.where(lax.broadcasted_iota(jnp.int32, (LANES, LANES), 0)
                    == lax.broadcasted_iota(jnp.int32, (LANES, LANES), 1), 1.0, 0.0)
    rrow_ref[0] = _dot_nt_hi(eye, rank)[0:N_EXPERTS, :]
    cnt_ref[0] = jnp.broadcast_to(jnp.sum(routed, axis=0, keepdims=True), (8, LANES))


def _moe_expert_kernel(cnt_ref, y_ref, h_ref, rrow_ref, rank_ref, gate_ref, w1_ref, w3_ref, w2_ref, o_ref, acc_s):
    e = pl.program_id(0)
    i = pl.program_id(1)
    tm = h_ref.shape[0]
    cap = MOE_CAP
    lane = lax.broadcasted_iota(jnp.int32, (tm, LANES), 1)
    rank_col = jnp.sum(jnp.where(lane == e, rank_ref[...], 0.0), axis=-1, keepdims=True)
    gate_col = jnp.sum(jnp.where(lane == e, gate_ref[...], 0.0), axis=-1, keepdims=True)
    rank_row = rrow_ref[0, pl.ds(e, 1), :]
    slot_r = lax.broadcasted_iota(jnp.int32, (cap, tm), 0).astype(F32)
    slot_c = lax.broadcasted_iota(jnp.int32, (tm, cap), 1).astype(F32)
    acc_s[...] = jnp.zeros_like(acc_s)

    def one_pass(p, carry):
        base = (p * cap).astype(F32)
        gather = jnp.where(rank_row - base == slot_r, 1.0, 0.0).astype(BF16)
        xc = _dot(gather, h_ref[...]).astype(BF16)
        u = _silu(_dot(xc, w1_ref[0])) * _dot(xc, w3_ref[0])
        yc_hi, yc_lo = _split_bf16(_dot(u.astype(BF16), w2_ref[0]))
        scatter = jnp.where(rank_col - base == slot_c, 1.0, 0.0).astype(BF16)
        acc_s[...] += _dot(scatter, yc_hi) + _dot(scatter, yc_lo)
        return carry

    n_pass = (cnt_ref[i, e] + cap - 1) // cap
    lax.fori_loop(0, n_pass, one_pass, 0)
    o_ref[...] = y_ref[...] + gate_col * acc_s[...]


def _moe(x2, g, router, w1, w3, w2, tm):
    t, d = x2.shape
    n_e, _, f = w1.shape
    n_t = t // tm
    router_p = jnp.pad(router, ((0, 0), (0, LANES - n_e)))
    row = lambda w: pl.BlockSpec((tm, w), lambda i: (i, 0))
    h, rank, gate, rrow, cnt = pl.pallas_call(
        _moe_router_kernel,
        out_shape=[jax.ShapeDtypeStruct((t, d), BF16), jax.ShapeDtypeStruct((t, LANES), F32),
                   jax.ShapeDtypeStruct((t, LANES), F32), jax.ShapeDtypeStruct((n_t, N_EXPERTS, tm), F32),
                   jax.ShapeDtypeStruct((n_t, 8, LANES), F32)],
        grid=(n_t,),
        in_specs=[row(d), _full((1, d)), _full((d, LANES))],
        out_specs=[row(d), row(LANES), row(LANES), pl.BlockSpec((1, N_EXPERTS, tm), lambda i: (i, 0, 0)),
                   pl.BlockSpec((1, 8, LANES), lambda i: (i, 0, 0))],
        compiler_params=_cparams("arbitrary"),
        name="moe_router",
    )(x2, g.reshape(1, d), router_p)
    counts = cnt[:, 0, :n_e].astype(jnp.int32)
    tile = lambda w: pl.BlockSpec((tm, w), lambda e, i, c: (i, 0))
    weight = lambda r, c_: pl.BlockSpec((1, r, c_), lambda e, i, c: (e, 0, 0), pipeline_mode=pl.Buffered(1))
    return pl.pallas_call(
        _moe_expert_kernel,
        out_shape=jax.ShapeDtypeStruct((t, d), F32),
        grid_spec=pltpu.PrefetchScalarGridSpec(
            num_scalar_prefetch=1,
            grid=(n_e, n_t),
            in_specs=[tile(d), tile(d), pl.BlockSpec((1, N_EXPERTS, tm), lambda e, i, c: (i, 0, 0)),
                      tile(LANES), tile(LANES), weight(d, f), weight(d, f), weight(f, d)],
            out_specs=tile(d),
            scratch_shapes=[pltpu.VMEM((tm, d), F32)]),
        input_output_aliases={1: 0},
        compiler_params=_cparams("arbitrary", "arbitrary"),
        name="moe_experts",
    )(counts, x2, h, rrow, rank, gate, w1, w3, w2)


def _final_norm_kernel(x_ref, g_ref, o_ref):
    o_ref[...] = _rms_rows(x_ref[...], g_ref[...])


def _final_norm(x2, g, tm):
    t, d = x2.shape
    row = pl.BlockSpec((tm, d), lambda i: (i, 0))
    return pl.pallas_call(
        _final_norm_kernel,
        out_shape=jax.ShapeDtypeStruct((t, d), F32),
        grid=(t // tm,),
        in_specs=[row, _full((1, d))],
        out_specs=row,
        compiler_params=_cparams("arbitrary"),
        name="final_norm",
    )(x2, g.reshape(1, d))


def kernel(x, norm_mix_g, w_in, dsa_g_cq, dsa_g_ckv, dsa_g_kidx, dsa_w_uq, dsa_w_qidx, dsa_w_uv, rwkv_mu, rwkv_w0, rwkv_w_up, rwkv_a0, rwkv_a_up, rwkv_g_up, rwkv_k_k, rwkv_k_a, rwkv_r_k, rwkv_ln_g, rwkv_ln_b, mlstm_conv_w, mlstm_conv_b, mlstm_b_i, mlstm_b_f, mlstm_norm_g, fox_b_f, w_branch, w_out, norm_ffn_g, ffn_w1, ffn_w3, ffn_w2, moe_router, moe_w1, moe_w3, moe_w2, final_norm_g):
    b, s, d = x.shape
    depth = w_in.shape[0]
    t = b * s
    tm = min(512, s)
    for l in range(depth):
        wts = _inproj_weights(w_in[l])
        (zr, zm, fq, fk, fv, ckv, ki, fcol, cqt, ckvt, wit, frow, mkt, mvt, mgt) = _inproj(
            x, norm_mix_g[l], wts, dsa_g_cq[l], dsa_g_ckv[l], dsa_g_kidx[l], fox_b_f[l], tm)
        y_fox = _fox(fq, fk, fv, fcol, frow, min(1024, s))
        y_dsa = _dsa(cqt, wit, ki, ckv, ckvt, dsa_w_uq[l], dsa_w_qidx[l], dsa_w_uv[l], min(256, s // 2)
                     ).reshape(t, BRANCH_WIDTH)
        y_rwkv = _rwkv(zr, rwkv_mu[l], rwkv_w0[l], rwkv_w_up[l], rwkv_a0[l], rwkv_a_up[l], rwkv_g_up[l], rwkv_k_k[l],
                       rwkv_k_a[l], rwkv_r_k[l], rwkv_ln_g[l], rwkv_ln_b[l], min(16, b), min(4, b), 32
                       ).reshape(t, BRANCH_WIDTH)
        y_mlstm = _mlstm(zm, mkt, mvt, mgt, mlstm_conv_w[l], mlstm_conv_b[l], mlstm_b_i[l], mlstm_b_f[l],
                         mlstm_norm_g[l]).reshape(t, BRANCH_WIDTH)
        ys = (y_dsa, y_rwkv, y_mlstm, y_fox.reshape(t, BRANCH_WIDTH))
        x2 = _merge(x.reshape(t, d), norm_mix_g[l], ys, wts['w_gate'], w_branch[l].astype(BF16),
                    w_out[l].astype(BF16), tm)
        j = l // 2
        if l % 2 == 0:
            x2 = _ffn(x2, norm_ffn_g[l], ffn_w1[j].astype(BF16), ffn_w3[j].astype(BF16), ffn_w2[j].astype(BF16), tm)
        else:
            x2 = _moe(x2, norm_ffn_g[l], moe_router[j], moe_w1[j].astype(BF16), moe_w3[j].astype(BF16),
                      moe_w2[j].astype(BF16), tm)
        x = x2.reshape(b, s, d)
    return _final_norm(x.reshape(t, d), final_norm_g, tm).reshape(b, s, d)
```

```python
import functools

import jax
import jax.numpy as jnp
from jax import lax
from jax.experimental import pallas as pl
from jax.experimental.pallas import tpu as pltpu

F32 = jnp.float32
BF16 = jnp.bfloat16
HIGHEST = lax.Precision.HIGHEST

EPS = 1e-6
NEG_BIG = -1e30

N_BRANCH = 4
BRANCH_WIDTH = 256
DSA_HEADS = 4
DSA_HEAD_DIM = 64
DSA_Q_LATENT = 128
DSA_KV_LATENT = 128
IDX_HEADS = 8
IDX_DIM = 32
TOPK_MAX = 256
DSA_Q_BLOCK = 128

RWKV_HEADS = 4
RWKV_HEAD_DIM = 64
RWKV_WIDTH = RWKV_HEADS * RWKV_HEAD_DIM
RWKV_LORA_W = 64
RWKV_LORA_A = 64
RWKV_LORA_G = 128
RWKV_GN_EPS = 64e-5
RWKV_IN = 3 * RWKV_WIDTH + RWKV_LORA_W + RWKV_LORA_A + RWKV_LORA_G

MLSTM_HEADS = 4
MLSTM_QK_DIM = 32
MLSTM_V_DIM = 64
MLSTM_CHUNK = 256
CONV_WIDTH = 4
GATE_SOFTCAP = 15.0
MLSTM_QK = MLSTM_HEADS * MLSTM_QK_DIM
MLSTM_V = MLSTM_HEADS * MLSTM_V_DIM

FOX_HEADS = 4
FOX_HEAD_DIM = 64
FOX_WIDTH = FOX_HEADS * FOX_HEAD_DIM

N_EXPERTS = 8

VMEM_LIMIT_BYTES = 56 * 1024 * 1024
LANES = 128


def _cparams(*sem):
    return pltpu.CompilerParams(dimension_semantics=sem, vmem_limit_bytes=VMEM_LIMIT_BYTES)


def _dot(a, b):
    return jnp.dot(a, b, preferred_element_type=F32)


def _dot_hi(a, b):
    return jnp.dot(a, b, preferred_element_type=F32, precision=HIGHEST)


def _dot_nt(a, b):
    return lax.dot_general(a, b, (((1,), (1,)), ((), ())), preferred_element_type=F32)


def _dot_nt_hi(a, b):
    return lax.dot_general(a, b, (((1,), (1,)), ((), ())), preferred_element_type=F32, precision=HIGHEST)


def _log_sigmoid(t):
    return jnp.minimum(t, 0.0) - jnp.log1p(jnp.exp(-jnp.abs(t)))


def _sigmoid(t):
    return 1.0 / (1.0 + jnp.exp(-t))


def _silu(t):
    return t * _sigmoid(t)


def _rms_rows(t, g_row):
    return t * lax.rsqrt(jnp.mean(t * t, axis=-1, keepdims=True) + EPS) * g_row


def _rms_cols(t, g_col):
    return t * lax.rsqrt(jnp.mean(t * t, axis=0, keepdims=True) + EPS) * g_col


def _full(shape):
    n = len(shape)
    return pl.BlockSpec(shape, lambda *_: (0,) * n)


T_CQ = 0
T_CKV = T_CQ + DSA_Q_LATENT
T_WIDX = T_CKV + DSA_KV_LATENT
T_FOXF = T_WIDX + IDX_HEADS
T_MK = T_FOXF + 8
T_MV = T_MK + MLSTM_QK
T_MG = T_MV + MLSTM_V
T_ROWS = T_MG + 8
D_COLS = 3 * LANES
M_COLS = 2 * MLSTM_QK + 2 * MLSTM_V + LANES


def _inproj_kernel(x_ref, g_ref, wr_ref, wm_ref, wf_ref, wd_ref, wt_ref,
                   gcq_ref, gckv_col_ref, gckv_row_ref, gki_ref, bf_row_ref, bf_col_ref,
                   zr_ref, zm_ref, q_ref, k_ref, v_ref, ckv_ref, ki_ref, fcol_ref,
                   cqt_ref, ckvt_ref, wit_ref, frow_ref, mkt_ref, mvt_ref, mgt_ref,
                   carry_row, carry_col):
    j = pl.program_id(1)
    tm = x_ref.shape[1]

    @pl.when(j == 0)
    def _():
        carry_row[...] = jnp.zeros_like(carry_row)
        carry_col[...] = jnp.zeros_like(carry_col)

    x = x_ref[0]
    h = _rms_rows(x, g_ref[...]).astype(BF16)

    zr_ref[0] = _dot_nt(h, wr_ref[...])
    zm_ref[0] = _dot_nt(h, wm_ref[...])

    zf = _dot_nt(h, wf_ref[...])
    for hd in range(FOX_HEADS):
        lo = hd * FOX_HEAD_DIM
        q_ref[0, hd] = (zf[:, lo:lo + FOX_HEAD_DIM] * FOX_HEAD_DIM ** -0.5).astype(BF16)
        k_ref[0, hd] = zf[:, FOX_WIDTH + lo:FOX_WIDTH + lo + FOX_HEAD_DIM].astype(BF16)
        v_ref[0, hd] = zf[:, 2 * FOX_WIDTH + lo:2 * FOX_WIDTH + lo + FOX_HEAD_DIM].astype(BF16)

    zd = _dot_nt(h, wd_ref[...])
    ckv_ref[0] = _rms_rows(zd[:, :LANES], gckv_row_ref[...]).astype(BF16)
    ki_pieces = [p.astype(F32) for p in _split3_bf16(_rms_rows(zd[:, LANES:LANES + IDX_DIM], gki_ref[...]))]
    ki_pad = jnp.zeros((tm, IDX_K_COLS - len(IDX_SPLIT_K) * IDX_DIM), F32)
    ki_ref[0] = jnp.concatenate([ki_pieces[kp] for kp, _ in IDX_SPLIT_K] + [ki_pad], axis=1).astype(BF16)
    lf_col = _log_sigmoid(zd[:, 2 * LANES:] + bf_row_ref[...])
    r_i = lax.broadcasted_iota(jnp.int32, (tm, tm), 0)
    c_i = lax.broadcasted_iota(jnp.int32, (tm, tm), 1)
    tri = jnp.where(c_i <= r_i, 1.0, 0.0).astype(BF16)
    cum_col = sum(_dot(tri, p) for p in _split3_bf16(lf_col)) + carry_row[...]
    fcol_ref[0] = cum_col
    carry_row[...] = cum_col[tm - 1:tm, :]

    zt = _dot_nt(wt_ref[...], h)
    cqt_ref[0] = _rms_cols(zt[T_CQ:T_CKV], gcq_ref[...])
    ckvt_ref[0] = _rms_cols(zt[T_CKV:T_WIDX], gckv_col_ref[...]).astype(BF16)
    wit_ref[0] = zt[T_WIDX:T_FOXF] * IDX_HEADS ** -0.5
    lf_row = _log_sigmoid(zt[T_FOXF:T_MK] + bf_col_ref[...])
    cum_row = sum(_dot_nt(p, tri) for p in _split3_bf16(lf_row)) + carry_col[:, 0:1]
    frow_ref[0] = cum_row
    carry_col[...] = jnp.broadcast_to(cum_row[:, tm - 1:tm], carry_col.shape)
    mkt_ref[0] = zt[T_MK:T_MV]
    mvt_ref[0] = zt[T_MV:T_MG]
    mgt_ref[0] = zt[T_MG:T_ROWS]


def _inproj_weights(w_in_l):
    wt = w_in_l.T
    o = 0
    w_cq = wt[o:o + DSA_Q_LATENT]; o += DSA_Q_LATENT
    w_ckv = wt[o:o + DSA_KV_LATENT]; o += DSA_KV_LATENT
    w_kidx = wt[o:o + IDX_DIM]; o += IDX_DIM
    w_widx = wt[o:o + IDX_HEADS]; o += IDX_HEADS
    w_rwkv = wt[o:o + RWKV_IN]; o += RWKV_IN
    w_mq = wt[o:o + MLSTM_QK]; o += MLSTM_QK
    w_mk = wt[o:o + MLSTM_QK]; o += MLSTM_QK
    w_mv = wt[o:o + MLSTM_V]; o += MLSTM_V
    w_mo = wt[o:o + MLSTM_V]; o += MLSTM_V
    w_mi = wt[o:o + MLSTM_HEADS]; o += MLSTM_HEADS
    w_mf = wt[o:o + MLSTM_HEADS]; o += MLSTM_HEADS
    w_fox = wt[o:o + 3 * FOX_WIDTH]; o += 3 * FOX_WIDTH
    w_ff = wt[o:o + FOX_HEADS]; o += FOX_HEADS
    w_gate = wt[o:]

    def padr(w, n):
        return jnp.pad(w, ((0, n - w.shape[0]), (0, 0)))

    w_d = jnp.concatenate([w_ckv, padr(w_kidx, LANES), padr(w_ff, LANES)], axis=0)
    w_m = jnp.concatenate([w_mq, w_mk, w_mv, w_mo, padr(jnp.concatenate([w_mi, w_mf], axis=0), LANES)], axis=0)
    w_t = jnp.concatenate([w_cq, w_ckv, w_widx, padr(w_ff, 8), w_mk, w_mv, w_mi, w_mf], axis=0)
    assert w_t.shape[0] == T_ROWS
    cast = lambda w: w.astype(BF16)
    return dict(w_r=cast(w_rwkv), w_m=cast(w_m), w_f=cast(w_fox), w_d=cast(w_d), w_t=cast(w_t), w_gate=cast(w_gate))


def _inproj(x, g, wts, g_cq, g_ckv, g_kidx, fox_b_f, tm):
    b, s, d = x.shape
    nj = s // tm
    bf_row = jnp.pad(fox_b_f, (0, LANES - FOX_HEADS)).reshape(1, LANES)
    bf_col = jnp.pad(fox_b_f, (0, 8 - FOX_HEADS)).reshape(8, 1)
    row = lambda w: pl.BlockSpec((1, tm, w), lambda i, j: (i, j, 0))
    head = pl.BlockSpec((1, FOX_HEADS, tm, FOX_HEAD_DIM), lambda i, j: (i, 0, j, 0))
    col = lambda r: pl.BlockSpec((1, r, tm), lambda i, j: (i, 0, j))
    out_shape = [
        jax.ShapeDtypeStruct((b, s, RWKV_IN), F32),
        jax.ShapeDtypeStruct((b, s, M_COLS), F32),
        jax.ShapeDtypeStruct((b, FOX_HEADS, s, FOX_HEAD_DIM), BF16),
        jax.ShapeDtypeStruct((b, FOX_HEADS, s, FOX_HEAD_DIM), BF16),
        jax.ShapeDtypeStruct((b, FOX_HEADS, s, FOX_HEAD_DIM), BF16),
        jax.ShapeDtypeStruct((b, s, DSA_KV_LATENT), BF16),
        jax.ShapeDtypeStruct((b, s, IDX_K_COLS), BF16),
        jax.ShapeDtypeStruct((b, s, LANES), F32),
        jax.ShapeDtypeStruct((b, DSA_Q_LATENT, s), F32),
        jax.ShapeDtypeStruct((b, DSA_KV_LATENT, s), BF16),
        jax.ShapeDtypeStruct((b, IDX_HEADS, s), F32),
        jax.ShapeDtypeStruct((b, 8, s), F32),
        jax.ShapeDtypeStruct((b, MLSTM_QK, s), F32),
        jax.ShapeDtypeStruct((b, MLSTM_V, s), F32),
        jax.ShapeDtypeStruct((b, 8, s), F32),
    ]
    out_specs = [row(RWKV_IN), row(M_COLS), head, head, head, row(DSA_KV_LATENT), row(IDX_K_COLS), row(LANES),
                 col(DSA_Q_LATENT), col(DSA_KV_LATENT), col(IDX_HEADS), col(8), col(MLSTM_QK), col(MLSTM_V), col(8)]
    in_specs = [row(d), _full((1, d)), _full(wts['w_r'].shape), _full(wts['w_m'].shape), _full(wts['w_f'].shape),
                _full(wts['w_d'].shape), _full(wts['w_t'].shape),
                _full((DSA_Q_LATENT, 1)), _full((DSA_KV_LATENT, 1)), _full((1, DSA_KV_LATENT)), _full((1, IDX_DIM)),
                _full((1, LANES)), _full((8, 1))]
    return pl.pallas_call(
        _inproj_kernel,
        out_shape=out_shape,
        grid=(b, nj),
        in_specs=in_specs,
        out_specs=out_specs,
        scratch_shapes=[pltpu.VMEM((1, LANES), F32), pltpu.VMEM((8, LANES), F32)],
        compiler_params=_cparams("arbitrary", "arbitrary"),
        name="inproj",
    )(x, g.reshape(1, d), wts['w_r'], wts['w_m'], wts['w_f'], wts['w_d'], wts['w_t'],
      g_cq.reshape(-1, 1), g_ckv.reshape(-1, 1), g_ckv.reshape(1, -1), g_kidx.reshape(1, -1), bf_row, bf_col)


def _fox_kernel(q_ref, k_ref, v_ref, fcol_ref, frow_ref, o_ref):
    qi = pl.program_id(1)
    tq = q_ref.shape[2]
    outs = []
    for hd in range(FOX_HEADS):
        q = q_ref[0, hd]
        fq = fcol_ref[0][:, hd:hd + 1]

        def scores(j):
            start = pl.multiple_of(j * tq, tq)
            k = k_ref[0, hd, pl.ds(start, tq), :]
            v = v_ref[0, hd, pl.ds(start, tq), :]
            fk = frow_ref[0, hd:hd + 1, pl.ds(start, tq)]
            return _dot_nt(q, k) + (fq - fk), v

        def update(carry, sc, v):
            m, l, acc = carry
            m_new = jnp.maximum(m, jnp.max(sc, axis=-1, keepdims=True))
            alpha = jnp.exp(m - m_new)
            p = jnp.exp(sc - m_new)
            l = alpha * l + jnp.sum(p, axis=-1, keepdims=True)
            acc = alpha * acc + _dot(p.astype(BF16), v)
            return m_new, l, acc

        def body(j, carry):
            sc, v = scores(j)
            return update(carry, sc, v)

        init = (jnp.full((tq, 1), NEG_BIG, F32), jnp.zeros((tq, 1), F32), jnp.zeros((tq, FOX_HEAD_DIM), F32))
        carry = lax.fori_loop(0, qi, body, init)
        sc, v = scores(qi)
        r_i = lax.broadcasted_iota(jnp.int32, (tq, tq), 0)
        c_i = lax.broadcasted_iota(jnp.int32, (tq, tq), 1)
        sc = jnp.where(c_i <= r_i, sc, NEG_BIG)
        m, l, acc = update(carry, sc, v)
        outs.append(acc / l)
    o_ref[0] = jnp.concatenate(outs, axis=-1).astype(o_ref.dtype)


def _fox(q, k, v, fcol, frow, tq):
    b, nh, s, d = q.shape
    return pl.pallas_call(
        _fox_kernel,
        out_shape=jax.ShapeDtypeStruct((b, s, nh * d), BF16),
        grid=(b, s // tq),
        in_specs=[
            pl.BlockSpec((1, nh, tq, d), lambda i, j: (i, 0, j, 0)),
            pl.BlockSpec((1, nh, s, d), lambda i, j: (i, 0, 0, 0)),
            pl.BlockSpec((1, nh, s, d), lambda i, j: (i, 0, 0, 0)),
            pl.BlockSpec((1, tq, LANES), lambda i, j: (i, j, 0)),
            pl.BlockSpec((1, 8, s), lambda i, j: (i, 0, 0)),
        ],
        out_specs=pl.BlockSpec((1, tq, nh * d), lambda i, j: (i, j, 0)),
        compiler_params=_cparams("arbitrary", "arbitrary"),
        name="fox",
    )(q, k, v, fcol, frow)


INT_MIN = -2 ** 31


IDX_SPLIT_K = ((0, 0), (0, 1), (1, 0), (0, 2), (1, 1), (2, 0))
IDX_K_COLS = 2 * LANES


def _split3_bf16(t):
    p0 = t.astype(BF16)
    r1 = t - p0.astype(F32)
    p1 = r1.astype(BF16)
    p2 = (r1 - p1.astype(F32)).astype(BF16)
    return p0, p1, p2


def _tree_sum(parts):
    while len(parts) > 1:
        parts = [parts[i] + parts[i + 1] for i in range(0, len(parts) - 1, 2)] + (parts[-1:] if len(parts) % 2 else [])
    return parts[0]


def _dsa_kernel(cqt_ref, wit_ref, ki6_ref, ckv_ref, ckvt_ref, wuqt_ref, wqit_ref, wuvt_ref, o_ref, key_s, *, ck):
    qb = pl.program_id(1)
    s = ki6_ref.shape[1]
    nq = cqt_ref.shape[2]
    topk = min(TOPK_MAX, s // 4)
    n_ck = (qb * nq + nq + ck - 1) // ck
    lat = DSA_KV_LATENT

    cq = cqt_ref[0]
    qi_t = _dot_hi(wqit_ref[...], cq)
    zeros_pad = jnp.zeros((IDX_K_COLS - len(IDX_SPLIT_K) * IDX_DIM, nq), BF16)
    q_blocks = []
    for hd in range(IDX_HEADS):
        pieces = _split3_bf16(qi_t[hd * IDX_DIM:(hd + 1) * IDX_DIM, :])
        q_blocks.append(jnp.concatenate([pieces[qp] for _, qp in IDX_SPLIT_K] + [zeros_pad], axis=0))
    q6 = jnp.concatenate(q_blocks, axis=1)
    wi = wit_ref[0] * IDX_DIM ** -0.5
    tpos = qb * nq + lax.broadcasted_iota(jnp.int32, (ck, nq), 1)
    row = lax.broadcasted_iota(jnp.int32, (ck, nq), 0)

    def chunk_start(c):
        return pl.multiple_of(c * ck, ck)

    def score_chunk(c, carry):
        c0 = chunk_start(c)
        ki6 = ki6_ref[0, pl.ds(c0, ck), :]
        score = None
        for hp in range(0, IDX_HEADS, 2):
            dots = _dot(ki6, q6[:, hp * nq:(hp + 2) * nq])
            for i in range(2):
                term = wi[hp + i:hp + i + 1, :] * jnp.maximum(dots[:, i * nq:(i + 1) * nq], 0.0)
                score = term if score is None else score + term
        bits = pltpu.bitcast(score, jnp.int32)
        key = jnp.where(bits < 0, bits ^ 0x7FFFFFFF, bits)
        key = jnp.where(score == 0.0, 0, key)
        key_s[pl.ds(c0, ck), :] = jnp.where(c0 + row <= tpos, key, INT_MIN)
        return carry

    n_pair = (n_ck + 1) // 2
    lax.fori_loop(0, n_pair, lambda j, carry: score_chunk(2 * j + 1, score_chunk(2 * j, carry)), 0)

    def count(mask_fn):
        def body(j, acc):
            parts = []
            for c in (2 * j, 2 * j + 1):
                c0 = chunk_start(c)
                ones = jnp.where(mask_fn(key_s[pl.ds(c0, ck), :], c0 + row), 1, 0)
                parts += [ones[i * 8:(i + 1) * 8] for i in range(ck // 8)]
            return acc + _tree_sum(parts)
        acc = lax.fori_loop(0, n_pair, body, jnp.zeros((8, nq), jnp.int32))
        return jnp.sum(acc, axis=0, keepdims=True)

    def value_bit(i, carry):
        lo, n_lo = carry
        cand = lo + jnp.left_shift(jnp.int32(1), 31 - i)
        n_cand = count(lambda key, pos: key >= cand)
        ok = n_cand >= topk
        return jnp.where(ok, cand, lo), jnp.where(ok, n_cand, n_lo)

    thr, n_ge = lax.fori_loop(0, 32, value_bit, (jnp.full((1, nq), INT_MIN, jnp.int32),
                                                 jnp.full((1, nq), s, jnp.int32)))

    n_bits = s.bit_length()

    def tie_search():
        need = topk - count(lambda key, pos: key > thr)

        def index_bit(i, lo):
            cand = lo + jnp.left_shift(jnp.int32(1), n_bits - 1 - i)
            return jnp.where(count(lambda key, pos: (key == thr) & (pos < cand)) < need, cand, lo)

        return lax.fori_loop(0, n_bits, index_bit, jnp.zeros((1, nq), jnp.int32))

    surplus = jnp.max(jnp.where((n_ge > topk) & (thr > INT_MIN), 1, 0))
    last = lax.cond(surplus > 0, tie_search, lambda: jnp.full((1, nq), s, jnp.int32))

    q_t = (_dot(wuqt_ref[...], cq.astype(BF16)) * lat ** -0.5).astype(BF16)

    def attend(c, carry):
        c0 = chunk_start(c)
        key = key_s[pl.ds(c0, ck), :]
        pos = c0 + row
        sel = ((key > thr) | ((key == thr) & (pos <= last))) & (pos <= tpos)
        bias = jnp.where(sel, 0.0, NEG_BIG)
        ckv = ckv_ref[0, pl.ds(c0, ck), :]
        ckvt = ckvt_ref[0, :, pl.ds(c0, ck)]
        new = []
        for hd in range(DSA_HEADS):
            m, l, acc = carry[hd]
            lg = _dot(ckv, q_t[hd * lat:(hd + 1) * lat, :]) + bias
            m_new = jnp.maximum(m, jnp.max(lg, axis=0, keepdims=True))
            alpha = jnp.exp(m - m_new)
            p = jnp.exp(lg - m_new)
            new.append((m_new, alpha * l + jnp.sum(p, axis=0, keepdims=True),
                        alpha * acc + _dot(ckvt, p.astype(BF16))))
        return tuple(new)

    init = tuple((jnp.full((1, nq), NEG_BIG, F32), jnp.zeros((1, nq), F32), jnp.zeros((lat, nq), F32))
                 for _ in range(DSA_HEADS))
    final = lax.fori_loop(0, n_pair, lambda j, carry: attend(2 * j + 1, attend(2 * j, carry)), init)
    outs = [_dot(wuvt_ref[hd], (acc / l).astype(BF16)) for hd, (_, l, acc) in enumerate(final)]
    o_ref[0] = jnp.concatenate(outs, axis=0).T.astype(o_ref.dtype)


def _dsa(cqt, wit, ki6, ckv, ckvt, w_uq, w_qidx, w_uv, ck):
    b, c, s = cqt.shape
    nq = DSA_Q_BLOCK
    wuqt = w_uq.reshape(c, -1).T.astype(BF16)
    wqit = w_qidx.reshape(c, -1).T
    wuvt = jnp.transpose(w_uv, (0, 2, 1)).astype(BF16)
    return pl.pallas_call(
        functools.partial(_dsa_kernel, ck=ck),
        out_shape=jax.ShapeDtypeStruct((b, s, DSA_HEADS * DSA_HEAD_DIM), BF16),
        grid=(b, s // nq),
        in_specs=[pl.BlockSpec((1, c, nq), lambda i, j: (i, 0, j)),
                  pl.BlockSpec((1, IDX_HEADS, nq), lambda i, j: (i, 0, j)),
                  pl.BlockSpec((1, s, IDX_K_COLS), lambda i, j: (i, 0, 0)),
                  pl.BlockSpec((1, s, DSA_KV_LATENT), lambda i, j: (i, 0, 0)),
                  pl.BlockSpec((1, DSA_KV_LATENT, s), lambda i, j: (i, 0, 0)),
                  _full(wuqt.shape), _full(wqit.shape), _full(wuvt.shape)],
        out_specs=pl.BlockSpec((1, nq, DSA_HEADS * DSA_HEAD_DIM), lambda i, j: (i, j, 0)),
        scratch_shapes=[pltpu.VMEM((s, nq), jnp.int32)],
        compiler_params=_cparams("arbitrary", "arbitrary"),
        name="dsa",
    )(cqt, wit, ki6, ckv, ckvt, wuqt, wqit, wuvt)


Y_PAD = 8


def _head_ones(n, dtype):
    r_i = lax.broadcasted_iota(jnp.int32, (n, n), 0) // RWKV_HEAD_DIM
    c_i = lax.broadcasted_iota(jnp.int32, (n, n), 1) // RWKV_HEAD_DIM
    return jnp.where(r_i == c_i, 1.0, 0.0).astype(dtype)


def _split_bf16(t):
    hi = t.astype(BF16)
    lo = (t - hi.astype(F32)).astype(BF16)
    return hi, lo


def _rwkv_kernel(z_ref, mu_ref, w0_ref, wup_ref, a0_ref, aup_ref, gup_ref, kk_ref, ka_ref, rk_ref, lng_ref, lnb_ref,
                 o_ref, st_s, prev_s, step_s, y_s, bonus_s, gate_s, *, grp):
    c = pl.program_id(1)
    n_g, tc, _ = z_ref.shape
    hd, wd_ = RWKV_HEAD_DIM, RWKV_WIDTH

    @pl.when(c == 0)
    def _():
        st_s[...] = jnp.zeros_like(st_s)
        prev_s[...] = jnp.zeros_like(prev_s)

    ones_b = _head_ones(wd_, BF16)
    diag = jnp.where(lax.broadcasted_iota(jnp.int32, (hd, wd_), 0)
                     == lax.broadcasted_iota(jnp.int32, (hd, wd_), 1) % hd, 1.0, 0.0).astype(F32)

    def head_sum(t):
        hi, lo = _split_bf16(t)
        return _dot(hi, ones_b) + _dot(lo, ones_b)

    zs = []
    for g in range(n_g):
        z = z_ref[g]
        row = lax.broadcasted_iota(jnp.int32, z.shape, 0)
        z_prev = jnp.where(row == 0, prev_s[g], pltpu.roll(z, 1, 0))
        prev_s[g] = z[tc - 1:tc, :]
        zs.append(z + mu_ref[...] * (z_prev - z))
    z = jnp.concatenate(zs, axis=0)
    r = z[:, 0:wd_]
    k = z[:, wd_:2 * wd_]
    v = z[:, 2 * wd_:3 * wd_]
    o = 3 * wd_
    w_lora = z[:, o:o + RWKV_LORA_W]
    a_lora = z[:, o + RWKV_LORA_W:o + RWKV_LORA_W + RWKV_LORA_A]
    g_lora = z[:, o + RWKV_LORA_W + RWKV_LORA_A:]
    w_log = _log_sigmoid(w0_ref[...] + _dot_hi(jnp.tanh(w_lora), wup_ref[...])) - 0.5
    a = _sigmoid(a0_ref[...] + _dot_hi(a_lora, aup_ref[...]))
    gate_s[...] = _dot_hi(_sigmoid(g_lora), gup_ref[...])
    kk = k * kk_ref[...]
    kk = kk / jnp.maximum(jnp.sqrt(head_sum(kk * kk)), 1e-12)
    k = k * (1.0 + (a - 1.0) * ka_ref[...])
    step_s[0] = jnp.exp(-jnp.exp(w_log))
    step_s[1] = -kk
    step_s[2] = kk * a
    step_s[3] = k
    step_s[4] = r
    step_s[5] = v
    bonus_s[...] = head_sum(r * k * rk_ref[...]) * v

    n_grp = n_g // grp

    def rows(kind, g0, t):
        return jnp.concatenate(
            [jnp.broadcast_to(step_s[kind, pl.ds((g0 + i) * tc + t, 1), :], (hd, wd_)) for i in range(grp)], axis=0)

    def store_y(yb, g0, t_write):
        yb = yb * diag_g
        for i in range(grp):
            tile = jnp.sum(yb[i * hd:(i + 1) * hd].reshape(hd // 8, 8, wd_), axis=0)
            y_s[pl.ds(Y_PAD + (g0 + i) * tc + t_write, 1), :] = jnp.sum(tile, axis=0, keepdims=True)

    def rows_b(kind, g0, t):
        tiles = []
        for i in range(grp):
            r16 = jnp.broadcast_to(step_s[kind, pl.ds((g0 + i) * tc + t, 1), :], (16, wd_)).astype(BF16)
            tiles.append(jnp.broadcast_to(r16[None], (hd // 16, 16, wd_)).reshape(hd, wd_))
        return jnp.concatenate(tiles, axis=0)

    def step(t, carry):
        for q in range(n_grp):
            g0 = q * grp
            st = st_s[q]
            st_b = st.astype(BF16)
            sa = _dot(st_b * rows_b(1, g0, t), ones_b)
            store_y(_dot(st_b * rows_b(4, g0, jnp.maximum(t - 1, 0)), ones_b), g0, t - 1)
            vb = _dot(rows_b(5, g0, t) * diag_b, ones_b)
            st_s[q] = st * rows(0, g0, t) + sa * rows(2, g0, t) + vb * rows(3, g0, t)
        return carry

    diag_g = jnp.concatenate([diag] * grp, axis=0)
    diag_b = diag_g.astype(BF16)
    lax.fori_loop(0, tc, step, 0)
    for q in range(n_grp):
        store_y(_dot(st_s[q].astype(BF16) * rows_b(4, q * grp, tc - 1), ones_b), q * grp, tc - 1)

    y = y_s[pl.ds(Y_PAD, n_g * tc), :]
    mean = head_sum(y) * (1.0 / hd)
    yc = y - mean
    var = head_sum(yc * yc) * (1.0 / hd)
    yn = yc * lax.rsqrt(var + RWKV_GN_EPS) * lng_ref[...] + lnb_ref[...]
    out = ((yn + bonus_s[...]) * gate_s[...]).astype(o_ref.dtype)
    for g in range(n_g):
        o_ref[g] = out[g * tc:(g + 1) * tc]


def _rwkv(zr, mu, w0, w_up, a0, a_up, g_up, k_k, k_a, r_k, ln_g, ln_b, n_g, grp, tc):
    b, s, zin = zr.shape
    wd_ = RWKV_WIDTH
    vec = lambda p: p.reshape(1, -1)
    params = [vec(mu), vec(w0), w_up, vec(a0), a_up, g_up, vec(k_k), vec(k_a), vec(r_k), vec(ln_g), vec(ln_b)]
    return pl.pallas_call(
        functools.partial(_rwkv_kernel, grp=grp),
        out_shape=jax.ShapeDtypeStruct((b, s, wd_), BF16),
        grid=(b // n_g, s // tc),
        in_specs=[pl.BlockSpec((n_g, tc, zin), lambda i, c: (i, c, 0))] + [_full(p.shape) for p in params],
        out_specs=pl.BlockSpec((n_g, tc, wd_), lambda i, c: (i, c, 0)),
        scratch_shapes=[pltpu.VMEM((n_g // grp, grp * RWKV_HEAD_DIM, wd_), F32), pltpu.VMEM((n_g, 1, zin), F32),
                        pltpu.VMEM((6, n_g * tc, wd_), F32), pltpu.VMEM((Y_PAD + n_g * tc, wd_), F32),
                        pltpu.VMEM((n_g * tc, wd_), F32), pltpu.VMEM((n_g * tc, wd_), F32)],
        compiler_params=_cparams("arbitrary", "arbitrary"),
        name="rwkv",
    )(zr, *params)


def _softcap(t):
    return GATE_SOFTCAP * jnp.tanh(t / GATE_SOFTCAP)


def _mlstm_kernel(zm_ref, mkt_ref, mvt_ref, mgt_ref, cw_row_ref, cb_row_ref, cw_col_ref, cb_col_ref,
                  bg_row_ref, bg_col_ref, ng_ref, o_ref, q_s, k_s, kt_s, gc_s, gr_s, yt_s):
    s = zm_ref.shape[1]
    nh, dk, dv, lc = MLSTM_HEADS, MLSTM_QK_DIM, MLSTM_V_DIM, MLSTM_CHUNK
    pair = 2 * lc

    qk = zm_ref[0, :, 0:2 * MLSTM_QK]
    pos_r = lax.broadcasted_iota(jnp.int32, qk.shape, 0)
    acc = cb_row_ref[...] + qk * cw_row_ref[CONV_WIDTH - 1:CONV_WIDTH, :]
    for r in range(1, CONV_WIDTH):
        sh = jnp.where(pos_r >= r, pltpu.roll(qk, r, 0), 0.0)
        acc = acc + sh * cw_row_ref[CONV_WIDTH - 1 - r:CONV_WIDTH - r, :]
    acc = _silu(acc)
    q_s[...] = acc[:, :MLSTM_QK] * dk ** -0.5
    k_s[...] = acc[:, MLSTM_QK:]
    kt = mkt_ref[0]
    pos_c = lax.broadcasted_iota(jnp.int32, kt.shape, 1)
    acc_t = cb_col_ref[...] + kt * cw_col_ref[:, CONV_WIDTH - 1:CONV_WIDTH]
    for r in range(1, CONV_WIDTH):
        sh = jnp.where(pos_c >= r, pltpu.roll(kt, r, 1), 0.0)
        acc_t = acc_t + sh * cw_col_ref[:, CONV_WIDTH - 1 - r:CONV_WIDTH - r]
    kt_s[...] = _silu(acc_t)

    gcol = _softcap(zm_ref[0, :, 2 * MLSTM_QK + 2 * MLSTM_V:] + bg_row_ref[...])
    lane = lax.broadcasted_iota(jnp.int32, gcol.shape, 1)
    gc_s[...] = jnp.where(lane < nh, gcol, _log_sigmoid(gcol))
    grow = _softcap(mgt_ref[0] + bg_col_ref[...])
    sub = lax.broadcasted_iota(jnp.int32, grow.shape, 0)
    gr_s[...] = jnp.where(sub < nh, grow, _log_sigmoid(grow))

    r_i = lax.broadcasted_iota(jnp.int32, (lc, lc), 0)
    c_i = lax.broadcasted_iota(jnp.int32, (lc, lc), 1)
    tri = jnp.where(c_i <= r_i, 1.0, 0.0).astype(F32)
    causal_t = r_i <= c_i
    ones_rows = jnp.ones((8, lc), F32)

    def chunk_pair(p, carry):
        base = pl.multiple_of(p * pair, pair)
        gr_slab = gr_s[:, pl.ds(base, pair)]
        kt_slab = kt_s[:, pl.ds(base, pair)]
        vt_slab = mvt_ref[0, :, pl.ds(base, pair)]
        outs = [[] for _ in range(nh)]
        for sc in range(2):
            r0 = base + sc * lc
            gcc = gc_s[pl.ds(r0, lc), :]
            grc = gr_slab[:, sc * lc:(sc + 1) * lc]
            bcum_col = _dot_hi(tri, gcc)
            bcum_row = _dot_nt_hi(grc, tri)
            qc = q_s[pl.ds(r0, lc), :]
            kc = k_s[pl.ds(r0, lc), :]
            new_carry = []
            for hd in range(nh):
                c_aug, m_prev = carry[hd]
                bc_c = bcum_col[:, nh + hd:nh + hd + 1]
                li_c = gcc[:, hd:hd + 1]
                bc_r = bcum_row[nh + hd:nh + hd + 1, :]
                li_r = grc[hd:hd + 1, :]
                gtot = bc_r[:, lc - 1:lc]
                d_t = jnp.where(causal_t, bc_r - bc_c + li_c, NEG_BIG)
                m_inter = bc_r + m_prev
                m_t = jnp.maximum(m_inter, jnp.max(d_t, axis=0, keepdims=True))
                q_h = qc[:, hd * dk:(hd + 1) * dk].astype(BF16)
                k_h = kc[:, hd * dk:(hd + 1) * dk].astype(BF16)
                kt_h = kt_slab[hd * dk:(hd + 1) * dk, sc * lc:(sc + 1) * lc].astype(BF16)
                vt_h = vt_slab[hd * dv:(hd + 1) * dv, sc * lc:(sc + 1) * lc]
                vt_aug = jnp.concatenate([vt_h, ones_rows], axis=0)
                s_t = _dot_nt(k_h, q_h)
                w_t = jnp.exp(d_t - m_t) * s_t
                s_inter = jnp.exp(m_inter - m_t)
                numden = _dot(vt_aug.astype(BF16), w_t.astype(BF16)) + s_inter * _dot_nt(c_aug.astype(BF16), q_h)
                den = numden[dv:dv + 1, :]
                outs[hd].append(numden[:dv, :] / jnp.maximum(jnp.abs(den), jnp.exp(-m_t)))
                a_log = gtot - bc_r + li_r
                a_max = jnp.max(a_log, axis=-1, keepdims=True)
                a_w = jnp.exp(a_log - a_max)
                kvn = _dot_nt((vt_aug * a_w).astype(BF16), kt_h)
                m_new = jnp.maximum(gtot + m_prev, a_max)
                s_old = jnp.exp(gtot + m_prev - m_new)
                s_new = jnp.exp(a_max - m_new)
                new_carry.append((s_old * c_aug + s_new * kvn, m_new))
            carry = tuple(new_carry)
        for hd in range(nh):
            yt_s[hd * dv:(hd + 1) * dv, pl.ds(base, pair)] = jnp.concatenate(outs[hd], axis=-1)
        return carry

    init = tuple((jnp.zeros((dv + 8, dk), F32), jnp.full((1, 1), NEG_BIG, F32)) for _ in range(nh))
    lax.fori_loop(0, s // pair, chunk_pair, init)

    parts = []
    for hd in range(nh):
        blk = yt_s[hd * dv:(hd + 1) * dv, :]
        parts.append(blk * lax.rsqrt(jnp.mean(blk * blk, axis=0, keepdims=True) + EPS))
    y = jnp.concatenate(parts, axis=0).T
    o_gate = _sigmoid(zm_ref[0, :, 2 * MLSTM_QK + MLSTM_V:2 * MLSTM_QK + 2 * MLSTM_V])
    o_ref[0] = (y * ng_ref[...] * o_gate).astype(o_ref.dtype)


def _mlstm(zm, mkt, mvt, mgt, conv_w, conv_b, b_i, b_f, norm_g):
    b, s, _ = zm.shape
    cw_col = conv_w[:, MLSTM_QK:].T
    cb_col = conv_b[MLSTM_QK:].reshape(-1, 1)
    bg = jnp.concatenate([b_i, b_f])
    bg_row = jnp.pad(bg, (0, LANES - 2 * MLSTM_HEADS)).reshape(1, LANES)
    bg_col = bg.reshape(-1, 1)
    per_b = lambda r, c: pl.BlockSpec((1, r, c), lambda i: (i, 0, 0))
    return pl.pallas_call(
        _mlstm_kernel,
        out_shape=jax.ShapeDtypeStruct((b, s, MLSTM_V), BF16),
        grid=(b,),
        in_specs=[per_b(s, M_COLS), per_b(MLSTM_QK, s), per_b(MLSTM_V, s), per_b(8, s),
                  _full(conv_w.shape), _full((1, 2 * MLSTM_QK)), _full(cw_col.shape), _full(cb_col.shape),
                  _full((1, LANES)), _full((8, 1)), _full((1, MLSTM_V))],
        out_specs=per_b(s, MLSTM_V),
        scratch_shapes=[pltpu.VMEM((s, MLSTM_QK), F32), pltpu.VMEM((s, MLSTM_QK), F32), pltpu.VMEM((MLSTM_QK, s), F32),
                        pltpu.VMEM((s, LANES), F32), pltpu.VMEM((8, s), F32), pltpu.VMEM((MLSTM_V, s), F32)],
        compiler_params=_cparams("arbitrary"),
        name="mlstm",
    )(zm, mkt, mvt, mgt, conv_w, conv_b.reshape(1, -1), cw_col, cb_col, bg_row, bg_col, norm_g.reshape(1, -1))


def _merge_kernel(x_ref, g_ref, y0_ref, y1_ref, y2_ref, y3_ref, wg_ref, wb_ref, wo_ref, o_ref):
    x = x_ref[...]
    d = x.shape[1]
    h = _rms_rows(x, g_ref[...]).astype(BF16)
    merged = None
    for n, y_ref in enumerate((y0_ref, y1_ref, y2_ref, y3_ref)):
        gate = _sigmoid(_dot_nt(h, wg_ref[n * d:(n + 1) * d, :]))
        term = gate * _dot(y_ref[...], wb_ref[n])
        merged = term if merged is None else merged + term
    o_ref[...] = x + _dot(merged.astype(BF16), wo_ref[...])


def _merge(x2, g, ys, w_gate, w_branch, w_out, tm):
    t, d = x2.shape
    row = lambda w: pl.BlockSpec((tm, w), lambda i: (i, 0))
    return pl.pallas_call(
        _merge_kernel,
        out_shape=jax.ShapeDtypeStruct((t, d), F32),
        grid=(t // tm,),
        in_specs=[row(d), _full((1, d))] + [row(BRANCH_WIDTH)] * N_BRANCH
                 + [_full(w_gate.shape), _full(w_branch.shape), _full(w_out.shape)],
        out_specs=row(d),
        compiler_params=_cparams("arbitrary"),
        name="merge",
    )(x2, g.reshape(1, d), *ys, w_gate, w_branch, w_out)


def _ffn_kernel(x_ref, g_ref, w1_ref, w3_ref, w2_ref, o_ref):
    x = x_ref[...]
    h = _rms_rows(x, g_ref[...]).astype(BF16)
    u = _silu(_dot(h, w1_ref[...])) * _dot(h, w3_ref[...])
    o_ref[...] = x + _dot(u.astype(BF16), w2_ref[...])


def _ffn(x2, g, w1, w3, w2, tm):
    t, d = x2.shape
    row = pl.BlockSpec((tm, d), lambda i: (i, 0))
    return pl.pallas_call(
        _ffn_kernel,
        out_shape=jax.ShapeDtypeStruct((t, d), F32),
        grid=(t // tm,),
        in_specs=[row, _full((1, d)), _full(w1.shape), _full(w3.shape), _full(w2.shape)],
        out_specs=row,
        compiler_params=_cparams("arbitrary"),
        name="ffn",
    )(x2, g.reshape(1, d), w1, w3, w2)


MOE_CAP = 160


def _moe_router_kernel(x_ref, g_ref, wr_ref, h_ref, rank_ref, gate_ref, rrow_ref, cnt_ref):
    tm = x_ref.shape[0]
    h = _rms_rows(x_ref[...], g_ref[...])
    h_ref[...] = h.astype(BF16)
    logits = _dot_hi(h, wr_ref[...])
    lane = lax.broadcasted_iota(jnp.int32, logits.shape, 1)
    logits = jnp.where(lane < N_EXPERTS, logits, -jnp.inf)
    v1 = jnp.max(logits, axis=-1, keepdims=True)
    i1 = jnp.min(jnp.where(logits == v1, lane, LANES), axis=-1, keepdims=True)
    rest = jnp.where(lane == i1, -jnp.inf, logits)
    v2 = jnp.max(rest, axis=-1, keepdims=True)
    i2 = jnp.min(jnp.where(rest == v2, lane, LANES), axis=-1, keepdims=True)
    e2 = jnp.exp(v2 - v1)
    gate_ref[...] = jnp.where(lane == i1, 1.0 / (1.0 + e2), 0.0) + jnp.where(lane == i2, e2 / (1.0 + e2), 0.0)
    routed = jnp.where((lane == i1) | (lane == i2), 1.0, 0.0)
    r_i = lax.broadcasted_iota(jnp.int32, (tm, tm), 0)
    c_i = lax.broadcasted_iota(jnp.int32, (tm, tm), 1)
    rank = _dot(jnp.where(c_i < r_i, 1.0, 0.0).astype(BF16), routed.astype(BF16))
    rank = jnp.where(routed > 0.0, rank, -1.0)
    rank_ref[...] = rank
    eye = jnp.where(lax.broadcasted_iota(jnp.int32, (LANES, LANES), 0)
                    == lax.broadcasted_iota(jnp.int32, (LANES, LANES), 1), 1.0, 0.0)
    rrow_ref[0] = _dot_nt_hi(eye, rank)[0:N_EXPERTS, :]
    cnt_ref[0] = jnp.broadcast_to(jnp.sum(routed, axis=0, keepdims=True), (8, LANES))


def _moe_expert_kernel(cnt_ref, y_ref, h_ref, rrow_ref, rank_ref, gate_ref, w1_ref, w3_ref, w2_ref, ng_ref,
                       o_ref, acc_s, *, norm_out):
    e = pl.program_id(0)
    i = pl.program_id(1)
    tm = h_ref.shape[0]
    cap = MOE_CAP
    lane = lax.broadcasted_iota(jnp.int32, (tm, LANES), 1)
    rank_col = jnp.sum(jnp.where(lane == e, rank_ref[...], 0.0), axis=-1, keepdims=True)
    gate_col = jnp.sum(jnp.where(lane == e, gate_ref[...], 0.0), axis=-1, keepdims=True)
    rank_row = rrow_ref[0, pl.ds(e, 1), :]
    slot_r = lax.broadcasted_iota(jnp.int32, (cap, tm), 0).astype(F32)
    slot_c = lax.broadcasted_iota(jnp.int32, (tm, cap), 1).astype(F32)
    acc_s[...] = jnp.zeros_like(acc_s)

    def one_pass(p, carry):
        base = (p * cap).astype(F32)
        gather = jnp.where(rank_row - base == slot_r, 1.0, 0.0).astype(BF16)
        xc = _dot(gather, h_ref[...]).astype(BF16)
        u = _silu(_dot(xc, w1_ref[0])) * _dot(xc, w3_ref[0])
        yc_hi, yc_lo = _split_bf16(_dot(u.astype(BF16), w2_ref[0]))
        scatter = jnp.where(rank_col - base == slot_c, 1.0, 0.0).astype(BF16)
        acc_s[...] += _dot(scatter, yc_hi) + _dot(scatter, yc_lo)
        return carry

    n_pass = (cnt_ref[i, e] + cap - 1) // cap
    lax.fori_loop(0, n_pass, one_pass, 0)
    out = y_ref[...] + gate_col * acc_s[...]
    if norm_out:
        is_last = e == pl.num_programs(0) - 1

        @pl.when(is_last)
        def _():
            o_ref[...] = _rms_rows(out, ng_ref[...])

        @pl.when(jnp.logical_not(is_last))
        def _():
            o_ref[...] = out
    else:
        o_ref[...] = out


def _moe(x2, g, router, w1, w3, w2, tm, out_norm_g=None):
    t, d = x2.shape
    n_e, _, f = w1.shape
    n_t = t // tm
    router_p = jnp.pad(router, ((0, 0), (0, LANES - n_e)))
    row = lambda w: pl.BlockSpec((tm, w), lambda i: (i, 0))
    h, rank, gate, rrow, cnt = pl.pallas_call(
        _moe_router_kernel,
        out_shape=[jax.ShapeDtypeStruct((t, d), BF16), jax.ShapeDtypeStruct((t, LANES), F32),
                   jax.ShapeDtypeStruct((t, LANES), F32), jax.ShapeDtypeStruct((n_t, N_EXPERTS, tm), F32),
                   jax.ShapeDtypeStruct((n_t, 8, LANES), F32)],
        grid=(n_t,),
        in_specs=[row(d), _full((1, d)), _full((d, LANES))],
        out_specs=[row(d), row(LANES), row(LANES), pl.BlockSpec((1, N_EXPERTS, tm), lambda i: (i, 0, 0)),
                   pl.BlockSpec((1, 8, LANES), lambda i: (i, 0, 0))],
        compiler_params=_cparams("arbitrary"),
        name="moe_router",
    )(x2, g.reshape(1, d), router_p)
    counts = cnt[:, 0, :n_e].astype(jnp.int32)
    tile = lambda w: pl.BlockSpec((tm, w), lambda e, i, c: (i, 0))
    weight = lambda r, c_: pl.BlockSpec((1, r, c_), lambda e, i, c: (e, 0, 0), pipeline_mode=pl.Buffered(1))
    norm_out = out_norm_g is not None
    norm_g = out_norm_g.reshape(1, d) if norm_out else jnp.ones((1, d), F32)
    return pl.pallas_call(
        functools.partial(_moe_expert_kernel, norm_out=norm_out),
        out_shape=jax.ShapeDtypeStruct((t, d), F32),
        grid_spec=pltpu.PrefetchScalarGridSpec(
            num_scalar_prefetch=1,
            grid=(n_e, n_t),
            in_specs=[tile(d), tile(d), pl.BlockSpec((1, N_EXPERTS, tm), lambda e, i, c: (i, 0, 0)),
                      tile(LANES), tile(LANES), weight(d, f), weight(d, f), weight(f, d),
                      pl.BlockSpec((1, d), lambda e, i, c: (0, 0))],
            out_specs=tile(d),
            scratch_shapes=[pltpu.VMEM((tm, d), F32)]),
        input_output_aliases={1: 0},
        compiler_params=_cparams("arbitrary", "arbitrary"),
        name="moe_experts",
    )(counts, x2, h, rrow, rank, gate, w1, w3, w2, norm_g)


def _final_norm_kernel(x_ref, g_ref, o_ref):
    o_ref[...] = _rms_rows(x_ref[...], g_ref[...])


def _final_norm(x2, g, tm):
    t, d = x2.shape
    row = pl.BlockSpec((tm, d), lambda i: (i, 0))
    return pl.pallas_call(
        _final_norm_kernel,
        out_shape=jax.ShapeDtypeStruct((t, d), F32),
        grid=(t // tm,),
        in_specs=[row, _full((1, d))],
        out_specs=row,
        compiler_params=_cparams("arbitrary"),
        name="final_norm",
    )(x2, g.reshape(1, d))


def kernel(x, norm_mix_g, w_in, dsa_g_cq, dsa_g_ckv, dsa_g_kidx, dsa_w_uq, dsa_w_qidx, dsa_w_uv, rwkv_mu, rwkv_w0, rwkv_w_up, rwkv_a0, rwkv_a_up, rwkv_g_up, rwkv_k_k, rwkv_k_a, rwkv_r_k, rwkv_ln_g, rwkv_ln_b, mlstm_conv_w, mlstm_conv_b, mlstm_b_i, mlstm_b_f, mlstm_norm_g, fox_b_f, w_branch, w_out, norm_ffn_g, ffn_w1, ffn_w3, ffn_w2, moe_router, moe_w1, moe_w3, moe_w2, final_norm_g):
    b, s, d = x.shape
    depth = w_in.shape[0]
    t = b * s
    tm = min(512, s)
    for l in range(depth):
        wts = _inproj_weights(w_in[l])
        (zr, zm, fq, fk, fv, ckv, ki, fcol, cqt, ckvt, wit, frow, mkt, mvt, mgt) = _inproj(
            x, norm_mix_g[l], wts, dsa_g_cq[l], dsa_g_ckv[l], dsa_g_kidx[l], fox_b_f[l], tm)
        y_fox = _fox(fq, fk, fv, fcol, frow, min(1024, s))
        y_dsa = _dsa(cqt, wit, ki, ckv, ckvt, dsa_w_uq[l], dsa_w_qidx[l], dsa_w_uv[l], min(256, s // 2)
                     ).reshape(t, BRANCH_WIDTH)
        y_rwkv = _rwkv(zr, rwkv_mu[l], rwkv_w0[l], rwkv_w_up[l], rwkv_a0[l], rwkv_a_up[l], rwkv_g_up[l], rwkv_k_k[l],
                       rwkv_k_a[l], rwkv_r_k[l], rwkv_ln_g[l], rwkv_ln_b[l], min(16, b), min(4, b), 32
                       ).reshape(t, BRANCH_WIDTH)
        y_mlstm = _mlstm(zm, mkt, mvt, mgt, mlstm_conv_w[l], mlstm_conv_b[l], mlstm_b_i[l], mlstm_b_f[l],
                         mlstm_norm_g[l]).reshape(t, BRANCH_WIDTH)
        ys = (y_dsa, y_rwkv, y_mlstm, y_fox.reshape(t, BRANCH_WIDTH))
        x2 = _merge(x.reshape(t, d), norm_mix_g[l], ys, wts['w_gate'], w_branch[l].astype(BF16),
                    w_out[l].astype(BF16), tm)
        j = l // 2
        if l % 2 == 0:
            x2 = _ffn(x2, norm_ffn_g[l], ffn_w1[j].astype(BF16), ffn_w3[j].astype(BF16), ffn_w2[j].astype(BF16), tm)
        else:
            x2 = _moe(x2, norm_ffn_g[l], moe_router[j], moe_w1[j].astype(BF16), moe_w3[j].astype(BF16),
                      moe_w2[j].astype(BF16), tm, final_norm_g if l == depth - 1 else None)
        x = x2.reshape(b, s, d)
    if depth > 0 and depth % 2 == 0:
        return x
    return _final_norm(x.reshape(t, d), final_norm_g, tm).reshape(b, s, d)
```

```python
import functools

import jax
import jax.numpy as jnp
from jax import lax
from jax.experimental import pallas as pl
from jax.experimental.pallas import tpu as pltpu

F32 = jnp.float32
BF16 = jnp.bfloat16
HIGHEST = lax.Precision.HIGHEST

EPS = 1e-6
NEG_BIG = -1e30

N_BRANCH = 4
BRANCH_WIDTH = 256
DSA_HEADS = 4
DSA_HEAD_DIM = 64
DSA_Q_LATENT = 128
DSA_KV_LATENT = 128
IDX_HEADS = 8
IDX_DIM = 32
TOPK_MAX = 256
DSA_Q_BLOCK = 128

RWKV_HEADS = 4
RWKV_HEAD_DIM = 64
RWKV_WIDTH = RWKV_HEADS * RWKV_HEAD_DIM
RWKV_LORA_W = 64
RWKV_LORA_A = 64
RWKV_LORA_G = 128
RWKV_GN_EPS = 64e-5
RWKV_IN = 3 * RWKV_WIDTH + RWKV_LORA_W + RWKV_LORA_A + RWKV_LORA_G

MLSTM_HEADS = 4
MLSTM_QK_DIM = 32
MLSTM_V_DIM = 64
MLSTM_CHUNK = 256
CONV_WIDTH = 4
GATE_SOFTCAP = 15.0
MLSTM_QK = MLSTM_HEADS * MLSTM_QK_DIM
MLSTM_V = MLSTM_HEADS * MLSTM_V_DIM

FOX_HEADS = 4
FOX_HEAD_DIM = 64
FOX_WIDTH = FOX_HEADS * FOX_HEAD_DIM

N_EXPERTS = 8

VMEM_LIMIT_BYTES = 56 * 1024 * 1024
LANES = 128


def _cparams(*sem):
    return pltpu.CompilerParams(dimension_semantics=sem, vmem_limit_bytes=VMEM_LIMIT_BYTES)


def _dot(a, b):
    return jnp.dot(a, b, preferred_element_type=F32)


def _dot_hi(a, b):
    return jnp.dot(a, b, preferred_element_type=F32, precision=HIGHEST)


def _dot_nt(a, b):
    return lax.dot_general(a, b, (((1,), (1,)), ((), ())), preferred_element_type=F32)


def _dot_nt_hi(a, b):
    return lax.dot_general(a, b, (((1,), (1,)), ((), ())), preferred_element_type=F32, precision=HIGHEST)


def _log_sigmoid(t):
    return jnp.minimum(t, 0.0) - jnp.log1p(jnp.exp(-jnp.abs(t)))


def _sigmoid(t):
    return 1.0 / (1.0 + jnp.exp(-t))


def _silu(t):
    return t * _sigmoid(t)


def _rms_rows(t, g_row):
    return t * lax.rsqrt(jnp.mean(t * t, axis=-1, keepdims=True) + EPS) * g_row


def _rms_cols(t, g_col):
    return t * lax.rsqrt(jnp.mean(t * t, axis=0, keepdims=True) + EPS) * g_col


def _full(shape):
    n = len(shape)
    return pl.BlockSpec(shape, lambda *_: (0,) * n)


T_CQ = 0
T_CKV = T_CQ + DSA_Q_LATENT
T_WIDX = T_CKV + DSA_KV_LATENT
T_FOXF = T_WIDX + IDX_HEADS
T_MK = T_FOXF + 8
T_MV = T_MK + MLSTM_QK
T_MG = T_MV + MLSTM_V
T_ROWS = T_MG + 8
D_COLS = 3 * LANES
M_COLS = 2 * MLSTM_QK + 2 * MLSTM_V + LANES


def _inproj_kernel(x_ref, g_ref, wr_ref, wm_ref, wf_ref, wd_ref, wt_ref,
                   gcq_ref, gckv_col_ref, gckv_row_ref, gki_ref, bf_row_ref, bf_col_ref,
                   zr_ref, zm_ref, q_ref, k_ref, v_ref, ckv_ref, ki_ref, fcol_ref,
                   cqt_ref, ckvt_ref, wit_ref, frow_ref, mkt_ref, mvt_ref, mgt_ref,
                   carry_row, carry_col):
    j = pl.program_id(1)
    tm = x_ref.shape[1]

    @pl.when(j == 0)
    def _():
        carry_row[...] = jnp.zeros_like(carry_row)
        carry_col[...] = jnp.zeros_like(carry_col)

    x = x_ref[0]
    h = _rms_rows(x, g_ref[...]).astype(BF16)

    zr_ref[0] = _dot_nt(h, wr_ref[...])
    zm_ref[0] = _dot_nt(h, wm_ref[...])

    zf = _dot_nt(h, wf_ref[...])
    for hd in range(FOX_HEADS):
        lo = hd * FOX_HEAD_DIM
        q_ref[0, hd] = (zf[:, lo:lo + FOX_HEAD_DIM] * FOX_HEAD_DIM ** -0.5).astype(BF16)
        k_ref[0, hd] = zf[:, FOX_WIDTH + lo:FOX_WIDTH + lo + FOX_HEAD_DIM].astype(BF16)
        v_ref[0, hd] = zf[:, 2 * FOX_WIDTH + lo:2 * FOX_WIDTH + lo + FOX_HEAD_DIM].astype(BF16)

    zd = _dot_nt(h, wd_ref[...])
    ckv_ref[0] = _rms_rows(zd[:, :LANES], gckv_row_ref[...]).astype(BF16)
    ki_pieces = [p.astype(F32) for p in _split3_bf16(_rms_rows(zd[:, LANES:LANES + IDX_DIM], gki_ref[...]))]
    ki_pad = jnp.zeros((tm, IDX_K_COLS - len(IDX_SPLIT_K) * IDX_DIM), F32)
    ki_ref[0] = jnp.concatenate([ki_pieces[kp] for kp, _ in IDX_SPLIT_K] + [ki_pad], axis=1).astype(BF16)
    lf_col = _log_sigmoid(zd[:, 2 * LANES:] + bf_row_ref[...])
    r_i = lax.broadcasted_iota(jnp.int32, (tm, tm), 0)
    c_i = lax.broadcasted_iota(jnp.int32, (tm, tm), 1)
    tri = jnp.where(c_i <= r_i, 1.0, 0.0).astype(BF16)
    cum_col = sum(_dot(tri, p) for p in _split3_bf16(lf_col)) + carry_row[...]
    fcol_ref[0] = cum_col
    carry_row[...] = cum_col[tm - 1:tm, :]

    zt = _dot_nt(wt_ref[...], h)
    cqt_ref[0] = _rms_cols(zt[T_CQ:T_CKV], gcq_ref[...])
    ckvt_ref[0] = _rms_cols(zt[T_CKV:T_WIDX], gckv_col_ref[...]).astype(BF16)
    wit_ref[0] = zt[T_WIDX:T_FOXF] * IDX_HEADS ** -0.5
    lf_row = _log_sigmoid(zt[T_FOXF:T_MK] + bf_col_ref[...])
    cum_row = sum(_dot_nt(p, tri) for p in _split3_bf16(lf_row)) + carry_col[:, 0:1]
    frow_ref[0] = cum_row
    carry_col[...] = jnp.broadcast_to(cum_row[:, tm - 1:tm], carry_col.shape)
    mkt_ref[0] = zt[T_MK:T_MV]
    mvt_ref[0] = zt[T_MV:T_MG]
    mgt_ref[0] = zt[T_MG:T_ROWS]


def _inproj_weights(w_in_l):
    wt = w_in_l.T
    o = 0
    w_cq = wt[o:o + DSA_Q_LATENT]; o += DSA_Q_LATENT
    w_ckv = wt[o:o + DSA_KV_LATENT]; o += DSA_KV_LATENT
    w_kidx = wt[o:o + IDX_DIM]; o += IDX_DIM
    w_widx = wt[o:o + IDX_HEADS]; o += IDX_HEADS
    w_rwkv = wt[o:o + RWKV_IN]; o += RWKV_IN
    w_mq = wt[o:o + MLSTM_QK]; o += MLSTM_QK
    w_mk = wt[o:o + MLSTM_QK]; o += MLSTM_QK
    w_mv = wt[o:o + MLSTM_V]; o += MLSTM_V
    w_mo = wt[o:o + MLSTM_V]; o += MLSTM_V
    w_mi = wt[o:o + MLSTM_HEADS]; o += MLSTM_HEADS
    w_mf = wt[o:o + MLSTM_HEADS]; o += MLSTM_HEADS
    w_fox = wt[o:o + 3 * FOX_WIDTH]; o += 3 * FOX_WIDTH
    w_ff = wt[o:o + FOX_HEADS]; o += FOX_HEADS
    w_gate = wt[o:]

    def padr(w, n):
        return jnp.pad(w, ((0, n - w.shape[0]), (0, 0)))

    w_d = jnp.concatenate([w_ckv, padr(w_kidx, LANES), padr(w_ff, LANES)], axis=0)
    w_m = jnp.concatenate([w_mq, w_mk, w_mv, w_mo, padr(jnp.concatenate([w_mi, w_mf], axis=0), LANES)], axis=0)
    w_t = jnp.concatenate([w_cq, w_ckv, w_widx, padr(w_ff, 8), w_mk, w_mv, w_mi, w_mf], axis=0)
    assert w_t.shape[0] == T_ROWS
    cast = lambda w: w.astype(BF16)
    return dict(w_r=cast(w_rwkv), w_m=cast(w_m), w_f=cast(w_fox), w_d=cast(w_d), w_t=cast(w_t), w_gate=cast(w_gate))


def _inproj(x, g, wts, g_cq, g_ckv, g_kidx, fox_b_f, tm):
    b, s, d = x.shape
    nj = s // tm
    bf_row = jnp.pad(fox_b_f, (0, LANES - FOX_HEADS)).reshape(1, LANES)
    bf_col = jnp.pad(fox_b_f, (0, 8 - FOX_HEADS)).reshape(8, 1)
    row = lambda w: pl.BlockSpec((1, tm, w), lambda i, j: (i, j, 0))
    head = pl.BlockSpec((1, FOX_HEADS, tm, FOX_HEAD_DIM), lambda i, j: (i, 0, j, 0))
    col = lambda r: pl.BlockSpec((1, r, tm), lambda i, j: (i, 0, j))
    out_shape = [
        jax.ShapeDtypeStruct((b, s, RWKV_IN), F32),
        jax.ShapeDtypeStruct((b, s, M_COLS), F32),
        jax.ShapeDtypeStruct((b, FOX_HEADS, s, FOX_HEAD_DIM), BF16),
        jax.ShapeDtypeStruct((b, FOX_HEADS, s, FOX_HEAD_DIM), BF16),
        jax.ShapeDtypeStruct((b, FOX_HEADS, s, FOX_HEAD_DIM), BF16),
        jax.ShapeDtypeStruct((b, s, DSA_KV_LATENT), BF16),
        jax.ShapeDtypeStruct((b, s, IDX_K_COLS), BF16),
        jax.ShapeDtypeStruct((b, s, LANES), F32),
        jax.ShapeDtypeStruct((b, DSA_Q_LATENT, s), F32),
        jax.ShapeDtypeStruct((b, DSA_KV_LATENT, s), BF16),
        jax.ShapeDtypeStruct((b, IDX_HEADS, s), F32),
        jax.ShapeDtypeStruct((b, 8, s), F32),
        jax.ShapeDtypeStruct((b, MLSTM_QK, s), F32),
        jax.ShapeDtypeStruct((b, MLSTM_V, s), F32),
        jax.ShapeDtypeStruct((b, 8, s), F32),
    ]
    out_specs = [row(RWKV_IN), row(M_COLS), head, head, head, row(DSA_KV_LATENT), row(IDX_K_COLS), row(LANES),
                 col(DSA_Q_LATENT), col(DSA_KV_LATENT), col(IDX_HEADS), col(8), col(MLSTM_QK), col(MLSTM_V), col(8)]
    in_specs = [row(d), _full((1, d)), _full(wts['w_r'].shape), _full(wts['w_m'].shape), _full(wts['w_f'].shape),
                _full(wts['w_d'].shape), _full(wts['w_t'].shape),
                _full((DSA_Q_LATENT, 1)), _full((DSA_KV_LATENT, 1)), _full((1, DSA_KV_LATENT)), _full((1, IDX_DIM)),
                _full((1, LANES)), _full((8, 1))]
    return pl.pallas_call(
        _inproj_kernel,
        out_shape=out_shape,
        grid=(b, nj),
        in_specs=in_specs,
        out_specs=out_specs,
        scratch_shapes=[pltpu.VMEM((1, LANES), F32), pltpu.VMEM((8, LANES), F32)],
        compiler_params=_cparams("arbitrary", "arbitrary"),
        name="inproj",
    )(x, g.reshape(1, d), wts['w_r'], wts['w_m'], wts['w_f'], wts['w_d'], wts['w_t'],
      g_cq.reshape(-1, 1), g_ckv.reshape(-1, 1), g_ckv.reshape(1, -1), g_kidx.reshape(1, -1), bf_row, bf_col)


def _fox_kernel(q_ref, k_ref, v_ref, fcol_ref, frow_ref, o_ref):
    qi = pl.program_id(1)
    tq = q_ref.shape[2]
    outs = []
    for hd in range(FOX_HEADS):
        q = q_ref[0, hd]
        fq = fcol_ref[0][:, hd:hd + 1]

        def scores(j):
            start = pl.multiple_of(j * tq, tq)
            k = k_ref[0, hd, pl.ds(start, tq), :]
            v = v_ref[0, hd, pl.ds(start, tq), :]
            fk = frow_ref[0, hd:hd + 1, pl.ds(start, tq)]
            return _dot_nt(q, k) + (fq - fk), v

        def update(carry, sc, v):
            m, l, acc = carry
            m_new = jnp.maximum(m, jnp.max(sc, axis=-1, keepdims=True))
            alpha = jnp.exp(m - m_new)
            p = jnp.exp(sc - m_new)
            l = alpha * l + jnp.sum(p, axis=-1, keepdims=True)
            acc = alpha * acc + _dot(p.astype(BF16), v)
            return m_new, l, acc

        def body(j, carry):
            sc, v = scores(j)
            return update(carry, sc, v)

        init = (jnp.full((tq, 1), NEG_BIG, F32), jnp.zeros((tq, 1), F32), jnp.zeros((tq, FOX_HEAD_DIM), F32))
        carry = lax.fori_loop(0, qi, body, init)
        sc, v = scores(qi)
        r_i = lax.broadcasted_iota(jnp.int32, (tq, tq), 0)
        c_i = lax.broadcasted_iota(jnp.int32, (tq, tq), 1)
        sc = jnp.where(c_i <= r_i, sc, NEG_BIG)
        m, l, acc = update(carry, sc, v)
        outs.append(acc / l)
    o_ref[0] = jnp.concatenate(outs, axis=-1).astype(o_ref.dtype)


def _fox(q, k, v, fcol, frow, tq):
    b, nh, s, d = q.shape
    return pl.pallas_call(
        _fox_kernel,
        out_shape=jax.ShapeDtypeStruct((b, s, nh * d), BF16),
        grid=(b, s // tq),
        in_specs=[
            pl.BlockSpec((1, nh, tq, d), lambda i, j: (i, 0, j, 0)),
            pl.BlockSpec((1, nh, s, d), lambda i, j: (i, 0, 0, 0)),
            pl.BlockSpec((1, nh, s, d), lambda i, j: (i, 0, 0, 0)),
            pl.BlockSpec((1, tq, LANES), lambda i, j: (i, j, 0)),
            pl.BlockSpec((1, 8, s), lambda i, j: (i, 0, 0)),
        ],
        out_specs=pl.BlockSpec((1, tq, nh * d), lambda i, j: (i, j, 0)),
        compiler_params=_cparams("arbitrary", "arbitrary"),
        name="fox",
    )(q, k, v, fcol, frow)


INT_MIN = -2 ** 31


IDX_SPLIT_K = ((0, 0), (0, 1), (1, 0), (0, 2), (1, 1), (2, 0))
IDX_K_COLS = 2 * LANES


def _split3_bf16(t):
    p0 = t.astype(BF16)
    r1 = t - p0.astype(F32)
    p1 = r1.astype(BF16)
    p2 = (r1 - p1.astype(F32)).astype(BF16)
    return p0, p1, p2


def _tree_sum(parts):
    while len(parts) > 1:
        parts = [parts[i] + parts[i + 1] for i in range(0, len(parts) - 1, 2)] + (parts[-1:] if len(parts) % 2 else [])
    return parts[0]


HALF = 1 << 15


def _dsa_kernel(cqt_ref, wit_ref, ki6_ref, ckv_ref, ckvt_ref, wuqt_ref, wqit_ref, wuvt_ref, o_ref,
                key_s, hi_s, lo_s, *, ck):
    qb = pl.program_id(1)
    s = ki6_ref.shape[1]
    nq = cqt_ref.shape[2]
    topk = min(TOPK_MAX, s // 4)
    n_ck = (qb * nq + nq + ck - 1) // ck
    lat = DSA_KV_LATENT

    cq = cqt_ref[0]
    qi_t = _dot_hi(wqit_ref[...], cq)
    zeros_pad = jnp.zeros((IDX_K_COLS - len(IDX_SPLIT_K) * IDX_DIM, nq), BF16)
    q_blocks = []
    for hd in range(IDX_HEADS):
        pieces = _split3_bf16(qi_t[hd * IDX_DIM:(hd + 1) * IDX_DIM, :])
        q_blocks.append(jnp.concatenate([pieces[qp] for _, qp in IDX_SPLIT_K] + [zeros_pad], axis=0))
    q6 = jnp.concatenate(q_blocks, axis=1)
    wi = wit_ref[0] * IDX_DIM ** -0.5
    tpos = qb * nq + lax.broadcasted_iota(jnp.int32, (ck, nq), 1)
    row = lax.broadcasted_iota(jnp.int32, (ck, nq), 0)

    def chunk_start(c):
        return pl.multiple_of(c * ck, ck)

    def score_chunk(c, carry):
        c0 = chunk_start(c)
        ki6 = ki6_ref[0, pl.ds(c0, ck), :]
        score = None
        for hp in range(0, IDX_HEADS, 2):
            dots = _dot(ki6, q6[:, hp * nq:(hp + 2) * nq])
            for i in range(2):
                term = wi[hp + i:hp + i + 1, :] * jnp.maximum(dots[:, i * nq:(i + 1) * nq], 0.0)
                score = term if score is None else score + term
        bits = pltpu.bitcast(score, jnp.int32)
        key = jnp.where(bits < 0, bits ^ 0x7FFFFFFF, bits)
        key = jnp.where(score == 0.0, 0, key)
        key = jnp.where(c0 + row <= tpos, key, INT_MIN)
        key_s[pl.ds(c0, ck), :] = key
        hi_s[pl.ds(c0, ck), :] = jnp.right_shift(key, 16).astype(jnp.int16)
        lo_s[pl.ds(c0, ck), :] = ((key & 0xFFFF) - HALF).astype(jnp.int16)
        return carry

    n_pair = (n_ck + 1) // 2
    lax.fori_loop(0, n_pair, lambda j, carry: score_chunk(2 * j + 1, score_chunk(2 * j, carry)), 0)

    def count(mask_fn):
        def body(j, acc):
            parts = []
            for c in (2 * j, 2 * j + 1):
                c0 = chunk_start(c)
                ones = jnp.where(mask_fn(key_s[pl.ds(c0, ck), :], c0 + row), 1, 0)
                parts += [ones[i * 8:(i + 1) * 8] for i in range(ck // 8)]
            return acc + _tree_sum(parts)
        acc = lax.fori_loop(0, n_pair, body, jnp.zeros((8, nq), jnp.int32))
        return jnp.sum(acc, axis=0, keepdims=True)

    def count16(ref, mask_fn):
        def body(j, acc):
            parts = []
            for c in (2 * j, 2 * j + 1):
                ones = jnp.where(mask_fn(ref[pl.ds(chunk_start(c), ck), :]), jnp.int16(1), jnp.int16(0))
                parts += [ones[i * 16:(i + 1) * 16] for i in range(ck // 16)]
            return acc + _tree_sum(parts).astype(jnp.int32)
        acc = lax.fori_loop(0, n_pair, body, jnp.zeros((16, nq), jnp.int32))
        return jnp.sum(acc, axis=0, keepdims=True)

    def kth_largest16(ref, k):
        def bit(i, lo):
            cand = lo + jnp.left_shift(jnp.int32(1), 15 - i)
            cand16 = cand.astype(jnp.int16)
            return jnp.where(count16(ref, lambda x: x >= cand16) >= k, cand, lo)
        return lax.fori_loop(0, 16, bit, jnp.full((1, nq), -HALF, jnp.int32))

    thr_hi = kth_largest16(hi_s, topk)
    thr_hi16 = thr_hi.astype(jnp.int16)
    need_lo = topk - count16(hi_s, lambda x: x > thr_hi16)

    def mask_lo(j, carry):
        for c in (2 * j, 2 * j + 1):
            c0 = chunk_start(c)
            lo_s[pl.ds(c0, ck), :] = jnp.where(hi_s[pl.ds(c0, ck), :] == thr_hi16, lo_s[pl.ds(c0, ck), :],
                                               jnp.int16(-HALF))
        return carry

    lax.fori_loop(0, n_pair, mask_lo, 0)
    thr = thr_hi * (2 * HALF) + (kth_largest16(lo_s, need_lo) + HALF)
    n_ge = count(lambda key, pos: key >= thr)

    n_bits = s.bit_length()

    def tie_search():
        need = topk - count(lambda key, pos: key > thr)

        def index_bit(i, lo):
            cand = lo + jnp.left_shift(jnp.int32(1), n_bits - 1 - i)
            return jnp.where(count(lambda key, pos: (key == thr) & (pos < cand)) < need, cand, lo)

        return lax.fori_loop(0, n_bits, index_bit, jnp.zeros((1, nq), jnp.int32))

    surplus = jnp.max(jnp.where((n_ge > topk) & (thr > INT_MIN), 1, 0))
    last = lax.cond(surplus > 0, tie_search, lambda: jnp.full((1, nq), s, jnp.int32))

    q_t = (_dot(wuqt_ref[...], cq.astype(BF16)) * lat ** -0.5).astype(BF16)

    def attend(c, carry):
        c0 = chunk_start(c)
        key = key_s[pl.ds(c0, ck), :]
        pos = c0 + row
        sel = ((key > thr) | ((key == thr) & (pos <= last))) & (pos <= tpos)
        bias = jnp.where(sel, 0.0, NEG_BIG)
        ckv = ckv_ref[0, pl.ds(c0, ck), :]
        ckvt = ckvt_ref[0, :, pl.ds(c0, ck)]
        new = []
        for hd in range(DSA_HEADS):
            m, l, acc = carry[hd]
            lg = _dot(ckv, q_t[hd * lat:(hd + 1) * lat, :]) + bias
            m_new = jnp.maximum(m, jnp.max(lg, axis=0, keepdims=True))
            alpha = jnp.exp(m - m_new)
            p = jnp.exp(lg - m_new)
            new.append((m_new, alpha * l + jnp.sum(p, axis=0, keepdims=True),
                        alpha * acc + _dot(ckvt, p.astype(BF16))))
        return tuple(new)

    init = tuple((jnp.full((1, nq), NEG_BIG, F32), jnp.zeros((1, nq), F32), jnp.zeros((lat, nq), F32))
                 for _ in range(DSA_HEADS))
    final = lax.fori_loop(0, n_pair, lambda j, carry: attend(2 * j + 1, attend(2 * j, carry)), init)
    outs = [_dot(wuvt_ref[hd], (acc / l).astype(BF16)) for hd, (_, l, acc) in enumerate(final)]
    o_ref[0] = jnp.concatenate(outs, axis=0).T.astype(o_ref.dtype)


def _dsa(cqt, wit, ki6, ckv, ckvt, w_uq, w_qidx, w_uv, ck):
    b, c, s = cqt.shape
    nq = DSA_Q_BLOCK
    wuqt = w_uq.reshape(c, -1).T.astype(BF16)
    wqit = w_qidx.reshape(c, -1).T
    wuvt = jnp.transpose(w_uv, (0, 2, 1)).astype(BF16)
    return pl.pallas_call(
        functools.partial(_dsa_kernel, ck=ck),
        out_shape=jax.ShapeDtypeStruct((b, s, DSA_HEADS * DSA_HEAD_DIM), BF16),
        grid=(b, s // nq),
        in_specs=[pl.BlockSpec((1, c, nq), lambda i, j: (i, 0, j)),
                  pl.BlockSpec((1, IDX_HEADS, nq), lambda i, j: (i, 0, j)),
                  pl.BlockSpec((1, s, IDX_K_COLS), lambda i, j: (i, 0, 0)),
                  pl.BlockSpec((1, s, DSA_KV_LATENT), lambda i, j: (i, 0, 0)),
                  pl.BlockSpec((1, DSA_KV_LATENT, s), lambda i, j: (i, 0, 0)),
                  _full(wuqt.shape), _full(wqit.shape), _full(wuvt.shape)],
        out_specs=pl.BlockSpec((1, nq, DSA_HEADS * DSA_HEAD_DIM), lambda i, j: (i, j, 0)),
        scratch_shapes=[pltpu.VMEM((s, nq), jnp.int32), pltpu.VMEM((s, nq), jnp.int16),
                        pltpu.VMEM((s, nq), jnp.int16)],
        compiler_params=_cparams("arbitrary", "arbitrary"),
        name="dsa",
    )(cqt, wit, ki6, ckv, ckvt, wuqt, wqit, wuvt)


Y_PAD = 8


def _head_ones(n, dtype):
    r_i = lax.broadcasted_iota(jnp.int32, (n, n), 0) // RWKV_HEAD_DIM
    c_i = lax.broadcasted_iota(jnp.int32, (n, n), 1) // RWKV_HEAD_DIM
    return jnp.where(r_i == c_i, 1.0, 0.0).astype(dtype)


def _split_bf16(t):
    hi = t.astype(BF16)
    lo = (t - hi.astype(F32)).astype(BF16)
    return hi, lo


def _rwkv_kernel(z_ref, mu_ref, w0_ref, wup_ref, a0_ref, aup_ref, gup_ref, kk_ref, ka_ref, rk_ref, lng_ref, lnb_ref,
                 o_ref, st_s, prev_s, step_s, y_s, bonus_s, gate_s, *, grp):
    c = pl.program_id(1)
    n_g, tc, _ = z_ref.shape
    hd, wd_ = RWKV_HEAD_DIM, RWKV_WIDTH

    @pl.when(c == 0)
    def _():
        st_s[...] = jnp.zeros_like(st_s)
        prev_s[...] = jnp.zeros_like(prev_s)

    ones_b = _head_ones(wd_, BF16)
    diag = jnp.where(lax.broadcasted_iota(jnp.int32, (hd, wd_), 0)
                     == lax.broadcasted_iota(jnp.int32, (hd, wd_), 1) % hd, 1.0, 0.0).astype(F32)

    def head_sum(t):
        hi, lo = _split_bf16(t)
        return _dot(hi, ones_b) + _dot(lo, ones_b)

    zs = []
    for g in range(n_g):
        z = z_ref[g]
        row = lax.broadcasted_iota(jnp.int32, z.shape, 0)
        z_prev = jnp.where(row == 0, prev_s[g], pltpu.roll(z, 1, 0))
        prev_s[g] = z[tc - 1:tc, :]
        zs.append(z + mu_ref[...] * (z_prev - z))
    z = jnp.concatenate(zs, axis=0)
    r = z[:, 0:wd_]
    k = z[:, wd_:2 * wd_]
    v = z[:, 2 * wd_:3 * wd_]
    o = 3 * wd_
    w_lora = z[:, o:o + RWKV_LORA_W]
    a_lora = z[:, o + RWKV_LORA_W:o + RWKV_LORA_W + RWKV_LORA_A]
    g_lora = z[:, o + RWKV_LORA_W + RWKV_LORA_A:]
    w_log = _log_sigmoid(w0_ref[...] + _dot_hi(jnp.tanh(w_lora), wup_ref[...])) - 0.5
    a = _sigmoid(a0_ref[...] + _dot_hi(a_lora, aup_ref[...]))
    gate_s[...] = _dot_hi(_sigmoid(g_lora), gup_ref[...])
    kk = k * kk_ref[...]
    kk = kk / jnp.maximum(jnp.sqrt(head_sum(kk * kk)), 1e-12)
    k = k * (1.0 + (a - 1.0) * ka_ref[...])
    step_s[0] = jnp.exp(-jnp.exp(w_log))
    step_s[1] = -kk
    step_s[2] = kk * a
    step_s[3] = k
    step_s[4] = r
    step_s[5] = v
    bonus_s[...] = head_sum(r * k * rk_ref[...]) * v

    n_grp = n_g // grp

    def rows(kind, g0, t):
        return jnp.concatenate(
            [jnp.broadcast_to(step_s[kind, pl.ds((g0 + i) * tc + t, 1), :], (hd, wd_)) for i in range(grp)], axis=0)

    def store_y(yb, g0, t_write):
        yb = yb * diag_g
        for i in range(grp):
            tile = jnp.sum(yb[i * hd:(i + 1) * hd].reshape(hd // 8, 8, wd_), axis=0)
            y_s[pl.ds(Y_PAD + (g0 + i) * tc + t_write, 1), :] = jnp.sum(tile, axis=0, keepdims=True)

    def rows_b(kind, g0, t):
        tiles = []
        for i in range(grp):
            r16 = jnp.broadcast_to(step_s[kind, pl.ds((g0 + i) * tc + t, 1), :], (16, wd_)).astype(BF16)
            tiles.append(jnp.broadcast_to(r16[None], (hd // 16, 16, wd_)).reshape(hd, wd_))
        return jnp.concatenate(tiles, axis=0)

    def step(t, carry):
        for q in range(n_grp):
            g0 = q * grp
            st = st_s[q]
            st_b = st.astype(BF16)
            sa = _dot(st_b * rows_b(1, g0, t), ones_b)
            store_y(_dot(st_b * rows_b(4, g0, jnp.maximum(t - 1, 0)), ones_b), g0, t - 1)
            vb = _dot(rows_b(5, g0, t) * diag_b, ones_b)
            st_s[q] = st * rows(0, g0, t) + sa * rows(2, g0, t) + vb * rows(3, g0, t)
        return carry

    diag_g = jnp.concatenate([diag] * grp, axis=0)
    diag_b = diag_g.astype(BF16)
    lax.fori_loop(0, tc, step, 0, unroll=4)
    for q in range(n_grp):
        store_y(_dot(st_s[q].astype(BF16) * rows_b(4, q * grp, tc - 1), ones_b), q * grp, tc - 1)

    y = y_s[pl.ds(Y_PAD, n_g * tc), :]
    mean = head_sum(y) * (1.0 / hd)
    yc = y - mean
    var = head_sum(yc * yc) * (1.0 / hd)
    yn = yc * lax.rsqrt(var + RWKV_GN_EPS) * lng_ref[...] + lnb_ref[...]
    out = ((yn + bonus_s[...]) * gate_s[...]).astype(o_ref.dtype)
    for g in range(n_g):
        o_ref[g] = out[g * tc:(g + 1) * tc]


def _rwkv(zr, mu, w0, w_up, a0, a_up, g_up, k_k, k_a, r_k, ln_g, ln_b, n_g, grp, tc):
    b, s, zin = zr.shape
    wd_ = RWKV_WIDTH
    vec = lambda p: p.reshape(1, -1)
    params = [vec(mu), vec(w0), w_up, vec(a0), a_up, g_up, vec(k_k), vec(k_a), vec(r_k), vec(ln_g), vec(ln_b)]
    return pl.pallas_call(
        functools.partial(_rwkv_kernel, grp=grp),
        out_shape=jax.ShapeDtypeStruct((b, s, wd_), BF16),
        grid=(b // n_g, s // tc),
        in_specs=[pl.BlockSpec((n_g, tc, zin), lambda i, c: (i, c, 0))] + [_full(p.shape) for p in params],
        out_specs=pl.BlockSpec((n_g, tc, wd_), lambda i, c: (i, c, 0)),
        scratch_shapes=[pltpu.VMEM((n_g // grp, grp * RWKV_HEAD_DIM, wd_), F32), pltpu.VMEM((n_g, 1, zin), F32),
                        pltpu.VMEM((6, n_g * tc, wd_), F32), pltpu.VMEM((Y_PAD + n_g * tc, wd_), F32),
                        pltpu.VMEM((n_g * tc, wd_), F32), pltpu.VMEM((n_g * tc, wd_), F32)],
        compiler_params=_cparams("arbitrary", "arbitrary"),
        name="rwkv",
    )(zr, *params)


def _softcap(t):
    return GATE_SOFTCAP * jnp.tanh(t / GATE_SOFTCAP)


def _mlstm_kernel(zm_ref, mkt_ref, mvt_ref, mgt_ref, cw_row_ref, cb_row_ref, cw_col_ref, cb_col_ref,
                  bg_row_ref, bg_col_ref, ng_ref, o_ref, q_s, k_s, kt_s, gc_s, gr_s, yt_s):
    s = zm_ref.shape[1]
    nh, dk, dv, lc = MLSTM_HEADS, MLSTM_QK_DIM, MLSTM_V_DIM, MLSTM_CHUNK
    pair = 2 * lc

    qk = zm_ref[0, :, 0:2 * MLSTM_QK]
    pos_r = lax.broadcasted_iota(jnp.int32, qk.shape, 0)
    acc = cb_row_ref[...] + qk * cw_row_ref[CONV_WIDTH - 1:CONV_WIDTH, :]
    for r in range(1, CONV_WIDTH):
        sh = jnp.where(pos_r >= r, pltpu.roll(qk, r, 0), 0.0)
        acc = acc + sh * cw_row_ref[CONV_WIDTH - 1 - r:CONV_WIDTH - r, :]
    acc = _silu(acc)
    q_s[...] = acc[:, :MLSTM_QK] * dk ** -0.5
    k_s[...] = acc[:, MLSTM_QK:]
    kt = mkt_ref[0]
    pos_c = lax.broadcasted_iota(jnp.int32, kt.shape, 1)
    acc_t = cb_col_ref[...] + kt * cw_col_ref[:, CONV_WIDTH - 1:CONV_WIDTH]
    for r in range(1, CONV_WIDTH):
        sh = jnp.where(pos_c >= r, pltpu.roll(kt, r, 1), 0.0)
        acc_t = acc_t + sh * cw_col_ref[:, CONV_WIDTH - 1 - r:CONV_WIDTH - r]
    kt_s[...] = _silu(acc_t)

    gcol = _softcap(zm_ref[0, :, 2 * MLSTM_QK + 2 * MLSTM_V:] + bg_row_ref[...])
    lane = lax.broadcasted_iota(jnp.int32, gcol.shape, 1)
    gc_s[...] = jnp.where(lane < nh, gcol, _log_sigmoid(gcol))
    grow = _softcap(mgt_ref[0] + bg_col_ref[...])
    sub = lax.broadcasted_iota(jnp.int32, grow.shape, 0)
    gr_s[...] = jnp.where(sub < nh, grow, _log_sigmoid(grow))

    r_i = lax.broadcasted_iota(jnp.int32, (lc, lc), 0)
    c_i = lax.broadcasted_iota(jnp.int32, (lc, lc), 1)
    tri = jnp.where(c_i <= r_i, 1.0, 0.0).astype(F32)
    causal_t = r_i <= c_i
    ones_rows = jnp.ones((8, lc), F32)

    def chunk_pair(p, carry):
        base = pl.multiple_of(p * pair, pair)
        gr_slab = gr_s[:, pl.ds(base, pair)]
        kt_slab = kt_s[:, pl.ds(base, pair)]
        vt_slab = mvt_ref[0, :, pl.ds(base, pair)]
        outs = [[] for _ in range(nh)]
        for sc in range(2):
            r0 = base + sc * lc
            gcc = gc_s[pl.ds(r0, lc), :]
            grc = gr_slab[:, sc * lc:(sc + 1) * lc]
            bcum_col = _dot_hi(tri, gcc)
            bcum_row = _dot_nt_hi(grc, tri)
            qc = q_s[pl.ds(r0, lc), :]
            kc = k_s[pl.ds(r0, lc), :]
            new_carry = []
            for hd in range(nh):
                c_aug, m_prev = carry[hd]
                bc_c = bcum_col[:, nh + hd:nh + hd + 1]
                li_c = gcc[:, hd:hd + 1]
                bc_r = bcum_row[nh + hd:nh + hd + 1, :]
                li_r = grc[hd:hd + 1, :]
                gtot = bc_r[:, lc - 1:lc]
                d_t = jnp.where(causal_t, bc_r - bc_c + li_c, NEG_BIG)
                m_inter = bc_r + m_prev
                m_t = jnp.maximum(m_inter, jnp.max(d_t, axis=0, keepdims=True))
                q_h = qc[:, hd * dk:(hd + 1) * dk].astype(BF16)
                k_h = kc[:, hd * dk:(hd + 1) * dk].astype(BF16)
                kt_h = kt_slab[hd * dk:(hd + 1) * dk, sc * lc:(sc + 1) * lc].astype(BF16)
                vt_h = vt_slab[hd * dv:(hd + 1) * dv, sc * lc:(sc + 1) * lc]
                vt_aug = jnp.concatenate([vt_h, ones_rows], axis=0)
                s_t = _dot_nt(k_h, q_h)
                w_t = jnp.exp(d_t - m_t) * s_t
                s_inter = jnp.exp(m_inter - m_t)
                numden = _dot(vt_aug.astype(BF16), w_t.astype(BF16)) + s_inter * _dot_nt(c_aug.astype(BF16), q_h)
                den = numden[dv:dv + 1, :]
                outs[hd].append(numden[:dv, :] / jnp.maximum(jnp.abs(den), jnp.exp(-m_t)))
                a_log = gtot - bc_r + li_r
                a_max = jnp.max(a_log, axis=-1, keepdims=True)
                a_w = jnp.exp(a_log - a_max)
                kvn = _dot_nt((vt_aug * a_w).astype(BF16), kt_h)
                m_new = jnp.maximum(gtot + m_prev, a_max)
                s_old = jnp.exp(gtot + m_prev - m_new)
                s_new = jnp.exp(a_max - m_new)
                new_carry.append((s_old * c_aug + s_new * kvn, m_new))
            carry = tuple(new_carry)
        for hd in range(nh):
            yt_s[hd * dv:(hd + 1) * dv, pl.ds(base, pair)] = jnp.concatenate(outs[hd], axis=-1)
        return carry

    init = tuple((jnp.zeros((dv + 8, dk), F32), jnp.full((1, 1), NEG_BIG, F32)) for _ in range(nh))
    lax.fori_loop(0, s // pair, chunk_pair, init)

    parts = []
    for hd in range(nh):
        blk = yt_s[hd * dv:(hd + 1) * dv, :]
        parts.append(blk * lax.rsqrt(jnp.mean(blk * blk, axis=0, keepdims=True) + EPS))
    y = jnp.concatenate(parts, axis=0).T
    o_gate = _sigmoid(zm_ref[0, :, 2 * MLSTM_QK + MLSTM_V:2 * MLSTM_QK + 2 * MLSTM_V])
    o_ref[0] = (y * ng_ref[...] * o_gate).astype(o_ref.dtype)


def _mlstm(zm, mkt, mvt, mgt, conv_w, conv_b, b_i, b_f, norm_g):
    b, s, _ = zm.shape
    cw_col = conv_w[:, MLSTM_QK:].T
    cb_col = conv_b[MLSTM_QK:].reshape(-1, 1)
    bg = jnp.concatenate([b_i, b_f])
    bg_row = jnp.pad(bg, (0, LANES - 2 * MLSTM_HEADS)).reshape(1, LANES)
    bg_col = bg.reshape(-1, 1)
    per_b = lambda r, c: pl.BlockSpec((1, r, c), lambda i: (i, 0, 0))
    return pl.pallas_call(
        _mlstm_kernel,
        out_shape=jax.ShapeDtypeStruct((b, s, MLSTM_V), BF16),
        grid=(b,),
        in_specs=[per_b(s, M_COLS), per_b(MLSTM_QK, s), per_b(MLSTM_V, s), per_b(8, s),
                  _full(conv_w.shape), _full((1, 2 * MLSTM_QK)), _full(cw_col.shape), _full(cb_col.shape),
                  _full((1, LANES)), _full((8, 1)), _full((1, MLSTM_V))],
        out_specs=per_b(s, MLSTM_V),
        scratch_shapes=[pltpu.VMEM((s, MLSTM_QK), F32), pltpu.VMEM((s, MLSTM_QK), F32), pltpu.VMEM((MLSTM_QK, s), F32),
                        pltpu.VMEM((s, LANES), F32), pltpu.VMEM((8, s), F32), pltpu.VMEM((MLSTM_V, s), F32)],
        compiler_params=_cparams("arbitrary"),
        name="mlstm",
    )(zm, mkt, mvt, mgt, conv_w, conv_b.reshape(1, -1), cw_col, cb_col, bg_row, bg_col, norm_g.reshape(1, -1))


def _merge_kernel(x_ref, g_ref, y0_ref, y1_ref, y2_ref, y3_ref, wg_ref, wb_ref, wo_ref, o_ref):
    x = x_ref[...]
    d = x.shape[1]
    h = _rms_rows(x, g_ref[...]).astype(BF16)
    merged = None
    for n, y_ref in enumerate((y0_ref, y1_ref, y2_ref, y3_ref)):
        gate = _sigmoid(_dot_nt(h, wg_ref[n * d:(n + 1) * d, :]))
        term = gate * _dot(y_ref[...], wb_ref[n])
        merged = term if merged is None else merged + term
    o_ref[...] = x + _dot(merged.astype(BF16), wo_ref[...])


def _merge(x2, g, ys, w_gate, w_branch, w_out, tm):
    t, d = x2.shape
    row = lambda w: pl.BlockSpec((tm, w), lambda i: (i, 0))
    return pl.pallas_call(
        _merge_kernel,
        out_shape=jax.ShapeDtypeStruct((t, d), F32),
        grid=(t // tm,),
        in_specs=[row(d), _full((1, d))] + [row(BRANCH_WIDTH)] * N_BRANCH
                 + [_full(w_gate.shape), _full(w_branch.shape), _full(w_out.shape)],
        out_specs=row(d),
        compiler_params=_cparams("arbitrary"),
        name="merge",
    )(x2, g.reshape(1, d), *ys, w_gate, w_branch, w_out)


def _ffn_kernel(x_ref, g_ref, w1_ref, w3_ref, w2_ref, o_ref):
    x = x_ref[...]
    h = _rms_rows(x, g_ref[...]).astype(BF16)
    u = _silu(_dot(h, w1_ref[...])) * _dot(h, w3_ref[...])
    o_ref[...] = x + _dot(u.astype(BF16), w2_ref[...])


def _ffn(x2, g, w1, w3, w2, tm):
    t, d = x2.shape
    row = pl.BlockSpec((tm, d), lambda i: (i, 0))
    return pl.pallas_call(
        _ffn_kernel,
        out_shape=jax.ShapeDtypeStruct((t, d), F32),
        grid=(t // tm,),
        in_specs=[row, _full((1, d)), _full(w1.shape), _full(w3.shape), _full(w2.shape)],
        out_specs=row,
        compiler_params=_cparams("arbitrary"),
        name="ffn",
    )(x2, g.reshape(1, d), w1, w3, w2)


MOE_CAP = 160


def _moe_router_kernel(x_ref, g_ref, wr_ref, h_ref, rank_ref, gate_ref, rrow_ref, cnt_ref):
    tm = x_ref.shape[0]
    h = _rms_rows(x_ref[...], g_ref[...])
    h_ref[...] = h.astype(BF16)
    logits = _dot_hi(h, wr_ref[...])
    lane = lax.broadcasted_iota(jnp.int32, logits.shape, 1)
    logits = jnp.where(lane < N_EXPERTS, logits, -jnp.inf)
    v1 = jnp.max(logits, axis=-1, keepdims=True)
    i1 = jnp.min(jnp.where(logits == v1, lane, LANES), axis=-1, keepdims=True)
    rest = jnp.where(lane == i1, -jnp.inf, logits)
    v2 = jnp.max(rest, axis=-1, keepdims=True)
    i2 = jnp.min(jnp.where(rest == v2, lane, LANES), axis=-1, keepdims=True)
    e2 = jnp.exp(v2 - v1)
    gate_ref[...] = jnp.where(lane == i1, 1.0 / (1.0 + e2), 0.0) + jnp.where(lane == i2, e2 / (1.0 + e2), 0.0)
    routed = jnp.where((lane == i1) | (lane == i2), 1.0, 0.0)
    r_i = lax.broadcasted_iota(jnp.int32, (tm, tm), 0)
    c_i = lax.broadcasted_iota(jnp.int32, (tm, tm), 1)
    rank = _dot(jnp.where(c_i < r_i, 1.0, 0.0).astype(BF16), routed.astype(BF16))
    rank = jnp.where(routed > 0.0, rank, -1.0)
    rank_ref[...] = rank
    eye = jnp.where(lax.broadcasted_iota(jnp.int32, (LANES, LANES), 0)
                    == lax.broadcasted_iota(jnp.int32, (LANES, LANES), 1), 1.0, 0.0)
    rrow_ref[0] = _dot_nt_hi(eye, rank)[0:N_EXPERTS, :]
    cnt_ref[0] = jnp.broadcast_to(jnp.sum(routed, axis=0, keepdims=True), (8, LANES))


def _moe_expert_kernel(cnt_ref, y_ref, h_ref, rrow_ref, rank_ref, gate_ref, w1_ref, w3_ref, w2_ref, ng_ref,
                       o_ref, acc_s, *, norm_out):
    e = pl.program_id(0)
    i = pl.program_id(1)
    tm = h_ref.shape[0]
    cap = MOE_CAP
    lane = lax.broadcasted_iota(jnp.int32, (tm, LANES), 1)
    rank_col = jnp.sum(jnp.where(lane == e, rank_ref[...], 0.0), axis=-1, keepdims=True)
    gate_col = jnp.sum(jnp.where(lane == e, gate_ref[...], 0.0), axis=-1, keepdims=True)
    rank_row = rrow_ref[0, pl.ds(e, 1), :]
    slot_r = lax.broadcasted_iota(jnp.int32, (cap, tm), 0).astype(F32)
    slot_c = lax.broadcasted_iota(jnp.int32, (tm, cap), 1).astype(F32)
    acc_s[...] = jnp.zeros_like(acc_s)

    def one_pass(p, carry):
        base = (p * cap).astype(F32)
        gather = jnp.where(rank_row - base == slot_r, 1.0, 0.0).astype(BF16)
        xc = _dot(gather, h_ref[...]).astype(BF16)
        u = _silu(_dot(xc, w1_ref[0])) * _dot(xc, w3_ref[0])
        yc_hi, yc_lo = _split_bf16(_dot(u.astype(BF16), w2_ref[0]))
        scatter = jnp.where(rank_col - base == slot_c, 1.0, 0.0).astype(BF16)
        acc_s[...] += _dot(scatter, yc_hi) + _dot(scatter, yc_lo)
        return carry

    n_pass = (cnt_ref[i, e] + cap - 1) // cap
    lax.fori_loop(0, n_pass, one_pass, 0)
    out = y_ref[...] + gate_col * acc_s[...]
    if norm_out:
        is_last = e == pl.num_programs(0) - 1

        @pl.when(is_last)
        def _():
            o_ref[...] = _rms_rows(out, ng_ref[...])

        @pl.when(jnp.logical_not(is_last))
        def _():
            o_ref[...] = out
    else:
        o_ref[...] = out


def _moe(x2, g, router, w1, w3, w2, tm, out_norm_g=None):
    t, d = x2.shape
    n_e, _, f = w1.shape
    n_t = t // tm
    router_p = jnp.pad(router, ((0, 0), (0, LANES - n_e)))
    row = lambda w: pl.BlockSpec((tm, w), lambda i: (i, 0))
    h, rank, gate, rrow, cnt = pl.pallas_call(
        _moe_router_kernel,
        out_shape=[jax.ShapeDtypeStruct((t, d), BF16), jax.ShapeDtypeStruct((t, LANES), F32),
                   jax.ShapeDtypeStruct((t, LANES), F32), jax.ShapeDtypeStruct((n_t, N_EXPERTS, tm), F32),
                   jax.ShapeDtypeStruct((n_t, 8, LANES), F32)],
        grid=(n_t,),
        in_specs=[row(d), _full((1, d)), _full((d, LANES))],
        out_specs=[row(d), row(LANES), row(LANES), pl.BlockSpec((1, N_EXPERTS, tm), lambda i: (i, 0, 0)),
                   pl.BlockSpec((1, 8, LANES), lambda i: (i, 0, 0))],
        compiler_params=_cparams("arbitrary"),
        name="moe_router",
    )(x2, g.reshape(1, d), router_p)
    counts = cnt[:, 0, :n_e].astype(jnp.int32)
    tile = lambda w: pl.BlockSpec((tm, w), lambda e, i, c: (i, 0))
    weight = lambda r, c_: pl.BlockSpec((1, r, c_), lambda e, i, c: (e, 0, 0), pipeline_mode=pl.Buffered(1))
    norm_out = out_norm_g is not None
    norm_g = out_norm_g.reshape(1, d) if norm_out else jnp.ones((1, d), F32)
    return pl.pallas_call(
        functools.partial(_moe_expert_kernel, norm_out=norm_out),
        out_shape=jax.ShapeDtypeStruct((t, d), F32),
        grid_spec=pltpu.PrefetchScalarGridSpec(
            num_scalar_prefetch=1,
            grid=(n_e, n_t),
            in_specs=[tile(d), tile(d), pl.BlockSpec((1, N_EXPERTS, tm), lambda e, i, c: (i, 0, 0)),
                      tile(LANES), tile(LANES), weight(d, f), weight(d, f), weight(f, d),
                      pl.BlockSpec((1, d), lambda e, i, c: (0, 0))],
            out_specs=tile(d),
            scratch_shapes=[pltpu.VMEM((tm, d), F32)]),
        input_output_aliases={1: 0},
        compiler_params=_cparams("arbitrary", "arbitrary"),
        name="moe_experts",
    )(counts, x2, h, rrow, rank, gate, w1, w3, w2, norm_g)


def _final_norm_kernel(x_ref, g_ref, o_ref):
    o_ref[...] = _rms_rows(x_ref[...], g_ref[...])


def _final_norm(x2, g, tm):
    t, d = x2.shape
    row = pl.BlockSpec((tm, d), lambda i: (i, 0))
    return pl.pallas_call(
        _final_norm_kernel,
        out_shape=jax.ShapeDtypeStruct((t, d), F32),
        grid=(t // tm,),
        in_specs=[row, _full((1, d))],
        out_specs=row,
        compiler_params=_cparams("arbitrary"),
        name="final_norm",
    )(x2, g.reshape(1, d))


def kernel(x, norm_mix_g, w_in, dsa_g_cq, dsa_g_ckv, dsa_g_kidx, dsa_w_uq, dsa_w_qidx, dsa_w_uv, rwkv_mu, rwkv_w0, rwkv_w_up, rwkv_a0, rwkv_a_up, rwkv_g_up, rwkv_k_k, rwkv_k_a, rwkv_r_k, rwkv_ln_g, rwkv_ln_b, mlstm_conv_w, mlstm_conv_b, mlstm_b_i, mlstm_b_f, mlstm_norm_g, fox_b_f, w_branch, w_out, norm_ffn_g, ffn_w1, ffn_w3, ffn_w2, moe_router, moe_w1, moe_w3, moe_w2, final_norm_g):
    b, s, d = x.shape
    depth = w_in.shape[0]
    t = b * s
    tm = min(512, s)
    for l in range(depth):
        wts = _inproj_weights(w_in[l])
        (zr, zm, fq, fk, fv, ckv, ki, fcol, cqt, ckvt, wit, frow, mkt, mvt, mgt) = _inproj(
            x, norm_mix_g[l], wts, dsa_g_cq[l], dsa_g_ckv[l], dsa_g_kidx[l], fox_b_f[l], tm)
        y_fox = _fox(fq, fk, fv, fcol, frow, min(1024, s))
        y_dsa = _dsa(cqt, wit, ki, ckv, ckvt, dsa_w_uq[l], dsa_w_qidx[l], dsa_w_uv[l], min(256, s // 2)
                     ).reshape(t, BRANCH_WIDTH)
        y_rwkv = _rwkv(zr, rwkv_mu[l], rwkv_w0[l], rwkv_w_up[l], rwkv_a0[l], rwkv_a_up[l], rwkv_g_up[l], rwkv_k_k[l],
                       rwkv_k_a[l], rwkv_r_k[l], rwkv_ln_g[l], rwkv_ln_b[l], min(16, b), min(4, b), 32
                       ).reshape(t, BRANCH_WIDTH)
        y_mlstm = _mlstm(zm, mkt, mvt, mgt, mlstm_conv_w[l], mlstm_conv_b[l], mlstm_b_i[l], mlstm_b_f[l],
                         mlstm_norm_g[l]).reshape(t, BRANCH_WIDTH)
        ys = (y_dsa, y_rwkv, y_mlstm, y_fox.reshape(t, BRANCH_WIDTH))
        x2 = _merge(x.reshape(t, d), norm_mix_g[l], ys, wts['w_gate'], w_branch[l].astype(BF16),
                    w_out[l].astype(BF16), tm)
        j = l // 2
        if l % 2 == 0:
            x2 = _ffn(x2, norm_ffn_g[l], ffn_w1[j].astype(BF16), ffn_w3[j].astype(BF16), ffn_w2[j].astype(BF16), tm)
        else:
            x2 = _moe(x2, norm_ffn_g[l], moe_router[j], moe_w1[j].astype(BF16), moe_w3[j].astype(BF16),
                      moe_w2[j].astype(BF16), tm, final_norm_g if l == depth - 1 else None)
        x = x2.reshape(b, s, d)
    if depth > 0 and depth % 2 == 0:
        return x
    return _final_norm(x.reshape(t, d), final_norm_g, tm).reshape(b, s, d)
```

```python
import functools

import jax
import jax.numpy as jnp
from jax import lax
from jax.experimental import pallas as pl
from jax.experimental.pallas import tpu as pltpu

F32 = jnp.float32
BF16 = jnp.bfloat16
HIGHEST = lax.Precision.HIGHEST

EPS = 1e-6
NEG_BIG = -1e30

N_BRANCH = 4
BRANCH_WIDTH = 256
DSA_HEADS = 4
DSA_HEAD_DIM = 64
DSA_Q_LATENT = 128
DSA_KV_LATENT = 128
IDX_HEADS = 8
IDX_DIM = 32
TOPK_MAX = 256
DSA_Q_BLOCK = 128

RWKV_HEADS = 4
RWKV_HEAD_DIM = 64
RWKV_WIDTH = RWKV_HEADS * RWKV_HEAD_DIM
RWKV_LORA_W = 64
RWKV_LORA_A = 64
RWKV_LORA_G = 128
RWKV_GN_EPS = 64e-5
RWKV_IN = 3 * RWKV_WIDTH + RWKV_LORA_W + RWKV_LORA_A + RWKV_LORA_G

MLSTM_HEADS = 4
MLSTM_QK_DIM = 32
MLSTM_V_DIM = 64
MLSTM_CHUNK = 256
CONV_WIDTH = 4
GATE_SOFTCAP = 15.0
MLSTM_QK = MLSTM_HEADS * MLSTM_QK_DIM
MLSTM_V = MLSTM_HEADS * MLSTM_V_DIM

FOX_HEADS = 4
FOX_HEAD_DIM = 64
FOX_WIDTH = FOX_HEADS * FOX_HEAD_DIM

N_EXPERTS = 8

VMEM_LIMIT_BYTES = 56 * 1024 * 1024
LANES = 128


def _cparams(*sem):
    return pltpu.CompilerParams(dimension_semantics=sem, vmem_limit_bytes=VMEM_LIMIT_BYTES)


def _dot(a, b):
    return jnp.dot(a, b, preferred_element_type=F32)


def _dot_hi(a, b):
    return jnp.dot(a, b, preferred_element_type=F32, precision=HIGHEST)


def _dot_nt(a, b):
    return lax.dot_general(a, b, (((1,), (1,)), ((), ())), preferred_element_type=F32)


def _dot_nt_hi(a, b):
    return lax.dot_general(a, b, (((1,), (1,)), ((), ())), preferred_element_type=F32, precision=HIGHEST)


def _log_sigmoid(t):
    return jnp.minimum(t, 0.0) - jnp.log1p(jnp.exp(-jnp.abs(t)))


def _sigmoid(t):
    return 1.0 / (1.0 + jnp.exp(-t))


def _silu(t):
    return t * _sigmoid(t)


def _rms_rows(t, g_row):
    return t * lax.rsqrt(jnp.mean(t * t, axis=-1, keepdims=True) + EPS) * g_row


def _rms_cols(t, g_col):
    return t * lax.rsqrt(jnp.mean(t * t, axis=0, keepdims=True) + EPS) * g_col


def _full(shape):
    n = len(shape)
    return pl.BlockSpec(shape, lambda *_: (0,) * n)


T_CQ = 0
T_CKV = T_CQ + DSA_Q_LATENT
T_WIDX = T_CKV + DSA_KV_LATENT
T_FOXF = T_WIDX + IDX_HEADS
T_MK = T_FOXF + 8
T_MV = T_MK + MLSTM_QK
T_MG = T_MV + MLSTM_V
T_ROWS = T_MG + 8
D_COLS = 3 * LANES
M_COLS = 2 * MLSTM_QK + 2 * MLSTM_V + LANES


def _inproj_kernel(x_ref, g_ref, wr_ref, wm_ref, wf_ref, wd_ref, wt_ref,
                   gcq_ref, gckv_col_ref, gckv_row_ref, gki_ref, bf_row_ref, bf_col_ref,
                   zr_ref, zm_ref, q_ref, k_ref, v_ref, ckv_ref, ki_ref, fcol_ref,
                   cqt_ref, ckvt_ref, wit_ref, frow_ref, mkt_ref, mvt_ref, mgt_ref,
                   carry_row, carry_col):
    j = pl.program_id(1)
    tm = x_ref.shape[1]

    @pl.when(j == 0)
    def _():
        carry_row[...] = jnp.zeros_like(carry_row)
        carry_col[...] = jnp.zeros_like(carry_col)

    x = x_ref[0]
    h = _rms_rows(x, g_ref[...]).astype(BF16)

    zr_ref[0] = _dot_nt(h, wr_ref[...])
    zm_ref[0] = _dot_nt(h, wm_ref[...])

    zf = _dot_nt(h, wf_ref[...])
    for hd in range(FOX_HEADS):
        lo = hd * FOX_HEAD_DIM
        q_ref[0, hd] = (zf[:, lo:lo + FOX_HEAD_DIM] * FOX_HEAD_DIM ** -0.5).astype(BF16)
        k_ref[0, hd] = zf[:, FOX_WIDTH + lo:FOX_WIDTH + lo + FOX_HEAD_DIM].astype(BF16)
        v_ref[0, hd] = zf[:, 2 * FOX_WIDTH + lo:2 * FOX_WIDTH + lo + FOX_HEAD_DIM].astype(BF16)

    zd = _dot_nt(h, wd_ref[...])
    ckv_ref[0] = _rms_rows(zd[:, :LANES], gckv_row_ref[...]).astype(BF16)
    ki_pieces = [p.astype(F32) for p in _split3_bf16(_rms_rows(zd[:, LANES:LANES + IDX_DIM], gki_ref[...]))]
    ki_pad = jnp.zeros((tm, IDX_K_COLS - len(IDX_SPLIT_K) * IDX_DIM), F32)
    ki_ref[0] = jnp.concatenate([ki_pieces[kp] for kp, _ in IDX_SPLIT_K] + [ki_pad], axis=1).astype(BF16)
    lf_col = _log_sigmoid(zd[:, 2 * LANES:] + bf_row_ref[...])
    r_i = lax.broadcasted_iota(jnp.int32, (tm, tm), 0)
    c_i = lax.broadcasted_iota(jnp.int32, (tm, tm), 1)
    tri = jnp.where(c_i <= r_i, 1.0, 0.0).astype(BF16)
    cum_col = sum(_dot(tri, p) for p in _split3_bf16(lf_col)) + carry_row[...]
    fcol_ref[0] = cum_col
    carry_row[...] = cum_col[tm - 1:tm, :]

    zt = _dot_nt(wt_ref[...], h)
    cqt_ref[0] = _rms_cols(zt[T_CQ:T_CKV], gcq_ref[...])
    ckvt_ref[0] = _rms_cols(zt[T_CKV:T_WIDX], gckv_col_ref[...]).astype(BF16)
    wit_ref[0] = zt[T_WIDX:T_FOXF] * IDX_HEADS ** -0.5
    lf_row = _log_sigmoid(zt[T_FOXF:T_MK] + bf_col_ref[...])
    cum_row = sum(_dot_nt(p, tri) for p in _split3_bf16(lf_row)) + carry_col[:, 0:1]
    frow_ref[0] = cum_row
    carry_col[...] = jnp.broadcast_to(cum_row[:, tm - 1:tm], carry_col.shape)
    mkt_ref[0] = zt[T_MK:T_MV]
    mvt_ref[0] = zt[T_MV:T_MG]
    mgt_ref[0] = zt[T_MG:T_ROWS]


def _inproj_weights(w_in_l):
    wt = w_in_l.T
    o = 0
    w_cq = wt[o:o + DSA_Q_LATENT]; o += DSA_Q_LATENT
    w_ckv = wt[o:o + DSA_KV_LATENT]; o += DSA_KV_LATENT
    w_kidx = wt[o:o + IDX_DIM]; o += IDX_DIM
    w_widx = wt[o:o + IDX_HEADS]; o += IDX_HEADS
    w_rwkv = wt[o:o + RWKV_IN]; o += RWKV_IN
    w_mq = wt[o:o + MLSTM_QK]; o += MLSTM_QK
    w_mk = wt[o:o + MLSTM_QK]; o += MLSTM_QK
    w_mv = wt[o:o + MLSTM_V]; o += MLSTM_V
    w_mo = wt[o:o + MLSTM_V]; o += MLSTM_V
    w_mi = wt[o:o + MLSTM_HEADS]; o += MLSTM_HEADS
    w_mf = wt[o:o + MLSTM_HEADS]; o += MLSTM_HEADS
    w_fox = wt[o:o + 3 * FOX_WIDTH]; o += 3 * FOX_WIDTH
    w_ff = wt[o:o + FOX_HEADS]; o += FOX_HEADS
    w_gate = wt[o:]

    def padr(w, n):
        return jnp.pad(w, ((0, n - w.shape[0]), (0, 0)))

    w_d = jnp.concatenate([w_ckv, padr(w_kidx, LANES), padr(w_ff, LANES)], axis=0)
    w_m = jnp.concatenate([w_mq, w_mk, w_mv, w_mo, padr(jnp.concatenate([w_mi, w_mf], axis=0), LANES)], axis=0)
    w_t = jnp.concatenate([w_cq, w_ckv, w_widx, padr(w_ff, 8), w_mk, w_mv, w_mi, w_mf], axis=0)
    assert w_t.shape[0] == T_ROWS
    cast = lambda w: w.astype(BF16)
    return dict(w_r=cast(w_rwkv), w_m=cast(w_m), w_f=cast(w_fox), w_d=cast(w_d), w_t=cast(w_t), w_gate=cast(w_gate))


def _inproj(x, g, wts, g_cq, g_ckv, g_kidx, fox_b_f, tm):
    b, s, d = x.shape
    nj = s // tm
    bf_row = jnp.pad(fox_b_f, (0, LANES - FOX_HEADS)).reshape(1, LANES)
    bf_col = jnp.pad(fox_b_f, (0, 8 - FOX_HEADS)).reshape(8, 1)
    row = lambda w: pl.BlockSpec((1, tm, w), lambda i, j: (i, j, 0))
    head = pl.BlockSpec((1, FOX_HEADS, tm, FOX_HEAD_DIM), lambda i, j: (i, 0, j, 0))
    col = lambda r: pl.BlockSpec((1, r, tm), lambda i, j: (i, 0, j))
    out_shape = [
        jax.ShapeDtypeStruct((b, s, RWKV_IN), F32),
        jax.ShapeDtypeStruct((b, s, M_COLS), F32),
        jax.ShapeDtypeStruct((b, FOX_HEADS, s, FOX_HEAD_DIM), BF16),
        jax.ShapeDtypeStruct((b, FOX_HEADS, s, FOX_HEAD_DIM), BF16),
        jax.ShapeDtypeStruct((b, FOX_HEADS, s, FOX_HEAD_DIM), BF16),
        jax.ShapeDtypeStruct((b, s, DSA_KV_LATENT), BF16),
        jax.ShapeDtypeStruct((b, s, IDX_K_COLS), BF16),
        jax.ShapeDtypeStruct((b, s, LANES), F32),
        jax.ShapeDtypeStruct((b, DSA_Q_LATENT, s), F32),
        jax.ShapeDtypeStruct((b, DSA_KV_LATENT, s), BF16),
        jax.ShapeDtypeStruct((b, IDX_HEADS, s), F32),
        jax.ShapeDtypeStruct((b, 8, s), F32),
        jax.ShapeDtypeStruct((b, MLSTM_QK, s), F32),
        jax.ShapeDtypeStruct((b, MLSTM_V, s), F32),
        jax.ShapeDtypeStruct((b, 8, s), F32),
    ]
    out_specs = [row(RWKV_IN), row(M_COLS), head, head, head, row(DSA_KV_LATENT), row(IDX_K_COLS), row(LANES),
                 col(DSA_Q_LATENT), col(DSA_KV_LATENT), col(IDX_HEADS), col(8), col(MLSTM_QK), col(MLSTM_V), col(8)]
    in_specs = [row(d), _full((1, d)), _full(wts['w_r'].shape), _full(wts['w_m'].shape), _full(wts['w_f'].shape),
                _full(wts['w_d'].shape), _full(wts['w_t'].shape),
                _full((DSA_Q_LATENT, 1)), _full((DSA_KV_LATENT, 1)), _full((1, DSA_KV_LATENT)), _full((1, IDX_DIM)),
                _full((1, LANES)), _full((8, 1))]
    return pl.pallas_call(
        _inproj_kernel,
        out_shape=out_shape,
        grid=(b, nj),
        in_specs=in_specs,
        out_specs=out_specs,
        scratch_shapes=[pltpu.VMEM((1, LANES), F32), pltpu.VMEM((8, LANES), F32)],
        compiler_params=_cparams("arbitrary", "arbitrary"),
        name="inproj",
    )(x, g.reshape(1, d), wts['w_r'], wts['w_m'], wts['w_f'], wts['w_d'], wts['w_t'],
      g_cq.reshape(-1, 1), g_ckv.reshape(-1, 1), g_ckv.reshape(1, -1), g_kidx.reshape(1, -1), bf_row, bf_col)


def _fox_kernel(q_ref, k_ref, v_ref, fcol_ref, frow_ref, o_ref):
    qi = pl.program_id(1)
    tq = q_ref.shape[2]
    outs = []
    for hd in range(FOX_HEADS):
        q = q_ref[0, hd]
        fq = fcol_ref[0][:, hd:hd + 1]

        def scores(j):
            start = pl.multiple_of(j * tq, tq)
            k = k_ref[0, hd, pl.ds(start, tq), :]
            v = v_ref[0, hd, pl.ds(start, tq), :]
            fk = frow_ref[0, hd:hd + 1, pl.ds(start, tq)]
            return _dot_nt(q, k) + (fq - fk), v

        def update(carry, sc, v):
            m, l, acc = carry
            m_new = jnp.maximum(m, jnp.max(sc, axis=-1, keepdims=True))
            alpha = jnp.exp(m - m_new)
            p = jnp.exp(sc - m_new)
            l = alpha * l + jnp.sum(p, axis=-1, keepdims=True)
            acc = alpha * acc + _dot(p.astype(BF16), v)
            return m_new, l, acc

        def body(j, carry):
            sc, v = scores(j)
            return update(carry, sc, v)

        init = (jnp.full((tq, 1), NEG_BIG, F32), jnp.zeros((tq, 1), F32), jnp.zeros((tq, FOX_HEAD_DIM), F32))
        carry = lax.fori_loop(0, qi, body, init)
        sc, v = scores(qi)
        r_i = lax.broadcasted_iota(jnp.int32, (tq, tq), 0)
        c_i = lax.broadcasted_iota(jnp.int32, (tq, tq), 1)
        sc = jnp.where(c_i <= r_i, sc, NEG_BIG)
        m, l, acc = update(carry, sc, v)
        outs.append(acc / l)
    o_ref[0] = jnp.concatenate(outs, axis=-1).astype(o_ref.dtype)


def _fox(q, k, v, fcol, frow, tq):
    b, nh, s, d = q.shape
    return pl.pallas_call(
        _fox_kernel,
        out_shape=jax.ShapeDtypeStruct((b, s, nh * d), BF16),
        grid=(b, s // tq),
        in_specs=[
            pl.BlockSpec((1, nh, tq, d), lambda i, j: (i, 0, j, 0)),
            pl.BlockSpec((1, nh, s, d), lambda i, j: (i, 0, 0, 0)),
            pl.BlockSpec((1, nh, s, d), lambda i, j: (i, 0, 0, 0)),
            pl.BlockSpec((1, tq, LANES), lambda i, j: (i, j, 0)),
            pl.BlockSpec((1, 8, s), lambda i, j: (i, 0, 0)),
        ],
        out_specs=pl.BlockSpec((1, tq, nh * d), lambda i, j: (i, j, 0)),
        compiler_params=_cparams("arbitrary", "arbitrary"),
        name="fox",
    )(q, k, v, fcol, frow)


INT_MIN = -2 ** 31


IDX_SPLIT_K = ((0, 0), (0, 1), (1, 0), (0, 2), (1, 1), (2, 0))
IDX_K_COLS = 2 * LANES


def _split3_bf16(t):
    p0 = t.astype(BF16)
    r1 = t - p0.astype(F32)
    p1 = r1.astype(BF16)
    p2 = (r1 - p1.astype(F32)).astype(BF16)
    return p0, p1, p2


def _tree_sum(parts):
    while len(parts) > 1:
        parts = [parts[i] + parts[i + 1] for i in range(0, len(parts) - 1, 2)] + (parts[-1:] if len(parts) % 2 else [])
    return parts[0]


def _dsa_kernel(cqt_ref, wit_ref, ki6_ref, ckv_ref, ckvt_ref, wuqt_ref, wqit_ref, wuvt_ref, o_ref, key_s, *, ck):
    qb = pl.program_id(1)
    s = ki6_ref.shape[1]
    nq = cqt_ref.shape[2]
    topk = min(TOPK_MAX, s // 4)
    n_ck = (qb * nq + nq + ck - 1) // ck
    lat = DSA_KV_LATENT

    cq = cqt_ref[0]
    qi_t = _dot_hi(wqit_ref[...], cq)
    zeros_pad = jnp.zeros((IDX_K_COLS - len(IDX_SPLIT_K) * IDX_DIM, nq), BF16)
    q_blocks = []
    for hd in range(IDX_HEADS):
        pieces = _split3_bf16(qi_t[hd * IDX_DIM:(hd + 1) * IDX_DIM, :])
        q_blocks.append(jnp.concatenate([pieces[qp] for _, qp in IDX_SPLIT_K] + [zeros_pad], axis=0))
    q6 = jnp.concatenate(q_blocks, axis=1)
    wi = wit_ref[0] * IDX_DIM ** -0.5
    tpos = qb * nq + lax.broadcasted_iota(jnp.int32, (ck, nq), 1)
    row = lax.broadcasted_iota(jnp.int32, (ck, nq), 0)

    def chunk_start(c):
        return pl.multiple_of(c * ck, ck)

    def score_chunk(c, carry):
        c0 = chunk_start(c)
        ki6 = ki6_ref[0, pl.ds(c0, ck), :]
        score = None
        for hp in range(0, IDX_HEADS, 2):
            dots = _dot(ki6, q6[:, hp * nq:(hp + 2) * nq])
            for i in range(2):
                term = wi[hp + i:hp + i + 1, :] * jnp.maximum(dots[:, i * nq:(i + 1) * nq], 0.0)
                score = term if score is None else score + term
        bits = pltpu.bitcast(score, jnp.int32)
        key = jnp.where(bits < 0, bits ^ 0x7FFFFFFF, bits)
        key = jnp.where(score == 0.0, 0, key)
        key_s[pl.ds(c0, ck), :] = jnp.where(c0 + row <= tpos, key, INT_MIN)
        return carry

    n_pair = (n_ck + 1) // 2
    lax.fori_loop(0, n_pair, lambda j, carry: score_chunk(2 * j + 1, score_chunk(2 * j, carry)), 0)

    def count(mask_fn):
        def body(j, acc):
            parts = []
            for c in (2 * j, 2 * j + 1):
                c0 = chunk_start(c)
                ones = jnp.where(mask_fn(key_s[pl.ds(c0, ck), :], c0 + row), 1, 0)
                parts += [ones[i * 8:(i + 1) * 8] for i in range(ck // 8)]
            return acc + _tree_sum(parts)
        acc = lax.fori_loop(0, n_pair, body, jnp.zeros((8, nq), jnp.int32))
        return jnp.sum(acc, axis=0, keepdims=True)

    def value_bit(i, carry):
        lo, n_lo = carry
        cand = lo + jnp.left_shift(jnp.int32(1), 31 - i)
        n_cand = count(lambda key, pos: key >= cand)
        ok = n_cand >= topk
        return jnp.where(ok, cand, lo), jnp.where(ok, n_cand, n_lo)

    thr, n_ge = lax.fori_loop(0, 32, value_bit, (jnp.full((1, nq), INT_MIN, jnp.int32),
                                                 jnp.full((1, nq), s, jnp.int32)))

    n_bits = s.bit_length()

    def tie_search():
        need = topk - count(lambda key, pos: key > thr)

        def index_bit(i, lo):
            cand = lo + jnp.left_shift(jnp.int32(1), n_bits - 1 - i)
            return jnp.where(count(lambda key, pos: (key == thr) & (pos < cand)) < need, cand, lo)

        return lax.fori_loop(0, n_bits, index_bit, jnp.zeros((1, nq), jnp.int32))

    surplus = jnp.max(jnp.where((n_ge > topk) & (thr > INT_MIN), 1, 0))
    last = lax.cond(surplus > 0, tie_search, lambda: jnp.full((1, nq), s, jnp.int32))

    q_t = (_dot(wuqt_ref[...], cq.astype(BF16)) * lat ** -0.5).astype(BF16)

    def attend(c, carry):
        c0 = chunk_start(c)
        key = key_s[pl.ds(c0, ck), :]
        pos = c0 + row
        sel = ((key > thr) | ((key == thr) & (pos <= last))) & (pos <= tpos)
        bias = jnp.where(sel, 0.0, NEG_BIG)
        ckv = ckv_ref[0, pl.ds(c0, ck), :]
        ckvt = ckvt_ref[0, :, pl.ds(c0, ck)]
        new = []
        for hd in range(DSA_HEADS):
            m, l, acc = carry[hd]
            lg = _dot(ckv, q_t[hd * lat:(hd + 1) * lat, :]) + bias
            m_new = jnp.maximum(m, jnp.max(lg, axis=0, keepdims=True))
            alpha = jnp.exp(m - m_new)
            p = jnp.exp(lg - m_new)
            new.append((m_new, alpha * l + jnp.sum(p, axis=0, keepdims=True),
                        alpha * acc + _dot(ckvt, p.astype(BF16))))
        return tuple(new)

    init = tuple((jnp.full((1, nq), NEG_BIG, F32), jnp.zeros((1, nq), F32), jnp.zeros((lat, nq), F32))
                 for _ in range(DSA_HEADS))
    final = lax.fori_loop(0, n_pair, lambda j, carry: attend(2 * j + 1, attend(2 * j, carry)), init)
    outs = [_dot(wuvt_ref[hd], (acc / l).astype(BF16)) for hd, (_, l, acc) in enumerate(final)]
    o_ref[0] = jnp.concatenate(outs, axis=0).T.astype(o_ref.dtype)


def _dsa(cqt, wit, ki6, ckv, ckvt, w_uq, w_qidx, w_uv, ck):
    b, c, s = cqt.shape
    nq = DSA_Q_BLOCK
    wuqt = w_uq.reshape(c, -1).T.astype(BF16)
    wqit = w_qidx.reshape(c, -1).T
    wuvt = jnp.transpose(w_uv, (0, 2, 1)).astype(BF16)
    return pl.pallas_call(
        functools.partial(_dsa_kernel, ck=ck),
        out_shape=jax.ShapeDtypeStruct((b, s, DSA_HEADS * DSA_HEAD_DIM), BF16),
        grid=(b, s // nq),
        in_specs=[pl.BlockSpec((1, c, nq), lambda i, j: (i, 0, j)),
                  pl.BlockSpec((1, IDX_HEADS, nq), lambda i, j: (i, 0, j)),
                  pl.BlockSpec((1, s, IDX_K_COLS), lambda i, j: (i, 0, 0)),
                  pl.BlockSpec((1, s, DSA_KV_LATENT), lambda i, j: (i, 0, 0)),
                  pl.BlockSpec((1, DSA_KV_LATENT, s), lambda i, j: (i, 0, 0)),
                  _full(wuqt.shape), _full(wqit.shape), _full(wuvt.shape)],
        out_specs=pl.BlockSpec((1, nq, DSA_HEADS * DSA_HEAD_DIM), lambda i, j: (i, j, 0)),
        scratch_shapes=[pltpu.VMEM((s, nq), jnp.int32)],
        compiler_params=_cparams("arbitrary", "arbitrary"),
        name="dsa",
    )(cqt, wit, ki6, ckv, ckvt, wuqt, wqit, wuvt)


Y_PAD = 8


def _head_ones(n, dtype):
    r_i = lax.broadcasted_iota(jnp.int32, (n, n), 0) // RWKV_HEAD_DIM
    c_i = lax.broadcasted_iota(jnp.int32, (n, n), 1) // RWKV_HEAD_DIM
    return jnp.where(r_i == c_i, 1.0, 0.0).astype(dtype)


def _split_bf16(t):
    hi = t.astype(BF16)
    lo = (t - hi.astype(F32)).astype(BF16)
    return hi, lo


def _rwkv_kernel(z_ref, mu_ref, w0_ref, wup_ref, a0_ref, aup_ref, gup_ref, kk_ref, ka_ref, rk_ref, lng_ref, lnb_ref,
                 o_ref, st_s, prev_s, step_s, y_s, bonus_s, gate_s, *, grp):
    c = pl.program_id(1)
    n_g, tc, _ = z_ref.shape
    hd, wd_ = RWKV_HEAD_DIM, RWKV_WIDTH

    @pl.when(c == 0)
    def _():
        st_s[...] = jnp.zeros_like(st_s)
        prev_s[...] = jnp.zeros_like(prev_s)

    ones_b = _head_ones(wd_, BF16)
    diag = jnp.where(lax.broadcasted_iota(jnp.int32, (hd, wd_), 0)
                     == lax.broadcasted_iota(jnp.int32, (hd, wd_), 1) % hd, 1.0, 0.0).astype(F32)

    def head_sum(t):
        hi, lo = _split_bf16(t)
        return _dot(hi, ones_b) + _dot(lo, ones_b)

    zs = []
    for g in range(n_g):
        z = z_ref[g]
        row = lax.broadcasted_iota(jnp.int32, z.shape, 0)
        z_prev = jnp.where(row == 0, prev_s[g], pltpu.roll(z, 1, 0))
        prev_s[g] = z[tc - 1:tc, :]
        zs.append(z + mu_ref[...] * (z_prev - z))
    z = jnp.concatenate(zs, axis=0)
    r = z[:, 0:wd_]
    k = z[:, wd_:2 * wd_]
    v = z[:, 2 * wd_:3 * wd_]
    o = 3 * wd_
    w_lora = z[:, o:o + RWKV_LORA_W]
    a_lora = z[:, o + RWKV_LORA_W:o + RWKV_LORA_W + RWKV_LORA_A]
    g_lora = z[:, o + RWKV_LORA_W + RWKV_LORA_A:]
    w_log = _log_sigmoid(w0_ref[...] + _dot_hi(jnp.tanh(w_lora), wup_ref[...])) - 0.5
    a = _sigmoid(a0_ref[...] + _dot_hi(a_lora, aup_ref[...]))
    gate_s[...] = _dot_hi(_sigmoid(g_lora), gup_ref[...])
    kk = k * kk_ref[...]
    kk = kk / jnp.maximum(jnp.sqrt(head_sum(kk * kk)), 1e-12)
    k = k * (1.0 + (a - 1.0) * ka_ref[...])
    step_s[0] = jnp.exp(-jnp.exp(w_log))
    step_s[1] = -kk
    step_s[2] = kk * a
    step_s[3] = k
    step_s[4] = r
    step_s[5] = v
    bonus_s[...] = head_sum(r * k * rk_ref[...]) * v

    n_grp = n_g // grp

    def rows(kind, g0, t):
        return jnp.concatenate(
            [jnp.broadcast_to(step_s[kind, pl.ds((g0 + i) * tc + t, 1), :], (hd, wd_)) for i in range(grp)], axis=0)

    def store_y(yb, g0, t_write):
        yb = yb * diag_g
        for i in range(grp):
            tile = jnp.sum(yb[i * hd:(i + 1) * hd].reshape(hd // 8, 8, wd_), axis=0)
            y_s[pl.ds(Y_PAD + (g0 + i) * tc + t_write, 1), :] = jnp.sum(tile, axis=0, keepdims=True)

    def rows_b(kind, g0, t):
        tiles = []
        for i in range(grp):
            r16 = jnp.broadcast_to(step_s[kind, pl.ds((g0 + i) * tc + t, 1), :], (16, wd_)).astype(BF16)
            tiles.append(jnp.broadcast_to(r16[None], (hd // 16, 16, wd_)).reshape(hd, wd_))
        return jnp.concatenate(tiles, axis=0)

    def step(t, carry):
        for q in range(n_grp):
            g0 = q * grp
            st = st_s[q]
            st_b = st.astype(BF16)
            sa = _dot(st_b * rows_b(1, g0, t), ones_b)
            store_y(_dot(st_b * rows_b(4, g0, jnp.maximum(t - 1, 0)), ones_b), g0, t - 1)
            vb = _dot(rows_b(5, g0, t) * diag_b, ones_b)
            st_s[q] = st * rows(0, g0, t) + sa * rows(2, g0, t) + vb * rows(3, g0, t)
        return carry

    diag_g = jnp.concatenate([diag] * grp, axis=0)
    diag_b = diag_g.astype(BF16)
    lax.fori_loop(0, tc, step, 0, unroll=8)
    for q in range(n_grp):
        store_y(_dot(st_s[q].astype(BF16) * rows_b(4, q * grp, tc - 1), ones_b), q * grp, tc - 1)

    y = y_s[pl.ds(Y_PAD, n_g * tc), :]
    mean = head_sum(y) * (1.0 / hd)
    yc = y - mean
    var = head_sum(yc * yc) * (1.0 / hd)
    yn = yc * lax.rsqrt(var + RWKV_GN_EPS) * lng_ref[...] + lnb_ref[...]
    out = ((yn + bonus_s[...]) * gate_s[...]).astype(o_ref.dtype)
    for g in range(n_g):
        o_ref[g] = out[g * tc:(g + 1) * tc]


def _rwkv(zr, mu, w0, w_up, a0, a_up, g_up, k_k, k_a, r_k, ln_g, ln_b, n_g, grp, tc):
    b, s, zin = zr.shape
    wd_ = RWKV_WIDTH
    vec = lambda p: p.reshape(1, -1)
    params = [vec(mu), vec(w0), w_up, vec(a0), a_up, g_up, vec(k_k), vec(k_a), vec(r_k), vec(ln_g), vec(ln_b)]
    return pl.pallas_call(
        functools.partial(_rwkv_kernel, grp=grp),
        out_shape=jax.ShapeDtypeStruct((b, s, wd_), BF16),
        grid=(b // n_g, s // tc),
        in_specs=[pl.BlockSpec((n_g, tc, zin), lambda i, c: (i, c, 0))] + [_full(p.shape) for p in params],
        out_specs=pl.BlockSpec((n_g, tc, wd_), lambda i, c: (i, c, 0)),
        scratch_shapes=[pltpu.VMEM((n_g // grp, grp * RWKV_HEAD_DIM, wd_), F32), pltpu.VMEM((n_g, 1, zin), F32),
                        pltpu.VMEM((6, n_g * tc, wd_), F32), pltpu.VMEM((Y_PAD + n_g * tc, wd_), F32),
                        pltpu.VMEM((n_g * tc, wd_), F32), pltpu.VMEM((n_g * tc, wd_), F32)],
        compiler_params=_cparams("arbitrary", "arbitrary"),
        name="rwkv",
    )(zr, *params)


def _softcap(t):
    return GATE_SOFTCAP * jnp.tanh(t / GATE_SOFTCAP)


def _mlstm_kernel(zm_ref, mkt_ref, mvt_ref, mgt_ref, cw_row_ref, cb_row_ref, cw_col_ref, cb_col_ref,
                  bg_row_ref, bg_col_ref, ng_ref, o_ref, q_s, k_s, kt_s, gc_s, gr_s, yt_s):
    s = zm_ref.shape[1]
    nh, dk, dv, lc = MLSTM_HEADS, MLSTM_QK_DIM, MLSTM_V_DIM, MLSTM_CHUNK
    pair = 2 * lc

    qk = zm_ref[0, :, 0:2 * MLSTM_QK]
    pos_r = lax.broadcasted_iota(jnp.int32, qk.shape, 0)
    acc = cb_row_ref[...] + qk * cw_row_ref[CONV_WIDTH - 1:CONV_WIDTH, :]
    for r in range(1, CONV_WIDTH):
        sh = jnp.where(pos_r >= r, pltpu.roll(qk, r, 0), 0.0)
        acc = acc + sh * cw_row_ref[CONV_WIDTH - 1 - r:CONV_WIDTH - r, :]
    acc = _silu(acc)
    q_s[...] = acc[:, :MLSTM_QK] * dk ** -0.5
    k_s[...] = acc[:, MLSTM_QK:]
    kt = mkt_ref[0]
    pos_c = lax.broadcasted_iota(jnp.int32, kt.shape, 1)
    acc_t = cb_col_ref[...] + kt * cw_col_ref[:, CONV_WIDTH - 1:CONV_WIDTH]
    for r in range(1, CONV_WIDTH):
        sh = jnp.where(pos_c >= r, pltpu.roll(kt, r, 1), 0.0)
        acc_t = acc_t + sh * cw_col_ref[:, CONV_WIDTH - 1 - r:CONV_WIDTH - r]
    kt_s[...] = _silu(acc_t)

    gcol = _softcap(zm_ref[0, :, 2 * MLSTM_QK + 2 * MLSTM_V:] + bg_row_ref[...])
    lane = lax.broadcasted_iota(jnp.int32, gcol.shape, 1)
    gc_s[...] = jnp.where(lane < nh, gcol, _log_sigmoid(gcol))
    grow = _softcap(mgt_ref[0] + bg_col_ref[...])
    sub = lax.broadcasted_iota(jnp.int32, grow.shape, 0)
    gr_s[...] = jnp.where(sub < nh, grow, _log_sigmoid(grow))

    r_i = lax.broadcasted_iota(jnp.int32, (lc, lc), 0)
    c_i = lax.broadcasted_iota(jnp.int32, (lc, lc), 1)
    tri = jnp.where(c_i <= r_i, 1.0, 0.0).astype(F32)
    causal_t = r_i <= c_i
    ones_rows = jnp.ones((8, lc), F32)

    def chunk_pair(p, carry):
        base = pl.multiple_of(p * pair, pair)
        gr_slab = gr_s[:, pl.ds(base, pair)]
        kt_slab = kt_s[:, pl.ds(base, pair)]
        vt_slab = mvt_ref[0, :, pl.ds(base, pair)]
        outs = [[] for _ in range(nh)]
        for sc in range(2):
            r0 = base + sc * lc
            gcc = gc_s[pl.ds(r0, lc), :]
            grc = gr_slab[:, sc * lc:(sc + 1) * lc]
            bcum_col = _dot_hi(tri, gcc)
            bcum_row = _dot_nt_hi(grc, tri)
            qc = q_s[pl.ds(r0, lc), :]
            kc = k_s[pl.ds(r0, lc), :]
            new_carry = []
            for hd in range(nh):
                c_aug, m_prev = carry[hd]
                bc_c = bcum_col[:, nh + hd:nh + hd + 1]
                li_c = gcc[:, hd:hd + 1]
                bc_r = bcum_row[nh + hd:nh + hd + 1, :]
                li_r = grc[hd:hd + 1, :]
                gtot = bc_r[:, lc - 1:lc]
                d_t = jnp.where(causal_t, bc_r - bc_c + li_c, NEG_BIG)
                m_inter = bc_r + m_prev
                m_t = jnp.maximum(m_inter, jnp.max(d_t, axis=0, keepdims=True))
                q_h = qc[:, hd * dk:(hd + 1) * dk].astype(BF16)
                k_h = kc[:, hd * dk:(hd + 1) * dk].astype(BF16)
                kt_h = kt_slab[hd * dk:(hd + 1) * dk, sc * lc:(sc + 1) * lc].astype(BF16)
                vt_h = vt_slab[hd * dv:(hd + 1) * dv, sc * lc:(sc + 1) * lc]
                vt_aug = jnp.concatenate([vt_h, ones_rows], axis=0)
                s_t = _dot_nt(k_h, q_h)
                w_t = jnp.exp(d_t - m_t) * s_t
                s_inter = jnp.exp(m_inter - m_t)
                numden = _dot(vt_aug.astype(BF16), w_t.astype(BF16)) + s_inter * _dot_nt(c_aug.astype(BF16), q_h)
                den = numden[dv:dv + 1, :]
                outs[hd].append(numden[:dv, :] / jnp.maximum(jnp.abs(den), jnp.exp(-m_t)))
                a_log = gtot - bc_r + li_r
                a_max = jnp.max(a_log, axis=-1, keepdims=True)
                a_w = jnp.exp(a_log - a_max)
                kvn = _dot_nt((vt_aug * a_w).astype(BF16), kt_h)
                m_new = jnp.maximum(gtot + m_prev, a_max)
                s_old = jnp.exp(gtot + m_prev - m_new)
                s_new = jnp.exp(a_max - m_new)
                new_carry.append((s_old * c_aug + s_new * kvn, m_new))
            carry = tuple(new_carry)
        for hd in range(nh):
            yt_s[hd * dv:(hd + 1) * dv, pl.ds(base, pair)] = jnp.concatenate(outs[hd], axis=-1)
        return carry

    init = tuple((jnp.zeros((dv + 8, dk), F32), jnp.full((1, 1), NEG_BIG, F32)) for _ in range(nh))
    lax.fori_loop(0, s // pair, chunk_pair, init)

    parts = []
    for hd in range(nh):
        blk = yt_s[hd * dv:(hd + 1) * dv, :]
        parts.append(blk * lax.rsqrt(jnp.mean(blk * blk, axis=0, keepdims=True) + EPS))
    y = jnp.concatenate(parts, axis=0).T
    o_gate = _sigmoid(zm_ref[0, :, 2 * MLSTM_QK + MLSTM_V:2 * MLSTM_QK + 2 * MLSTM_V])
    o_ref[0] = (y * ng_ref[...] * o_gate).astype(o_ref.dtype)


def _mlstm(zm, mkt, mvt, mgt, conv_w, conv_b, b_i, b_f, norm_g):
    b, s, _ = zm.shape
    cw_col = conv_w[:, MLSTM_QK:].T
    cb_col = conv_b[MLSTM_QK:].reshape(-1, 1)
    bg = jnp.concatenate([b_i, b_f])
    bg_row = jnp.pad(bg, (0, LANES - 2 * MLSTM_HEADS)).reshape(1, LANES)
    bg_col = bg.reshape(-1, 1)
    per_b = lambda r, c: pl.BlockSpec((1, r, c), lambda i: (i, 0, 0))
    return pl.pallas_call(
        _mlstm_kernel,
        out_shape=jax.ShapeDtypeStruct((b, s, MLSTM_V), BF16),
        grid=(b,),
        in_specs=[per_b(s, M_COLS), per_b(MLSTM_QK, s), per_b(MLSTM_V, s), per_b(8, s),
                  _full(conv_w.shape), _full((1, 2 * MLSTM_QK)), _full(cw_col.shape), _full(cb_col.shape),
                  _full((1, LANES)), _full((8, 1)), _full((1, MLSTM_V))],
        out_specs=per_b(s, MLSTM_V),
        scratch_shapes=[pltpu.VMEM((s, MLSTM_QK), F32), pltpu.VMEM((s, MLSTM_QK), F32), pltpu.VMEM((MLSTM_QK, s), F32),
                        pltpu.VMEM((s, LANES), F32), pltpu.VMEM((8, s), F32), pltpu.VMEM((MLSTM_V, s), F32)],
        compiler_params=_cparams("arbitrary"),
        name="mlstm",
    )(zm, mkt, mvt, mgt, conv_w, conv_b.reshape(1, -1), cw_col, cb_col, bg_row, bg_col, norm_g.reshape(1, -1))


def _merge_kernel(x_ref, g_ref, y0_ref, y1_ref, y2_ref, y3_ref, wg_ref, wb_ref, wo_ref, o_ref):
    x = x_ref[...]
    d = x.shape[1]
    h = _rms_rows(x, g_ref[...]).astype(BF16)
    merged = None
    for n, y_ref in enumerate((y0_ref, y1_ref, y2_ref, y3_ref)):
        gate = _sigmoid(_dot_nt(h, wg_ref[n * d:(n + 1) * d, :]))
        term = gate * _dot(y_ref[...], wb_ref[n])
        merged = term if merged is None else merged + term
    o_ref[...] = x + _dot(merged.astype(BF16), wo_ref[...])


def _merge(x2, g, ys, w_gate, w_branch, w_out, tm):
    t, d = x2.shape
    row = lambda w: pl.BlockSpec((tm, w), lambda i: (i, 0))
    return pl.pallas_call(
        _merge_kernel,
        out_shape=jax.ShapeDtypeStruct((t, d), F32),
        grid=(t // tm,),
        in_specs=[row(d), _full((1, d))] + [row(BRANCH_WIDTH)] * N_BRANCH
                 + [_full(w_gate.shape), _full(w_branch.shape), _full(w_out.shape)],
        out_specs=row(d),
        compiler_params=_cparams("arbitrary"),
        name="merge",
    )(x2, g.reshape(1, d), *ys, w_gate, w_branch, w_out)


def _ffn_kernel(x_ref, g_ref, w1_ref, w3_ref, w2_ref, o_ref):
    x = x_ref[...]
    h = _rms_rows(x, g_ref[...]).astype(BF16)
    u = _silu(_dot(h, w1_ref[...])) * _dot(h, w3_ref[...])
    o_ref[...] = x + _dot(u.astype(BF16), w2_ref[...])


def _ffn(x2, g, w1, w3, w2, tm):
    t, d = x2.shape
    row = pl.BlockSpec((tm, d), lambda i: (i, 0))
    return pl.pallas_call(
        _ffn_kernel,
        out_shape=jax.ShapeDtypeStruct((t, d), F32),
        grid=(t // tm,),
        in_specs=[row, _full((1, d)), _full(w1.shape), _full(w3.shape), _full(w2.shape)],
        out_specs=row,
        compiler_params=_cparams("arbitrary"),
        name="ffn",
    )(x2, g.reshape(1, d), w1, w3, w2)


MOE_CAP = 160


def _moe_router_kernel(x_ref, g_ref, wr_ref, h_ref, rank_ref, gate_ref, rrow_ref, cnt_ref):
    tm = x_ref.shape[0]
    h = _rms_rows(x_ref[...], g_ref[...])
    h_ref[...] = h.astype(BF16)
    logits = _dot_hi(h, wr_ref[...])
    lane = lax.broadcasted_iota(jnp.int32, logits.shape, 1)
    logits = jnp.where(lane < N_EXPERTS, logits, -jnp.inf)
    v1 = jnp.max(logits, axis=-1, keepdims=True)
    i1 = jnp.min(jnp.where(logits == v1, lane, LANES), axis=-1, keepdims=True)
    rest = jnp.where(lane == i1, -jnp.inf, logits)
    v2 = jnp.max(rest, axis=-1, keepdims=True)
    i2 = jnp.min(jnp.where(rest == v2, lane, LANES), axis=-1, keepdims=True)
    e2 = jnp.exp(v2 - v1)
    gate_ref[...] = jnp.where(lane == i1, 1.0 / (1.0 + e2), 0.0) + jnp.where(lane == i2, e2 / (1.0 + e2), 0.0)
    routed = jnp.where((lane == i1) | (lane == i2), 1.0, 0.0)
    r_i = lax.broadcasted_iota(jnp.int32, (tm, tm), 0)
    c_i = lax.broadcasted_iota(jnp.int32, (tm, tm), 1)
    rank = _dot(jnp.where(c_i < r_i, 1.0, 0.0).astype(BF16), routed.astype(BF16))
    rank = jnp.where(routed > 0.0, rank, -1.0)
    rank_ref[...] = rank
    eye = jnp.where(lax.broadcasted_iota(jnp.int32, (LANES, LANES), 0)
                    == lax.broadcasted_iota(jnp.int32, (LANES, LANES), 1), 1.0, 0.0)
    rrow_ref[0] = _dot_nt_hi(eye, rank)[0:N_EXPERTS, :]
    cnt_ref[0] = jnp.broadcast_to(jnp.sum(routed, axis=0, keepdims=True), (8, LANES))


def _moe_expert_kernel(cnt_ref, y_ref, h_ref, rrow_ref, rank_ref, gate_ref, w1_ref, w3_ref, w2_ref, ng_ref,
                       o_ref, acc_s, *, norm_out):
    e = pl.program_id(0)
    i = pl.program_id(1)
    tm = h_ref.shape[0]
    cap = MOE_CAP
    lane = lax.broadcasted_iota(jnp.int32, (tm, LANES), 1)
    rank_col = jnp.sum(jnp.where(lane == e, rank_ref[...], 0.0), axis=-1, keepdims=True)
    gate_col = jnp.sum(jnp.where(lane == e, gate_ref[...], 0.0), axis=-1, keepdims=True)
    rank_row = rrow_ref[0, pl.ds(e, 1), :]
    slot_r = lax.broadcasted_iota(jnp.int32, (cap, tm), 0).astype(F32)
    slot_c = lax.broadcasted_iota(jnp.int32, (tm, cap), 1).astype(F32)
    acc_s[...] = jnp.zeros_like(acc_s)

    def one_pass(p, carry):
        base = (p * cap).astype(F32)
        gather = jnp.where(rank_row - base == slot_r, 1.0, 0.0).astype(BF16)
        xc = _dot(gather, h_ref[...]).astype(BF16)
        u = _silu(_dot(xc, w1_ref[0])) * _dot(xc, w3_ref[0])
        yc_hi, yc_lo = _split_bf16(_dot(u.astype(BF16), w2_ref[0]))
        scatter = jnp.where(rank_col - base == slot_c, 1.0, 0.0).astype(BF16)
        acc_s[...] += _dot(scatter, yc_hi) + _dot(scatter, yc_lo)
        return carry

    n_pass = (cnt_ref[i, e] + cap - 1) // cap
    lax.fori_loop(0, n_pass, one_pass, 0)
    out = y_ref[...] + gate_col * acc_s[...]
    if norm_out:
        is_last = e == pl.num_programs(0) - 1

        @pl.when(is_last)
        def _():
            o_ref[...] = _rms_rows(out, ng_ref[...])

        @pl.when(jnp.logical_not(is_last))
        def _():
            o_ref[...] = out
    else:
        o_ref[...] = out


def _moe(x2, g, router, w1, w3, w2, tm, out_norm_g=None):
    t, d = x2.shape
    n_e, _, f = w1.shape
    n_t = t // tm
    router_p = jnp.pad(router, ((0, 0), (0, LANES - n_e)))
    row = lambda w: pl.BlockSpec((tm, w), lambda i: (i, 0))
    h, rank, gate, rrow, cnt = pl.pallas_call(
        _moe_router_kernel,
        out_shape=[jax.ShapeDtypeStruct((t, d), BF16), jax.ShapeDtypeStruct((t, LANES), F32),
                   jax.ShapeDtypeStruct((t, LANES), F32), jax.ShapeDtypeStruct((n_t, N_EXPERTS, tm), F32),
                   jax.ShapeDtypeStruct((n_t, 8, LANES), F32)],
        grid=(n_t,),
        in_specs=[row(d), _full((1, d)), _full((d, LANES))],
        out_specs=[row(d), row(LANES), row(LANES), pl.BlockSpec((1, N_EXPERTS, tm), lambda i: (i, 0, 0)),
                   pl.BlockSpec((1, 8, LANES), lambda i: (i, 0, 0))],
        compiler_params=_cparams("arbitrary"),
        name="moe_router",
    )(x2, g.reshape(1, d), router_p)
    counts = cnt[:, 0, :n_e].astype(jnp.int32)
    tile = lambda w: pl.BlockSpec((tm, w), lambda e, i, c: (i, 0))
    weight = lambda r, c_: pl.BlockSpec((1, r, c_), lambda e, i, c: (e, 0, 0), pipeline_mode=pl.Buffered(1))
    norm_out = out_norm_g is not None
    norm_g = out_norm_g.reshape(1, d) if norm_out else jnp.ones((1, d), F32)
    return pl.pallas_call(
        functools.partial(_moe_expert_kernel, norm_out=norm_out),
        out_shape=jax.ShapeDtypeStruct((t, d), F32),
        grid_spec=pltpu.PrefetchScalarGridSpec(
            num_scalar_prefetch=1,
            grid=(n_e, n_t),
            in_specs=[tile(d), tile(d), pl.BlockSpec((1, N_EXPERTS, tm), lambda e, i, c: (i, 0, 0)),
                      tile(LANES), tile(LANES), weight(d, f), weight(d, f), weight(f, d),
                      pl.BlockSpec((1, d), lambda e, i, c: (0, 0))],
            out_specs=tile(d),
            scratch_shapes=[pltpu.VMEM((tm, d), F32)]),
        input_output_aliases={1: 0},
        compiler_params=_cparams("arbitrary", "arbitrary"),
        name="moe_experts",
    )(counts, x2, h, rrow, rank, gate, w1, w3, w2, norm_g)


def _final_norm_kernel(x_ref, g_ref, o_ref):
    o_ref[...] = _rms_rows(x_ref[...], g_ref[...])


def _final_norm(x2, g, tm):
    t, d = x2.shape
    row = pl.BlockSpec((tm, d), lambda i: (i, 0))
    return pl.pallas_call(
        _final_norm_kernel,
        out_shape=jax.ShapeDtypeStruct((t, d), F32),
        grid=(t // tm,),
        in_specs=[row, _full((1, d))],
        out_specs=row,
        compiler_params=_cparams("arbitrary"),
        name="final_norm",
    )(x2, g.reshape(1, d))


def kernel(x, norm_mix_g, w_in, dsa_g_cq, dsa_g_ckv, dsa_g_kidx, dsa_w_uq, dsa_w_qidx, dsa_w_uv, rwkv_mu, rwkv_w0, rwkv_w_up, rwkv_a0, rwkv_a_up, rwkv_g_up, rwkv_k_k, rwkv_k_a, rwkv_r_k, rwkv_ln_g, rwkv_ln_b, mlstm_conv_w, mlstm_conv_b, mlstm_b_i, mlstm_b_f, mlstm_norm_g, fox_b_f, w_branch, w_out, norm_ffn_g, ffn_w1, ffn_w3, ffn_w2, moe_router, moe_w1, moe_w3, moe_w2, final_norm_g):
    b, s, d = x.shape
    depth = w_in.shape[0]
    t = b * s
    tm = min(512, s)
    for l in range(depth):
        wts = _inproj_weights(w_in[l])
        (zr, zm, fq, fk, fv, ckv, ki, fcol, cqt, ckvt, wit, frow, mkt, mvt, mgt) = _inproj(
            x, norm_mix_g[l], wts, dsa_g_cq[l], dsa_g_ckv[l], dsa_g_kidx[l], fox_b_f[l], tm)
        y_fox = _fox(fq, fk, fv, fcol, frow, min(1024, s))
        y_dsa = _dsa(cqt, wit, ki, ckv, ckvt, dsa_w_uq[l], dsa_w_qidx[l], dsa_w_uv[l], min(256, s // 2)
                     ).reshape(t, BRANCH_WIDTH)
        y_rwkv = _rwkv(zr, rwkv_mu[l], rwkv_w0[l], rwkv_w_up[l], rwkv_a0[l], rwkv_a_up[l], rwkv_g_up[l], rwkv_k_k[l],
                       rwkv_k_a[l], rwkv_r_k[l], rwkv_ln_g[l], rwkv_ln_b[l], min(16, b), min(4, b), 32
                       ).reshape(t, BRANCH_WIDTH)
        y_mlstm = _mlstm(zm, mkt, mvt, mgt, mlstm_conv_w[l], mlstm_conv_b[l], mlstm_b_i[l], mlstm_b_f[l],
                         mlstm_norm_g[l]).reshape(t, BRANCH_WIDTH)
        ys = (y_dsa, y_rwkv, y_mlstm, y_fox.reshape(t, BRANCH_WIDTH))
        x2 = _merge(x.reshape(t, d), norm_mix_g[l], ys, wts['w_gate'], w_branch[l].astype(BF16),
                    w_out[l].astype(BF16), tm)
        j = l // 2
        if l % 2 == 0:
            x2 = _ffn(x2, norm_ffn_g[l], ffn_w1[j].astype(BF16), ffn_w3[j].astype(BF16), ffn_w2[j].astype(BF16), tm)
        else:
            x2 = _moe(x2, norm_ffn_g[l], moe_router[j], moe_w1[j].astype(BF16), moe_w3[j].astype(BF16),
                      moe_w2[j].astype(BF16), tm, final_norm_g if l == depth - 1 else None)
        x = x2.reshape(b, s, d)
    if depth > 0 and depth % 2 == 0:
        return x
    return _final_norm(x.reshape(t, d), final_norm_g, tm).reshape(b, s, d)
```

```python
import functools

import jax
import jax.numpy as jnp
from jax import lax
from jax.experimental import pallas as pl
from jax.experimental.pallas import tpu as pltpu

F32 = jnp.float32
BF16 = jnp.bfloat16
HIGHEST = lax.Precision.HIGHEST

EPS = 1e-6
NEG_BIG = -1e30

N_BRANCH = 4
BRANCH_WIDTH = 256
DSA_HEADS = 4
DSA_HEAD_DIM = 64
DSA_Q_LATENT = 128
DSA_KV_LATENT = 128
IDX_HEADS = 8
IDX_DIM = 32
TOPK_MAX = 256
DSA_Q_BLOCK = 128

RWKV_HEADS = 4
RWKV_HEAD_DIM = 64
RWKV_WIDTH = RWKV_HEADS * RWKV_HEAD_DIM
RWKV_LORA_W = 64
RWKV_LORA_A = 64
RWKV_LORA_G = 128
RWKV_GN_EPS = 64e-5
RWKV_IN = 3 * RWKV_WIDTH + RWKV_LORA_W + RWKV_LORA_A + RWKV_LORA_G

MLSTM_HEADS = 4
MLSTM_QK_DIM = 32
MLSTM_V_DIM = 64
MLSTM_CHUNK = 256
CONV_WIDTH = 4
GATE_SOFTCAP = 15.0
MLSTM_QK = MLSTM_HEADS * MLSTM_QK_DIM
MLSTM_V = MLSTM_HEADS * MLSTM_V_DIM

FOX_HEADS = 4
FOX_HEAD_DIM = 64
FOX_WIDTH = FOX_HEADS * FOX_HEAD_DIM

N_EXPERTS = 8

VMEM_LIMIT_BYTES = 56 * 1024 * 1024
LANES = 128
SUBLANES = 8
BF16_SUBLANES = 16


def _cparams(*sem):
    return pltpu.CompilerParams(dimension_semantics=sem, vmem_limit_bytes=VMEM_LIMIT_BYTES)


def _dot(a, b):
    return jnp.dot(a, b, preferred_element_type=F32)


def _dot_hi(a, b):
    return jnp.dot(a, b, preferred_element_type=F32, precision=HIGHEST)


def _dot_nt(a, b):
    return lax.dot_general(a, b, (((1,), (1,)), ((), ())), preferred_element_type=F32)


def _dot_nt_hi(a, b):
    return lax.dot_general(a, b, (((1,), (1,)), ((), ())), preferred_element_type=F32, precision=HIGHEST)


def _log_sigmoid(t):
    return jnp.minimum(t, 0.0) - jnp.log1p(jnp.exp(-jnp.abs(t)))


def _sigmoid(t):
    return 1.0 / (1.0 + jnp.exp(-t))


def _silu(t):
    return t * _sigmoid(t)


def _rms_rows(t, g_row):
    return t * lax.rsqrt(jnp.mean(t * t, axis=-1, keepdims=True) + EPS) * g_row


def _rms_cols(t, g_col):
    return t * lax.rsqrt(jnp.mean(t * t, axis=0, keepdims=True) + EPS) * g_col


def _full(shape):
    n = len(shape)
    return pl.BlockSpec(shape, lambda *_: (0,) * n)


T_CQ = 0
T_CKV = T_CQ + DSA_Q_LATENT
T_WIDX = T_CKV + DSA_KV_LATENT
T_FOXF = T_WIDX + IDX_HEADS
T_MK = T_FOXF + SUBLANES
T_MV = T_MK + MLSTM_QK
T_MG = T_MV + MLSTM_V
T_ROWS = T_MG + 2 * MLSTM_HEADS
M_COLS = 2 * MLSTM_QK + 2 * MLSTM_V + LANES


def _inproj_kernel(x_ref, g_ref, wr_ref, wm_ref, wf_ref, wd_ref, wt_ref,
                   gcq_ref, gckv_col_ref, gckv_row_ref, gki_ref, bf_row_ref, bf_col_ref,
                   zr_ref, zm_ref, q_ref, k_ref, v_ref, ckv_ref, ki_ref, fcol_ref,
                   cqt_ref, ckvt_ref, wit_ref, frow_ref, mkt_ref, mvt_ref, mgt_ref,
                   carry_row, carry_col):
    j = pl.program_id(1)
    tm = x_ref.shape[1]

    @pl.when(j == 0)
    def _():
        carry_row[...] = jnp.zeros_like(carry_row)
        carry_col[...] = jnp.zeros_like(carry_col)

    x = x_ref[0]
    h = _rms_rows(x, g_ref[...]).astype(BF16)

    zr_ref[0] = _dot_nt(h, wr_ref[...])
    zm_ref[0] = _dot_nt(h, wm_ref[...])

    zf = _dot_nt(h, wf_ref[...])
    for hd in range(FOX_HEADS):
        lo = hd * FOX_HEAD_DIM
        q_ref[0, hd] = (zf[:, lo:lo + FOX_HEAD_DIM] * FOX_HEAD_DIM ** -0.5).astype(BF16)
        k_ref[0, hd] = zf[:, FOX_WIDTH + lo:FOX_WIDTH + lo + FOX_HEAD_DIM].astype(BF16)
        v_ref[0, hd] = zf[:, 2 * FOX_WIDTH + lo:2 * FOX_WIDTH + lo + FOX_HEAD_DIM].astype(BF16)

    zd = _dot_nt(h, wd_ref[...])
    ckv_ref[0] = _rms_rows(zd[:, :LANES], gckv_row_ref[...]).astype(BF16)
    ki_pieces = [p.astype(F32) for p in _split3_bf16(_rms_rows(zd[:, LANES:LANES + IDX_DIM], gki_ref[...]))]
    ki_pad = jnp.zeros((tm, IDX_K_COLS - len(IDX_SPLIT_K) * IDX_DIM), F32)
    ki_ref[0] = jnp.concatenate([ki_pieces[kp] for kp, _ in IDX_SPLIT_K] + [ki_pad], axis=1).astype(BF16)
    lf_col = _log_sigmoid(zd[:, 2 * LANES:] + bf_row_ref[...])
    r_i = lax.broadcasted_iota(jnp.int32, (tm, tm), 0)
    c_i = lax.broadcasted_iota(jnp.int32, (tm, tm), 1)
    tri = jnp.where(c_i <= r_i, 1.0, 0.0).astype(BF16)
    cum_col = sum(_dot(tri, p) for p in _split3_bf16(lf_col)) + carry_row[...]
    fcol_ref[0] = cum_col
    carry_row[...] = cum_col[tm - 1:tm, :]

    zt = _dot_nt(wt_ref[...], h)
    cqt_ref[0] = _rms_cols(zt[T_CQ:T_CKV], gcq_ref[...])
    ckvt_ref[0] = _rms_cols(zt[T_CKV:T_WIDX], gckv_col_ref[...]).astype(BF16)
    wit_ref[0] = zt[T_WIDX:T_FOXF] * IDX_HEADS ** -0.5
    lf_row = _log_sigmoid(zt[T_FOXF:T_MK] + bf_col_ref[...])
    cum_row = sum(_dot_nt(p, tri) for p in _split3_bf16(lf_row)) + carry_col[:, 0:1]
    frow_ref[0] = cum_row
    carry_col[...] = jnp.broadcast_to(cum_row[:, tm - 1:tm], carry_col.shape)
    mkt_ref[0] = zt[T_MK:T_MV]
    mvt_ref[0] = zt[T_MV:T_MG]
    mgt_ref[0] = zt[T_MG:T_ROWS]


def _inproj_weights(w_in_l):
    wt = w_in_l.T
    o = 0
    w_cq = wt[o:o + DSA_Q_LATENT]; o += DSA_Q_LATENT
    w_ckv = wt[o:o + DSA_KV_LATENT]; o += DSA_KV_LATENT
    w_kidx = wt[o:o + IDX_DIM]; o += IDX_DIM
    w_widx = wt[o:o + IDX_HEADS]; o += IDX_HEADS
    w_rwkv = wt[o:o + RWKV_IN]; o += RWKV_IN
    w_mq = wt[o:o + MLSTM_QK]; o += MLSTM_QK
    w_mk = wt[o:o + MLSTM_QK]; o += MLSTM_QK
    w_mv = wt[o:o + MLSTM_V]; o += MLSTM_V
    w_mo = wt[o:o + MLSTM_V]; o += MLSTM_V
    w_mi = wt[o:o + MLSTM_HEADS]; o += MLSTM_HEADS
    w_mf = wt[o:o + MLSTM_HEADS]; o += MLSTM_HEADS
    w_fox = wt[o:o + 3 * FOX_WIDTH]; o += 3 * FOX_WIDTH
    w_ff = wt[o:o + FOX_HEADS]; o += FOX_HEADS
    w_gate = wt[o:]

    def padr(w, n):
        return jnp.pad(w, ((0, n - w.shape[0]), (0, 0)))

    w_d = jnp.concatenate([w_ckv, padr(w_kidx, LANES), padr(w_ff, LANES)], axis=0)
    w_m = jnp.concatenate([w_mq, w_mk, w_mv, w_mo, padr(jnp.concatenate([w_mi, w_mf], axis=0), LANES)], axis=0)
    w_t = jnp.concatenate([w_cq, w_ckv, w_widx, padr(w_ff, 8), w_mk, w_mv, w_mi, w_mf], axis=0)
    assert w_t.shape[0] == T_ROWS
    cast = lambda w: w.astype(BF16)
    return dict(w_r=cast(w_rwkv), w_m=cast(w_m), w_f=cast(w_fox), w_d=cast(w_d), w_t=cast(w_t), w_gate=cast(w_gate))


def _inproj(x, g, wts, g_cq, g_ckv, g_kidx, fox_b_f, tm):
    b, s, d = x.shape
    nj = s // tm
    bf_row = jnp.pad(fox_b_f, (0, LANES - FOX_HEADS)).reshape(1, LANES)
    bf_col = jnp.pad(fox_b_f, (0, 8 - FOX_HEADS)).reshape(8, 1)
    row = lambda w: pl.BlockSpec((1, tm, w), lambda i, j: (i, j, 0))
    head = pl.BlockSpec((1, FOX_HEADS, tm, FOX_HEAD_DIM), lambda i, j: (i, 0, j, 0))
    col = lambda r: pl.BlockSpec((1, r, tm), lambda i, j: (i, 0, j))
    out_shape = [
        jax.ShapeDtypeStruct((b, s, RWKV_IN), F32),
        jax.ShapeDtypeStruct((b, s, M_COLS), F32),
        jax.ShapeDtypeStruct((b, FOX_HEADS, s, FOX_HEAD_DIM), BF16),
        jax.ShapeDtypeStruct((b, FOX_HEADS, s, FOX_HEAD_DIM), BF16),
        jax.ShapeDtypeStruct((b, FOX_HEADS, s, FOX_HEAD_DIM), BF16),
        jax.ShapeDtypeStruct((b, s, DSA_KV_LATENT), BF16),
        jax.ShapeDtypeStruct((b, s, IDX_K_COLS), BF16),
        jax.ShapeDtypeStruct((b, s, LANES), F32),
        jax.ShapeDtypeStruct((b, DSA_Q_LATENT, s), F32),
        jax.ShapeDtypeStruct((b, DSA_KV_LATENT, s), BF16),
        jax.ShapeDtypeStruct((b, IDX_HEADS, s), F32),
        jax.ShapeDtypeStruct((b, 8, s), F32),
        jax.ShapeDtypeStruct((b, MLSTM_QK, s), F32),
        jax.ShapeDtypeStruct((b, MLSTM_V, s), F32),
        jax.ShapeDtypeStruct((b, 8, s), F32),
    ]
    out_specs = [row(RWKV_IN), row(M_COLS), head, head, head, row(DSA_KV_LATENT), row(IDX_K_COLS), row(LANES),
                 col(DSA_Q_LATENT), col(DSA_KV_LATENT), col(IDX_HEADS), col(8), col(MLSTM_QK), col(MLSTM_V), col(8)]
    in_specs = [row(d), _full((1, d)), _full(wts['w_r'].shape), _full(wts['w_m'].shape), _full(wts['w_f'].shape),
                _full(wts['w_d'].shape), _full(wts['w_t'].shape),
                _full((DSA_Q_LATENT, 1)), _full((DSA_KV_LATENT, 1)), _full((1, DSA_KV_LATENT)), _full((1, IDX_DIM)),
                _full((1, LANES)), _full((8, 1))]
    return pl.pallas_call(
        _inproj_kernel,
        out_shape=out_shape,
        grid=(b, nj),
        in_specs=in_specs,
        out_specs=out_specs,
        scratch_shapes=[pltpu.VMEM((1, LANES), F32), pltpu.VMEM((8, LANES), F32)],
        compiler_params=_cparams("arbitrary", "arbitrary"),
        name="inproj",
    )(x, g.reshape(1, d), wts['w_r'], wts['w_m'], wts['w_f'], wts['w_d'], wts['w_t'],
      g_cq.reshape(-1, 1), g_ckv.reshape(-1, 1), g_ckv.reshape(1, -1), g_kidx.reshape(1, -1), bf_row, bf_col)


def _fox_kernel(q_ref, k_ref, v_ref, fcol_ref, frow_ref, o_ref):
    qi = pl.program_id(1)
    tq = q_ref.shape[2]
    outs = []
    for hd in range(FOX_HEADS):
        q = q_ref[0, hd]
        fq = fcol_ref[0][:, hd:hd + 1]

        def scores(j):
            start = pl.multiple_of(j * tq, tq)
            k = k_ref[0, hd, pl.ds(start, tq), :]
            v = v_ref[0, hd, pl.ds(start, tq), :]
            fk = frow_ref[0, hd:hd + 1, pl.ds(start, tq)]
            return _dot_nt(q, k) + (fq - fk), v

        def update(carry, sc, v):
            m, l, acc = carry
            m_new = jnp.maximum(m, jnp.max(sc, axis=-1, keepdims=True))
            alpha = jnp.exp(m - m_new)
            p = jnp.exp(sc - m_new)
            l = alpha * l + jnp.sum(p, axis=-1, keepdims=True)
            acc = alpha * acc + _dot(p.astype(BF16), v)
            return m_new, l, acc

        def body(j, carry):
            sc, v = scores(j)
            return update(carry, sc, v)

        init = (jnp.full((tq, 1), NEG_BIG, F32), jnp.zeros((tq, 1), F32), jnp.zeros((tq, FOX_HEAD_DIM), F32))
        carry = lax.fori_loop(0, qi, body, init)
        sc, v = scores(qi)
        r_i = lax.broadcasted_iota(jnp.int32, (tq, tq), 0)
        c_i = lax.broadcasted_iota(jnp.int32, (tq, tq), 1)
        sc = jnp.where(c_i <= r_i, sc, NEG_BIG)
        m, l, acc = update(carry, sc, v)
        outs.append(acc / l)
    o_ref[0] = jnp.concatenate(outs, axis=-1).astype(o_ref.dtype)


def _fox(q, k, v, fcol, frow, tq):
    b, nh, s, d = q.shape
    return pl.pallas_call(
        _fox_kernel,
        out_shape=jax.ShapeDtypeStruct((b, s, nh * d), BF16),
        grid=(b, s // tq),
        in_specs=[
            pl.BlockSpec((1, nh, tq, d), lambda i, j: (i, 0, j, 0)),
            pl.BlockSpec((1, nh, s, d), lambda i, j: (i, 0, 0, 0)),
            pl.BlockSpec((1, nh, s, d), lambda i, j: (i, 0, 0, 0)),
            pl.BlockSpec((1, tq, LANES), lambda i, j: (i, j, 0)),
            pl.BlockSpec((1, 8, s), lambda i, j: (i, 0, 0)),
        ],
        out_specs=pl.BlockSpec((1, tq, nh * d), lambda i, j: (i, j, 0)),
        compiler_params=_cparams("arbitrary", "arbitrary"),
        name="fox",
    )(q, k, v, fcol, frow)


INT_MIN = -2 ** 31


IDX_SPLIT_K = ((0, 0), (0, 1), (1, 0), (0, 2), (1, 1), (2, 0))
IDX_K_COLS = 2 * LANES


def _split3_bf16(t):
    p0 = t.astype(BF16)
    r1 = t - p0.astype(F32)
    p1 = r1.astype(BF16)
    p2 = (r1 - p1.astype(F32)).astype(BF16)
    return p0, p1, p2


def _tree_sum(parts):
    while len(parts) > 1:
        parts = [parts[i] + parts[i + 1] for i in range(0, len(parts) - 1, 2)] + (parts[-1:] if len(parts) % 2 else [])
    return parts[0]


def _dsa_kernel(cqt_ref, wit_ref, ki6_ref, ckv_ref, ckvt_ref, wuqt_ref, wqit_ref, wuvt_ref, o_ref, key_s, *, ck):
    qb = pl.program_id(1)
    s = ki6_ref.shape[1]
    nq = cqt_ref.shape[2]
    topk = min(TOPK_MAX, s // 4)
    n_ck = (qb * nq + nq + ck - 1) // ck
    lat = DSA_KV_LATENT

    cq = cqt_ref[0]
    qi_t = _dot_hi(wqit_ref[...], cq)
    zeros_pad = jnp.zeros((IDX_K_COLS - len(IDX_SPLIT_K) * IDX_DIM, nq), BF16)
    q_blocks = []
    for hd in range(IDX_HEADS):
        pieces = _split3_bf16(qi_t[hd * IDX_DIM:(hd + 1) * IDX_DIM, :])
        q_blocks.append(jnp.concatenate([pieces[qp] for _, qp in IDX_SPLIT_K] + [zeros_pad], axis=0))
    q6 = jnp.concatenate(q_blocks, axis=1)
    wi = wit_ref[0] * IDX_DIM ** -0.5
    tpos = qb * nq + lax.broadcasted_iota(jnp.int32, (ck, nq), 1)
    row = lax.broadcasted_iota(jnp.int32, (ck, nq), 0)

    def chunk_start(c):
        return pl.multiple_of(c * ck, ck)

    def score_chunk(c, carry):
        c0 = chunk_start(c)
        ki6 = ki6_ref[0, pl.ds(c0, ck), :]
        score = None
        for hp in range(0, IDX_HEADS, 2):
            dots = _dot(ki6, q6[:, hp * nq:(hp + 2) * nq])
            for i in range(2):
                term = wi[hp + i:hp + i + 1, :] * jnp.maximum(dots[:, i * nq:(i + 1) * nq], 0.0)
                score = term if score is None else score + term
        bits = pltpu.bitcast(score, jnp.int32)
        key = jnp.where(bits < 0, bits ^ 0x7FFFFFFF, bits)
        key = jnp.where(score == 0.0, 0, key)
        key_s[pl.ds(c0, ck), :] = jnp.where(c0 + row <= tpos, key, INT_MIN)
        return carry

    n_pair = (n_ck + 1) // 2
    lax.fori_loop(0, n_pair, lambda j, carry: score_chunk(2 * j + 1, score_chunk(2 * j, carry)), 0)

    def count(mask_fn):
        def body(j, acc):
            parts = []
            for c in (2 * j, 2 * j + 1):
                c0 = chunk_start(c)
                ones = jnp.where(mask_fn(key_s[pl.ds(c0, ck), :], c0 + row), 1, 0)
                parts += [ones[i * SUBLANES:(i + 1) * SUBLANES] for i in range(ck // SUBLANES)]
            return acc + _tree_sum(parts)
        acc = lax.fori_loop(0, n_pair, body, jnp.zeros((SUBLANES, nq), jnp.int32))
        return jnp.sum(acc, axis=0, keepdims=True)

    def value_bit(i, carry):
        lo, n_lo = carry
        cand = lo + jnp.left_shift(jnp.int32(1), 31 - i)
        n_cand = count(lambda key, pos: key >= cand)
        ok = n_cand >= topk
        return jnp.where(ok, cand, lo), jnp.where(ok, n_cand, n_lo)

    thr, n_ge = lax.fori_loop(0, 32, value_bit, (jnp.full((1, nq), INT_MIN, jnp.int32),
                                                 jnp.full((1, nq), s, jnp.int32)))

    n_bits = s.bit_length()

    def tie_search():
        need = topk - count(lambda key, pos: key > thr)

        def index_bit(i, lo):
            cand = lo + jnp.left_shift(jnp.int32(1), n_bits - 1 - i)
            return jnp.where(count(lambda key, pos: (key == thr) & (pos < cand)) < need, cand, lo)

        return lax.fori_loop(0, n_bits, index_bit, jnp.zeros((1, nq), jnp.int32))

    surplus = jnp.max(jnp.where((n_ge > topk) & (thr > INT_MIN), 1, 0))
    last = lax.cond(surplus > 0, tie_search, lambda: jnp.full((1, nq), s, jnp.int32))

    q_t = (_dot(wuqt_ref[...], cq.astype(BF16)) * lat ** -0.5).astype(BF16)

    def attend(c, carry):
        c0 = chunk_start(c)
        key = key_s[pl.ds(c0, ck), :]
        pos = c0 + row
        sel = ((key > thr) | ((key == thr) & (pos <= last))) & (pos <= tpos)
        bias = jnp.where(sel, 0.0, NEG_BIG)
        ckv = ckv_ref[0, pl.ds(c0, ck), :]
        ckvt = ckvt_ref[0, :, pl.ds(c0, ck)]
        new = []
        for hd in range(DSA_HEADS):
            m, l, acc = carry[hd]
            lg = _dot(ckv, q_t[hd * lat:(hd + 1) * lat, :]) + bias
            m_new = jnp.maximum(m, jnp.max(lg, axis=0, keepdims=True))
            alpha = jnp.exp(m - m_new)
            p = jnp.exp(lg - m_new)
            new.append((m_new, alpha * l + jnp.sum(p, axis=0, keepdims=True),
                        alpha * acc + _dot(ckvt, p.astype(BF16))))
        return tuple(new)

    init = tuple((jnp.full((1, nq), NEG_BIG, F32), jnp.zeros((1, nq), F32), jnp.zeros((lat, nq), F32))
                 for _ in range(DSA_HEADS))
    final = lax.fori_loop(0, n_pair, lambda j, carry: attend(2 * j + 1, attend(2 * j, carry)), init)
    outs = [_dot(wuvt_ref[hd], (acc / l).astype(BF16)) for hd, (_, l, acc) in enumerate(final)]
    o_ref[0] = jnp.concatenate(outs, axis=0).T.astype(o_ref.dtype)


def _dsa(cqt, wit, ki6, ckv, ckvt, w_uq, w_qidx, w_uv, ck):
    b, c, s = cqt.shape
    nq = DSA_Q_BLOCK
    wuqt = w_uq.reshape(c, -1).T.astype(BF16)
    wqit = w_qidx.reshape(c, -1).T
    wuvt = jnp.transpose(w_uv, (0, 2, 1)).astype(BF16)
    return pl.pallas_call(
        functools.partial(_dsa_kernel, ck=ck),
        out_shape=jax.ShapeDtypeStruct((b, s, DSA_HEADS * DSA_HEAD_DIM), BF16),
        grid=(b, s // nq),
        in_specs=[pl.BlockSpec((1, c, nq), lambda i, j: (i, 0, j)),
                  pl.BlockSpec((1, IDX_HEADS, nq), lambda i, j: (i, 0, j)),
                  pl.BlockSpec((1, s, IDX_K_COLS), lambda i, j: (i, 0, 0)),
                  pl.BlockSpec((1, s, DSA_KV_LATENT), lambda i, j: (i, 0, 0)),
                  pl.BlockSpec((1, DSA_KV_LATENT, s), lambda i, j: (i, 0, 0)),
                  _full(wuqt.shape), _full(wqit.shape), _full(wuvt.shape)],
        out_specs=pl.BlockSpec((1, nq, DSA_HEADS * DSA_HEAD_DIM), lambda i, j: (i, j, 0)),
        scratch_shapes=[pltpu.VMEM((s, nq), jnp.int32)],
        compiler_params=_cparams("arbitrary", "arbitrary"),
        name="dsa",
    )(cqt, wit, ki6, ckv, ckvt, wuqt, wqit, wuvt)


Y_PAD = 8


def _head_ones(n, dtype):
    r_i = lax.broadcasted_iota(jnp.int32, (n, n), 0) // RWKV_HEAD_DIM
    c_i = lax.broadcasted_iota(jnp.int32, (n, n), 1) // RWKV_HEAD_DIM
    return jnp.where(r_i == c_i, 1.0, 0.0).astype(dtype)


def _split_bf16(t):
    hi = t.astype(BF16)
    lo = (t - hi.astype(F32)).astype(BF16)
    return hi, lo


def _rwkv_kernel(z_ref, mu_ref, w0_ref, wup_ref, a0_ref, aup_ref, gup_ref, kk_ref, ka_ref, rk_ref, lng_ref, lnb_ref,
                 o_ref, st_s, prev_s, step_s, y_s, bonus_s, gate_s, *, grp):
    c = pl.program_id(1)
    n_g, tc, _ = z_ref.shape
    hd, wd_ = RWKV_HEAD_DIM, RWKV_WIDTH

    @pl.when(c == 0)
    def _():
        st_s[...] = jnp.zeros_like(st_s)
        prev_s[...] = jnp.zeros_like(prev_s)

    ones_b = _head_ones(wd_, BF16)
    diag = jnp.where(lax.broadcasted_iota(jnp.int32, (hd, wd_), 0)
                     == lax.broadcasted_iota(jnp.int32, (hd, wd_), 1) % hd, 1.0, 0.0).astype(F32)

    def head_sum(t):
        hi, lo = _split_bf16(t)
        return _dot(hi, ones_b) + _dot(lo, ones_b)

    zs = []
    for g in range(n_g):
        z = z_ref[g]
        row = lax.broadcasted_iota(jnp.int32, z.shape, 0)
        z_prev = jnp.where(row == 0, prev_s[g], pltpu.roll(z, 1, 0))
        prev_s[g] = z[tc - 1:tc, :]
        zs.append(z + mu_ref[...] * (z_prev - z))
    z = jnp.concatenate(zs, axis=0)
    r = z[:, 0:wd_]
    k = z[:, wd_:2 * wd_]
    v = z[:, 2 * wd_:3 * wd_]
    o = 3 * wd_
    w_lora = z[:, o:o + RWKV_LORA_W]
    a_lora = z[:, o + RWKV_LORA_W:o + RWKV_LORA_W + RWKV_LORA_A]
    g_lora = z[:, o + RWKV_LORA_W + RWKV_LORA_A:]
    w_log = _log_sigmoid(w0_ref[...] + _dot_hi(jnp.tanh(w_lora), wup_ref[...])) - 0.5
    a = _sigmoid(a0_ref[...] + _dot_hi(a_lora, aup_ref[...]))
    gate_s[...] = _dot_hi(_sigmoid(g_lora), gup_ref[...])
    kk = k * kk_ref[...]
    kk = kk / jnp.maximum(jnp.sqrt(head_sum(kk * kk)), 1e-12)
    k = k * (1.0 + (a - 1.0) * ka_ref[...])
    step_s[0] = jnp.exp(-jnp.exp(w_log))
    step_s[1] = -kk
    step_s[2] = kk * a
    step_s[3] = k
    step_s[4] = r
    step_s[5] = v
    bonus_s[...] = head_sum(r * k * rk_ref[...]) * v

    n_grp = n_g // grp

    def rows(kind, g0, t):
        return jnp.concatenate(
            [jnp.broadcast_to(step_s[kind, pl.ds((g0 + i) * tc + t, 1), :], (hd, wd_)) for i in range(grp)], axis=0)

    def store_y(yb, g0, t_write):
        yb = yb * diag_g
        for i in range(grp):
            tile = jnp.sum(yb[i * hd:(i + 1) * hd].reshape(hd // SUBLANES, SUBLANES, wd_), axis=0)
            y_s[pl.ds(Y_PAD + (g0 + i) * tc + t_write, 1), :] = jnp.sum(tile, axis=0, keepdims=True)

    def rows_b(kind, g0, t):
        tiles = []
        for i in range(grp):
            r16 = jnp.broadcast_to(step_s[kind, pl.ds((g0 + i) * tc + t, 1), :], (BF16_SUBLANES, wd_)).astype(BF16)
            tiles.append(jnp.broadcast_to(r16[None], (hd // BF16_SUBLANES, BF16_SUBLANES, wd_)).reshape(hd, wd_))
        return jnp.concatenate(tiles, axis=0)

    def step(t, carry):
        for q in range(n_grp):
            g0 = q * grp
            st = st_s[q]
            st_b = st.astype(BF16)
            sa = _dot(st_b * rows_b(1, g0, t), ones_b)
            store_y(_dot(st_b * rows_b(4, g0, jnp.maximum(t - 1, 0)), ones_b), g0, t - 1)
            vb = _dot(rows_b(5, g0, t) * diag_b, ones_b)
            st_s[q] = st * rows(0, g0, t) + sa * rows(2, g0, t) + vb * rows(3, g0, t)
        return carry

    diag_g = jnp.concatenate([diag] * grp, axis=0)
    diag_b = diag_g.astype(BF16)
    lax.fori_loop(0, tc, step, 0, unroll=8)
    for q in range(n_grp):
        store_y(_dot(st_s[q].astype(BF16) * rows_b(4, q * grp, tc - 1), ones_b), q * grp, tc - 1)

    y = y_s[pl.ds(Y_PAD, n_g * tc), :]
    mean = head_sum(y) * (1.0 / hd)
    yc = y - mean
    var = head_sum(yc * yc) * (1.0 / hd)
    yn = yc * lax.rsqrt(var + RWKV_GN_EPS) * lng_ref[...] + lnb_ref[...]
    out = ((yn + bonus_s[...]) * gate_s[...]).astype(o_ref.dtype)
    for g in range(n_g):
        o_ref[g] = out[g * tc:(g + 1) * tc]


def _rwkv(zr, mu, w0, w_up, a0, a_up, g_up, k_k, k_a, r_k, ln_g, ln_b, n_g, grp, tc):
    b, s, zin = zr.shape
    wd_ = RWKV_WIDTH
    vec = lambda p: p.reshape(1, -1)
    params = [vec(mu), vec(w0), w_up, vec(a0), a_up, g_up, vec(k_k), vec(k_a), vec(r_k), vec(ln_g), vec(ln_b)]
    return pl.pallas_call(
        functools.partial(_rwkv_kernel, grp=grp),
        out_shape=jax.ShapeDtypeStruct((b, s, wd_), BF16),
        grid=(b // n_g, s // tc),
        in_specs=[pl.BlockSpec((n_g, tc, zin), lambda i, c: (i, c, 0))] + [_full(p.shape) for p in params],
        out_specs=pl.BlockSpec((n_g, tc, wd_), lambda i, c: (i, c, 0)),
        scratch_shapes=[pltpu.VMEM((n_g // grp, grp * RWKV_HEAD_DIM, wd_), F32), pltpu.VMEM((n_g, 1, zin), F32),
                        pltpu.VMEM((6, n_g * tc, wd_), F32), pltpu.VMEM((Y_PAD + n_g * tc, wd_), F32),
                        pltpu.VMEM((n_g * tc, wd_), F32), pltpu.VMEM((n_g * tc, wd_), F32)],
        compiler_params=_cparams("arbitrary", "arbitrary"),
        name="rwkv",
    )(zr, *params)


def _softcap(t):
    return GATE_SOFTCAP * jnp.tanh(t / GATE_SOFTCAP)


def _mlstm_kernel(zm_ref, mkt_ref, mvt_ref, mgt_ref, cw_row_ref, cb_row_ref, cw_col_ref, cb_col_ref,
                  bg_row_ref, bg_col_ref, ng_ref, o_ref, q_s, k_s, kt_s, gc_s, gr_s, yt_s):
    s = zm_ref.shape[1]
    nh, dk, dv, lc = MLSTM_HEADS, MLSTM_QK_DIM, MLSTM_V_DIM, MLSTM_CHUNK
    pair = 2 * lc

    qk = zm_ref[0, :, 0:2 * MLSTM_QK]
    pos_r = lax.broadcasted_iota(jnp.int32, qk.shape, 0)
    acc = cb_row_ref[...] + qk * cw_row_ref[CONV_WIDTH - 1:CONV_WIDTH, :]
    for r in range(1, CONV_WIDTH):
        sh = jnp.where(pos_r >= r, pltpu.roll(qk, r, 0), 0.0)
        acc = acc + sh * cw_row_ref[CONV_WIDTH - 1 - r:CONV_WIDTH - r, :]
    acc = _silu(acc)
    q_s[...] = acc[:, :MLSTM_QK] * dk ** -0.5
    k_s[...] = acc[:, MLSTM_QK:]
    kt = mkt_ref[0]
    pos_c = lax.broadcasted_iota(jnp.int32, kt.shape, 1)
    acc_t = cb_col_ref[...] + kt * cw_col_ref[:, CONV_WIDTH - 1:CONV_WIDTH]
    for r in range(1, CONV_WIDTH):
        sh = jnp.where(pos_c >= r, pltpu.roll(kt, r, 1), 0.0)
        acc_t = acc_t + sh * cw_col_ref[:, CONV_WIDTH - 1 - r:CONV_WIDTH - r]
    kt_s[...] = _silu(acc_t)

    gcol = _softcap(zm_ref[0, :, 2 * MLSTM_QK + 2 * MLSTM_V:] + bg_row_ref[...])
    lane = lax.broadcasted_iota(jnp.int32, gcol.shape, 1)
    gc_s[...] = jnp.where(lane < nh, gcol, _log_sigmoid(gcol))
    grow = _softcap(mgt_ref[0] + bg_col_ref[...])
    sub = lax.broadcasted_iota(jnp.int32, grow.shape, 0)
    gr_s[...] = jnp.where(sub < nh, grow, _log_sigmoid(grow))

    r_i = lax.broadcasted_iota(jnp.int32, (lc, lc), 0)
    c_i = lax.broadcasted_iota(jnp.int32, (lc, lc), 1)
    tri = jnp.where(c_i <= r_i, 1.0, 0.0).astype(F32)
    causal_t = r_i <= c_i
    ones_rows = jnp.ones((8, lc), F32)

    def chunk_pair(p, carry):
        base = pl.multiple_of(p * pair, pair)
        gr_slab = gr_s[:, pl.ds(base, pair)]
        kt_slab = kt_s[:, pl.ds(base, pair)]
        vt_slab = mvt_ref[0, :, pl.ds(base, pair)]
        outs = [[] for _ in range(nh)]
        for sc in range(2):
            r0 = base + sc * lc
            gcc = gc_s[pl.ds(r0, lc), :]
            grc = gr_slab[:, sc * lc:(sc + 1) * lc]
            bcum_col = _dot_hi(tri, gcc)
            bcum_row = _dot_nt_hi(grc, tri)
            qc = q_s[pl.ds(r0, lc), :]
            kc = k_s[pl.ds(r0, lc), :]
            new_carry = []
            for hd in range(nh):
                c_aug, m_prev = carry[hd]
                bc_c = bcum_col[:, nh + hd:nh + hd + 1]
                li_c = gcc[:, hd:hd + 1]
                bc_r = bcum_row[nh + hd:nh + hd + 1, :]
                li_r = grc[hd:hd + 1, :]
                gtot = bc_r[:, lc - 1:lc]
                d_t = jnp.where(causal_t, bc_r - bc_c + li_c, NEG_BIG)
                m_inter = bc_r + m_prev
                m_t = jnp.maximum(m_inter, jnp.max(d_t, axis=0, keepdims=True))
                q_h = qc[:, hd * dk:(hd + 1) * dk].astype(BF16)
                k_h = kc[:, hd * dk:(hd + 1) * dk].astype(BF16)
                kt_h = kt_slab[hd * dk:(hd + 1) * dk, sc * lc:(sc + 1) * lc].astype(BF16)
                vt_h = vt_slab[hd * dv:(hd + 1) * dv, sc * lc:(sc + 1) * lc]
                vt_aug = jnp.concatenate([vt_h, ones_rows], axis=0)
                s_t = _dot_nt(k_h, q_h)
                w_t = jnp.exp(d_t - m_t) * s_t
                s_inter = jnp.exp(m_inter - m_t)
                numden = _dot(vt_aug.astype(BF16), w_t.astype(BF16)) + s_inter * _dot_nt(c_aug.astype(BF16), q_h)
                den = numden[dv:dv + 1, :]
                outs[hd].append(numden[:dv, :] / jnp.maximum(jnp.abs(den), jnp.exp(-m_t)))
                a_log = gtot - bc_r + li_r
                a_max = jnp.max(a_log, axis=-1, keepdims=True)
                a_w = jnp.exp(a_log - a_max)
                kvn = _dot_nt((vt_aug * a_w).astype(BF16), kt_h)
                m_new = jnp.maximum(gtot + m_prev, a_max)
                s_old = jnp.exp(gtot + m_prev - m_new)
                s_new = jnp.exp(a_max - m_new)
                new_carry.append((s_old * c_aug + s_new * kvn, m_new))
            carry = tuple(new_carry)
        for hd in range(nh):
            yt_s[hd * dv:(hd + 1) * dv, pl.ds(base, pair)] = jnp.concatenate(outs[hd], axis=-1)
        return carry

    init = tuple((jnp.zeros((dv + 8, dk), F32), jnp.full((1, 1), NEG_BIG, F32)) for _ in range(nh))
    lax.fori_loop(0, s // pair, chunk_pair, init)

    parts = []
    for hd in range(nh):
        blk = yt_s[hd * dv:(hd + 1) * dv, :]
        parts.append(blk * lax.rsqrt(jnp.mean(blk * blk, axis=0, keepdims=True) + EPS))
    y = jnp.concatenate(parts, axis=0).T
    o_gate = _sigmoid(zm_ref[0, :, 2 * MLSTM_QK + MLSTM_V:2 * MLSTM_QK + 2 * MLSTM_V])
    o_ref[0] = (y * ng_ref[...] * o_gate).astype(o_ref.dtype)


def _mlstm(zm, mkt, mvt, mgt, conv_w, conv_b, b_i, b_f, norm_g):
    b, s, _ = zm.shape
    cw_col = conv_w[:, MLSTM_QK:].T
    cb_col = conv_b[MLSTM_QK:].reshape(-1, 1)
    bg = jnp.concatenate([b_i, b_f])
    bg_row = jnp.pad(bg, (0, LANES - 2 * MLSTM_HEADS)).reshape(1, LANES)
    bg_col = bg.reshape(-1, 1)
    per_b = lambda r, c: pl.BlockSpec((1, r, c), lambda i: (i, 0, 0))
    return pl.pallas_call(
        _mlstm_kernel,
        out_shape=jax.ShapeDtypeStruct((b, s, MLSTM_V), BF16),
        grid=(b,),
        in_specs=[per_b(s, M_COLS), per_b(MLSTM_QK, s), per_b(MLSTM_V, s), per_b(8, s),
                  _full(conv_w.shape), _full((1, 2 * MLSTM_QK)), _full(cw_col.shape), _full(cb_col.shape),
                  _full((1, LANES)), _full((8, 1)), _full((1, MLSTM_V))],
        out_specs=per_b(s, MLSTM_V),
        scratch_shapes=[pltpu.VMEM((s, MLSTM_QK), F32), pltpu.VMEM((s, MLSTM_QK), F32), pltpu.VMEM((MLSTM_QK, s), F32),
                        pltpu.VMEM((s, LANES), F32), pltpu.VMEM((8, s), F32), pltpu.VMEM((MLSTM_V, s), F32)],
        compiler_params=_cparams("arbitrary"),
        name="mlstm",
    )(zm, mkt, mvt, mgt, conv_w, conv_b.reshape(1, -1), cw_col, cb_col, bg_row, bg_col, norm_g.reshape(1, -1))


def _merge_kernel(x_ref, g_ref, y0_ref, y1_ref, y2_ref, y3_ref, wg_ref, wb_ref, wo_ref, o_ref):
    x = x_ref[...]
    d = x.shape[1]
    h = _rms_rows(x, g_ref[...]).astype(BF16)
    merged = None
    for n, y_ref in enumerate((y0_ref, y1_ref, y2_ref, y3_ref)):
        gate = _sigmoid(_dot_nt(h, wg_ref[n * d:(n + 1) * d, :]))
        term = gate * _dot(y_ref[...], wb_ref[n])
        merged = term if merged is None else merged + term
    o_ref[...] = x + _dot(merged.astype(BF16), wo_ref[...])


def _merge(x2, g, ys, w_gate, w_branch, w_out, tm):
    t, d = x2.shape
    row = lambda w: pl.BlockSpec((tm, w), lambda i: (i, 0))
    return pl.pallas_call(
        _merge_kernel,
        out_shape=jax.ShapeDtypeStruct((t, d), F32),
        grid=(t // tm,),
        in_specs=[row(d), _full((1, d))] + [row(BRANCH_WIDTH)] * N_BRANCH
                 + [_full(w_gate.shape), _full(w_branch.shape), _full(w_out.shape)],
        out_specs=row(d),
        compiler_params=_cparams("arbitrary"),
        name="merge",
    )(x2, g.reshape(1, d), *ys, w_gate, w_branch, w_out)


def _ffn_kernel(x_ref, g_ref, w1_ref, w3_ref, w2_ref, o_ref):
    x = x_ref[...]
    h = _rms_rows(x, g_ref[...]).astype(BF16)
    u = _silu(_dot(h, w1_ref[...])) * _dot(h, w3_ref[...])
    o_ref[...] = x + _dot(u.astype(BF16), w2_ref[...])


def _ffn(x2, g, w1, w3, w2, tm):
    t, d = x2.shape
    row = pl.BlockSpec((tm, d), lambda i: (i, 0))
    return pl.pallas_call(
        _ffn_kernel,
        out_shape=jax.ShapeDtypeStruct((t, d), F32),
        grid=(t // tm,),
        in_specs=[row, _full((1, d)), _full(w1.shape), _full(w3.shape), _full(w2.shape)],
        out_specs=row,
        compiler_params=_cparams("arbitrary"),
        name="ffn",
    )(x2, g.reshape(1, d), w1, w3, w2)


MOE_CAP = 160


def _moe_router_kernel(x_ref, g_ref, wr_ref, h_ref, rank_ref, gate_ref, rrow_ref, cnt_ref):
    tm = x_ref.shape[0]
    h = _rms_rows(x_ref[...], g_ref[...])
    h_ref[...] = h.astype(BF16)
    logits = _dot_hi(h, wr_ref[...])
    lane = lax.broadcasted_iota(jnp.int32, logits.shape, 1)
    logits = jnp.where(lane < N_EXPERTS, logits, -jnp.inf)
    v1 = jnp.max(logits, axis=-1, keepdims=True)
    i1 = jnp.min(jnp.where(logits == v1, lane, LANES), axis=-1, keepdims=True)
    rest = jnp.where(lane == i1, -jnp.inf, logits)
    v2 = jnp.max(rest, axis=-1, keepdims=True)
    i2 = jnp.min(jnp.where(rest == v2, lane, LANES), axis=-1, keepdims=True)
    e2 = jnp.exp(v2 - v1)
    gate_ref[...] = jnp.where(lane == i1, 1.0 / (1.0 + e2), 0.0) + jnp.where(lane == i2, e2 / (1.0 + e2), 0.0)
    routed = jnp.where((lane == i1) | (lane == i2), 1.0, 0.0)
    r_i = lax.broadcasted_iota(jnp.int32, (tm, tm), 0)
    c_i = lax.broadcasted_iota(jnp.int32, (tm, tm), 1)
    rank = _dot(jnp.where(c_i < r_i, 1.0, 0.0).astype(BF16), routed.astype(BF16))
    rank = jnp.where(routed > 0.0, rank, -1.0)
    rank_ref[...] = rank
    eye = jnp.where(lax.broadcasted_iota(jnp.int32, (LANES, LANES), 0)
                    == lax.broadcasted_iota(jnp.int32, (LANES, LANES), 1), 1.0, 0.0)
    rrow_ref[0] = _dot_nt_hi(eye, rank)[0:N_EXPERTS, :]
    cnt_ref[0] = jnp.broadcast_to(jnp.sum(routed, axis=0, keepdims=True), (8, LANES))


def _moe_expert_kernel(cnt_ref, y_ref, h_ref, rrow_ref, rank_ref, gate_ref, w1_ref, w3_ref, w2_ref, o_ref, acc_s):
    e = pl.program_id(0)
    i = pl.program_id(1)
    tm = h_ref.shape[0]
    cap = MOE_CAP
    lane = lax.broadcasted_iota(jnp.int32, (tm, LANES), 1)
    rank_col = jnp.sum(jnp.where(lane == e, rank_ref[...], 0.0), axis=-1, keepdims=True)
    gate_col = jnp.sum(jnp.where(lane == e, gate_ref[...], 0.0), axis=-1, keepdims=True)
    rank_row = rrow_ref[0, pl.ds(e, 1), :]
    slot_r = lax.broadcasted_iota(jnp.int32, (cap, tm), 0).astype(F32)
    slot_c = lax.broadcasted_iota(jnp.int32, (tm, cap), 1).astype(F32)
    acc_s[...] = jnp.zeros_like(acc_s)

    def one_pass(p, carry):
        base = (p * cap).astype(F32)
        gather = jnp.where(rank_row - base == slot_r, 1.0, 0.0).astype(BF16)
        xc = _dot(gather, h_ref[...]).astype(BF16)
        u = _silu(_dot(xc, w1_ref[0])) * _dot(xc, w3_ref[0])
        yc_hi, yc_lo = _split_bf16(_dot(u.astype(BF16), w2_ref[0]))
        scatter = jnp.where(rank_col - base == slot_c, 1.0, 0.0).astype(BF16)
        acc_s[...] += _dot(scatter, yc_hi) + _dot(scatter, yc_lo)
        return carry

    n_pass = (cnt_ref[i, e] + cap - 1) // cap
    lax.fori_loop(0, n_pass, one_pass, 0)
    o_ref[...] = y_ref[...] + gate_col * acc_s[...]


def _moe(x2, g, router, w1, w3, w2, tm):
    t, d = x2.shape
    n_e, _, f = w1.shape
    n_t = t // tm
    router_p = jnp.pad(router, ((0, 0), (0, LANES - n_e)))
    row = lambda w: pl.BlockSpec((tm, w), lambda i: (i, 0))
    h, rank, gate, rrow, cnt = pl.pallas_call(
        _moe_router_kernel,
        out_shape=[jax.ShapeDtypeStruct((t, d), BF16), jax.ShapeDtypeStruct((t, LANES), F32),
                   jax.ShapeDtypeStruct((t, LANES), F32), jax.ShapeDtypeStruct((n_t, N_EXPERTS, tm), F32),
                   jax.ShapeDtypeStruct((n_t, 8, LANES), F32)],
        grid=(n_t,),
        in_specs=[row(d), _full((1, d)), _full((d, LANES))],
        out_specs=[row(d), row(LANES), row(LANES), pl.BlockSpec((1, N_EXPERTS, tm), lambda i: (i, 0, 0)),
                   pl.BlockSpec((1, 8, LANES), lambda i: (i, 0, 0))],
        compiler_params=_cparams("arbitrary"),
        name="moe_router",
    )(x2, g.reshape(1, d), router_p)
    counts = cnt[:, 0, :n_e].astype(jnp.int32)
    tile = lambda w: pl.BlockSpec((tm, w), lambda e, i, c: (i, 0))
    weight = lambda r, c_: pl.BlockSpec((1, r, c_), lambda e, i, c: (e, 0, 0), pipeline_mode=pl.Buffered(1))
    return pl.pallas_call(
        _moe_expert_kernel,
        out_shape=jax.ShapeDtypeStruct((t, d), F32),
        grid_spec=pltpu.PrefetchScalarGridSpec(
            num_scalar_prefetch=1,
            grid=(n_e, n_t),
            in_specs=[tile(d), tile(d), pl.BlockSpec((1, N_EXPERTS, tm), lambda e, i, c: (i, 0, 0)),
                      tile(LANES), tile(LANES), weight(d, f), weight(d, f), weight(f, d)],
            out_specs=tile(d),
            scratch_shapes=[pltpu.VMEM((tm, d), F32)]),
        input_output_aliases={1: 0},
        compiler_params=_cparams("arbitrary", "arbitrary"),
        name="moe_experts",
    )(counts, x2, h, rrow, rank, gate, w1, w3, w2)


def _final_norm_kernel(x_ref, g_ref, o_ref):
    o_ref[...] = _rms_rows(x_ref[...], g_ref[...])


def _final_norm(x2, g, tm):
    t, d = x2.shape
    row = pl.BlockSpec((tm, d), lambda i: (i, 0))
    return pl.pallas_call(
        _final_norm_kernel,
        out_shape=jax.ShapeDtypeStruct((t, d), F32),
        grid=(t // tm,),
        in_specs=[row, _full((1, d))],
        out_specs=row,
        compiler_params=_cparams("arbitrary"),
        name="final_norm",
    )(x2, g.reshape(1, d))


def _tiles(b, s):
    tiles = dict(
        rows=min(512, s),
        fox_q=min(1024, s),
        dsa_keys=min(256, s // 2),
        rwkv_seqs=min(16, b),
        rwkv_group=min(4, b),
        rwkv_steps=32,
    )
    assert s % tiles['rows'] == 0 and s % tiles['fox_q'] == 0 and s % (2 * tiles['dsa_keys']) == 0
    assert s % DSA_Q_BLOCK == 0 and s % (2 * MLSTM_CHUNK) == 0 and s % tiles['rwkv_steps'] == 0
    assert b % tiles['rwkv_seqs'] == 0 and tiles['rwkv_seqs'] % tiles['rwkv_group'] == 0
    return tiles


def kernel(x, norm_mix_g, w_in, dsa_g_cq, dsa_g_ckv, dsa_g_kidx, dsa_w_uq, dsa_w_qidx, dsa_w_uv, rwkv_mu, rwkv_w0, rwkv_w_up, rwkv_a0, rwkv_a_up, rwkv_g_up, rwkv_k_k, rwkv_k_a, rwkv_r_k, rwkv_ln_g, rwkv_ln_b, mlstm_conv_w, mlstm_conv_b, mlstm_b_i, mlstm_b_f, mlstm_norm_g, fox_b_f, w_branch, w_out, norm_ffn_g, ffn_w1, ffn_w3, ffn_w2, moe_router, moe_w1, moe_w3, moe_w2, final_norm_g):
    b, s, d = x.shape
    depth = w_in.shape[0]
    t = b * s
    tl = _tiles(b, s)
    for l in range(depth):
        wts = _inproj_weights(w_in[l])
        (zr, zm, fq, fk, fv, ckv, ki6, fcol, cqt, ckvt, wit, frow, mkt, mvt, mgt) = _inproj(
            x, norm_mix_g[l], wts, dsa_g_cq[l], dsa_g_ckv[l], dsa_g_kidx[l], fox_b_f[l], tl['rows'])
        y_fox = _fox(fq, fk, fv, fcol, frow, tl['fox_q'])
        y_dsa = _dsa(cqt, wit, ki6, ckv, ckvt, dsa_w_uq[l], dsa_w_qidx[l], dsa_w_uv[l], tl['dsa_keys'])
        y_rwkv = _rwkv(zr, rwkv_mu[l], rwkv_w0[l], rwkv_w_up[l], rwkv_a0[l], rwkv_a_up[l], rwkv_g_up[l], rwkv_k_k[l],
                       rwkv_k_a[l], rwkv_r_k[l], rwkv_ln_g[l], rwkv_ln_b[l], tl['rwkv_seqs'], tl['rwkv_group'],
                       tl['rwkv_steps'])
        y_mlstm = _mlstm(zm, mkt, mvt, mgt, mlstm_conv_w[l], mlstm_conv_b[l], mlstm_b_i[l], mlstm_b_f[l],
                         mlstm_norm_g[l])
        ys = tuple(y.reshape(t, BRANCH_WIDTH) for y in (y_dsa, y_rwkv, y_mlstm, y_fox))
        x2 = _merge(x.reshape(t, d), norm_mix_g[l], ys, wts['w_gate'], w_branch[l].astype(BF16),
                    w_out[l].astype(BF16), tl['rows'])
        j = l // 2
        if l % 2 == 0:
            x2 = _ffn(x2, norm_ffn_g[l], ffn_w1[j].astype(BF16), ffn_w3[j].astype(BF16), ffn_w2[j].astype(BF16),
                      tl['rows'])
        else:
            x2 = _moe(x2, norm_ffn_g[l], moe_router[j], moe_w1[j].astype(BF16), moe_w3[j].astype(BF16),
                      moe_w2[j].astype(BF16), tl['rows'])
        x = x2.reshape(b, s, d)
    return _final_norm(x.reshape(t, d), final_norm_g, tl['rows']).reshape(b, s, d)
```

```python
import functools

import jax
import jax.numpy as jnp
from jax import lax
from jax.experimental import pallas as pl
from jax.experimental.pallas import tpu as pltpu

F32 = jnp.float32
BF16 = jnp.bfloat16
HIGHEST = lax.Precision.HIGHEST

EPS = 1e-6
NEG_BIG = -1e30

N_BRANCH = 4
BRANCH_WIDTH = 256
DSA_HEADS = 4
DSA_HEAD_DIM = 64
DSA_Q_LATENT = 128
DSA_KV_LATENT = 128
IDX_HEADS = 8
IDX_DIM = 32
TOPK_MAX = 256
DSA_Q_BLOCK = 128

RWKV_HEADS = 4
RWKV_HEAD_DIM = 64
RWKV_WIDTH = RWKV_HEADS * RWKV_HEAD_DIM
RWKV_LORA_W = 64
RWKV_LORA_A = 64
RWKV_LORA_G = 128
RWKV_GN_EPS = 64e-5
RWKV_IN = 3 * RWKV_WIDTH + RWKV_LORA_W + RWKV_LORA_A + RWKV_LORA_G

MLSTM_HEADS = 4
MLSTM_QK_DIM = 32
MLSTM_V_DIM = 64
MLSTM_CHUNK = 256
CONV_WIDTH = 4
GATE_SOFTCAP = 15.0
MLSTM_QK = MLSTM_HEADS * MLSTM_QK_DIM
MLSTM_V = MLSTM_HEADS * MLSTM_V_DIM

FOX_HEADS = 4
FOX_HEAD_DIM = 64
FOX_WIDTH = FOX_HEADS * FOX_HEAD_DIM

N_EXPERTS = 8

VMEM_LIMIT_BYTES = 56 * 1024 * 1024
LANES = 128
SUBLANES = 8
BF16_SUBLANES = 16


def _cparams(*sem):
    return pltpu.CompilerParams(dimension_semantics=sem, vmem_limit_bytes=VMEM_LIMIT_BYTES)


def _dot(a, b):
    return jnp.dot(a, b, preferred_element_type=F32)


def _dot_hi(a, b):
    return jnp.dot(a, b, preferred_element_type=F32, precision=HIGHEST)


def _dot_nt(a, b):
    return lax.dot_general(a, b, (((1,), (1,)), ((), ())), preferred_element_type=F32)


def _dot_nt_hi(a, b):
    return lax.dot_general(a, b, (((1,), (1,)), ((), ())), preferred_element_type=F32, precision=HIGHEST)


def _log_sigmoid(t):
    return jnp.minimum(t, 0.0) - jnp.log1p(jnp.exp(-jnp.abs(t)))


def _sigmoid(t):
    return 1.0 / (1.0 + jnp.exp(-t))


def _silu(t):
    return t * _sigmoid(t)


def _rms_rows(t, g_row):
    return t * lax.rsqrt(jnp.mean(t * t, axis=-1, keepdims=True) + EPS) * g_row


def _rms_cols(t, g_col):
    return t * lax.rsqrt(jnp.mean(t * t, axis=0, keepdims=True) + EPS) * g_col


def _full(shape):
    n = len(shape)
    return pl.BlockSpec(shape, lambda *_: (0,) * n)


T_CQ = 0
T_CKV = T_CQ + DSA_Q_LATENT
T_WIDX = T_CKV + DSA_KV_LATENT
T_FOXF = T_WIDX + IDX_HEADS
T_MK = T_FOXF + SUBLANES
T_MV = T_MK + MLSTM_QK
T_MG = T_MV + MLSTM_V
T_ROWS = T_MG + 2 * MLSTM_HEADS
M_COLS = 2 * MLSTM_QK + 2 * MLSTM_V + LANES


def _inproj_kernel(x_ref, g_ref, wr_ref, wm_ref, wf_ref, wd_ref, wt_ref,
                   gcq_ref, gckv_col_ref, gckv_row_ref, gki_ref, bf_col_ref,
                   zr_ref, zm_ref, q_ref, k_ref, v_ref, ckv_ref, ki_ref,
                   cqt_ref, ckvt_ref, wit_ref, frow_ref, mkt_ref, mvt_ref, mgt_ref,
                   carry_col):
    j = pl.program_id(1)
    tm = x_ref.shape[1]

    @pl.when(j == 0)
    def _():
        carry_col[...] = jnp.zeros_like(carry_col)

    x = x_ref[0]
    h = _rms_rows(x, g_ref[...]).astype(BF16)

    zr_ref[0] = _dot_nt(h, wr_ref[...])
    zm_ref[0] = _dot_nt(h, wm_ref[...])

    zf = _dot_nt(h, wf_ref[...])
    for hd in range(FOX_HEADS):
        lo = hd * FOX_HEAD_DIM
        q_ref[0, hd] = (zf[:, lo:lo + FOX_HEAD_DIM] * FOX_HEAD_DIM ** -0.5).astype(BF16)
        k_ref[0, hd] = zf[:, FOX_WIDTH + lo:FOX_WIDTH + lo + FOX_HEAD_DIM].astype(BF16)
        v_ref[0, hd] = zf[:, 2 * FOX_WIDTH + lo:2 * FOX_WIDTH + lo + FOX_HEAD_DIM].astype(BF16)

    zd = _dot_nt(h, wd_ref[...])
    ckv_ref[0] = _rms_rows(zd[:, :LANES], gckv_row_ref[...]).astype(BF16)
    ki_pieces = [p.astype(F32) for p in _split3_bf16(_rms_rows(zd[:, LANES:LANES + IDX_DIM], gki_ref[...]))]
    ki_pad = jnp.zeros((tm, IDX_K_COLS - len(IDX_SPLIT_K) * IDX_DIM), F32)
    ki_ref[0] = jnp.concatenate([ki_pieces[kp] for kp, _ in IDX_SPLIT_K] + [ki_pad], axis=1).astype(BF16)

    zt = _dot_nt(wt_ref[...], h)
    cqt_ref[0] = _rms_cols(zt[T_CQ:T_CKV], gcq_ref[...])
    ckvt_ref[0] = _rms_cols(zt[T_CKV:T_WIDX], gckv_col_ref[...]).astype(BF16)
    wit_ref[0] = zt[T_WIDX:T_FOXF] * IDX_HEADS ** -0.5
    lf_row = _log_sigmoid(zt[T_FOXF:T_MK] + bf_col_ref[...])
    r_i = lax.broadcasted_iota(jnp.int32, (tm, tm), 0)
    c_i = lax.broadcasted_iota(jnp.int32, (tm, tm), 1)
    tri = jnp.where(c_i <= r_i, 1.0, 0.0).astype(BF16)
    cum_row = sum(_dot_nt(p, tri) for p in _split3_bf16(lf_row)) + carry_col[:, 0:1]
    frow_ref[0] = cum_row
    carry_col[...] = jnp.broadcast_to(cum_row[:, tm - 1:tm], carry_col.shape)
    mkt_ref[0] = zt[T_MK:T_MV]
    mvt_ref[0] = zt[T_MV:T_MG]
    mgt_ref[0] = zt[T_MG:T_ROWS]


def _inproj_weights(w_in_l):
    wt = w_in_l.T
    o = 0
    w_cq = wt[o:o + DSA_Q_LATENT]; o += DSA_Q_LATENT
    w_ckv = wt[o:o + DSA_KV_LATENT]; o += DSA_KV_LATENT
    w_kidx = wt[o:o + IDX_DIM]; o += IDX_DIM
    w_widx = wt[o:o + IDX_HEADS]; o += IDX_HEADS
    w_rwkv = wt[o:o + RWKV_IN]; o += RWKV_IN
    w_mq = wt[o:o + MLSTM_QK]; o += MLSTM_QK
    w_mk = wt[o:o + MLSTM_QK]; o += MLSTM_QK
    w_mv = wt[o:o + MLSTM_V]; o += MLSTM_V
    w_mo = wt[o:o + MLSTM_V]; o += MLSTM_V
    w_mi = wt[o:o + MLSTM_HEADS]; o += MLSTM_HEADS
    w_mf = wt[o:o + MLSTM_HEADS]; o += MLSTM_HEADS
    w_fox = wt[o:o + 3 * FOX_WIDTH]; o += 3 * FOX_WIDTH
    w_ff = wt[o:o + FOX_HEADS]; o += FOX_HEADS
    w_gate = wt[o:]

    def padr(w, n):
        return jnp.pad(w, ((0, n - w.shape[0]), (0, 0)))

    w_d = jnp.concatenate([w_ckv, padr(w_kidx, LANES)], axis=0)
    w_m = jnp.concatenate([w_mq, w_mk, w_mv, w_mo, padr(jnp.concatenate([w_mi, w_mf], axis=0), LANES)], axis=0)
    w_t = jnp.concatenate([w_cq, w_ckv, w_widx, padr(w_ff, 8), w_mk, w_mv, w_mi, w_mf], axis=0)
    assert w_t.shape[0] == T_ROWS
    cast = lambda w: w.astype(BF16)
    return dict(w_r=cast(w_rwkv), w_m=cast(w_m), w_f=cast(w_fox), w_d=cast(w_d), w_t=cast(w_t), w_gate=cast(w_gate))


def _inproj(x, g, wts, g_cq, g_ckv, g_kidx, fox_b_f, tm):
    b, s, d = x.shape
    nj = s // tm
    bf_col = jnp.pad(fox_b_f, (0, SUBLANES - FOX_HEADS)).reshape(SUBLANES, 1)
    row = lambda w: pl.BlockSpec((1, tm, w), lambda i, j: (i, j, 0))
    head = pl.BlockSpec((1, FOX_HEADS, tm, FOX_HEAD_DIM), lambda i, j: (i, 0, j, 0))
    col = lambda r: pl.BlockSpec((1, r, tm), lambda i, j: (i, 0, j))
    out_shape = [
        jax.ShapeDtypeStruct((b, s, RWKV_IN), F32),
        jax.ShapeDtypeStruct((b, s, M_COLS), F32),
        jax.ShapeDtypeStruct((b, FOX_HEADS, s, FOX_HEAD_DIM), BF16),
        jax.ShapeDtypeStruct((b, FOX_HEADS, s, FOX_HEAD_DIM), BF16),
        jax.ShapeDtypeStruct((b, FOX_HEADS, s, FOX_HEAD_DIM), BF16),
        jax.ShapeDtypeStruct((b, s, DSA_KV_LATENT), BF16),
        jax.ShapeDtypeStruct((b, s, IDX_K_COLS), BF16),
        jax.ShapeDtypeStruct((b, DSA_Q_LATENT, s), F32),
        jax.ShapeDtypeStruct((b, DSA_KV_LATENT, s), BF16),
        jax.ShapeDtypeStruct((b, IDX_HEADS, s), F32),
        jax.ShapeDtypeStruct((b, 8, s), F32),
        jax.ShapeDtypeStruct((b, MLSTM_QK, s), F32),
        jax.ShapeDtypeStruct((b, MLSTM_V, s), F32),
        jax.ShapeDtypeStruct((b, 8, s), F32),
    ]
    out_specs = [row(RWKV_IN), row(M_COLS), head, head, head, row(DSA_KV_LATENT), row(IDX_K_COLS),
                 col(DSA_Q_LATENT), col(DSA_KV_LATENT), col(IDX_HEADS), col(8), col(MLSTM_QK), col(MLSTM_V), col(8)]
    in_specs = [row(d), _full((1, d)), _full(wts['w_r'].shape), _full(wts['w_m'].shape), _full(wts['w_f'].shape),
                _full(wts['w_d'].shape), _full(wts['w_t'].shape),
                _full((DSA_Q_LATENT, 1)), _full((DSA_KV_LATENT, 1)), _full((1, DSA_KV_LATENT)), _full((1, IDX_DIM)),
                _full((SUBLANES, 1))]
    return pl.pallas_call(
        _inproj_kernel,
        out_shape=out_shape,
        grid=(b, nj),
        in_specs=in_specs,
        out_specs=out_specs,
        scratch_shapes=[pltpu.VMEM((SUBLANES, LANES), F32)],
        compiler_params=_cparams("arbitrary", "arbitrary"),
        name="inproj",
    )(x, g.reshape(1, d), wts['w_r'], wts['w_m'], wts['w_f'], wts['w_d'], wts['w_t'],
      g_cq.reshape(-1, 1), g_ckv.reshape(-1, 1), g_ckv.reshape(1, -1), g_kidx.reshape(1, -1), bf_col)


def _fox_kernel(q_ref, k_ref, v_ref, frow_ref, o_ref):
    qi = pl.program_id(1)
    tq = q_ref.shape[2]
    outs = []
    for hd in range(FOX_HEADS):
        q = q_ref[0, hd]

        def scores(j):
            start = pl.multiple_of(j * tq, tq)
            k = k_ref[0, hd, pl.ds(start, tq), :]
            v = v_ref[0, hd, pl.ds(start, tq), :]
            fk = frow_ref[0, hd:hd + 1, pl.ds(start, tq)]
            return _dot_nt(q, k) - fk, v

        def update(carry, sc, v):
            m, l, acc = carry
            m_new = jnp.maximum(m, jnp.max(sc, axis=-1, keepdims=True))
            alpha = jnp.exp(m - m_new)
            p = jnp.exp(sc - m_new)
            l = alpha * l + jnp.sum(p, axis=-1, keepdims=True)
            acc = alpha * acc + _dot(p.astype(BF16), v)
            return m_new, l, acc

        def body(j, carry):
            sc, v = scores(j)
            return update(carry, sc, v)

        init = (jnp.full((tq, 1), NEG_BIG, F32), jnp.zeros((tq, 1), F32), jnp.zeros((tq, FOX_HEAD_DIM), F32))
        carry = lax.fori_loop(0, qi, body, init)
        sc, v = scores(qi)
        r_i = lax.broadcasted_iota(jnp.int32, (tq, tq), 0)
        c_i = lax.broadcasted_iota(jnp.int32, (tq, tq), 1)
        sc = jnp.where(c_i <= r_i, sc, NEG_BIG)
        m, l, acc = update(carry, sc, v)
        outs.append(acc / l)
    o_ref[0] = jnp.concatenate(outs, axis=-1).astype(o_ref.dtype)


def _fox(q, k, v, frow, tq):
    b, nh, s, d = q.shape
    return pl.pallas_call(
        _fox_kernel,
        out_shape=jax.ShapeDtypeStruct((b, s, nh * d), BF16),
        grid=(b, s // tq),
        in_specs=[
            pl.BlockSpec((1, nh, tq, d), lambda i, j: (i, 0, j, 0)),
            pl.BlockSpec((1, nh, s, d), lambda i, j: (i, 0, 0, 0)),
            pl.BlockSpec((1, nh, s, d), lambda i, j: (i, 0, 0, 0)),
            pl.BlockSpec((1, SUBLANES, s), lambda i, j: (i, 0, 0)),
        ],
        out_specs=pl.BlockSpec((1, tq, nh * d), lambda i, j: (i, j, 0)),
        compiler_params=_cparams("arbitrary", "arbitrary"),
        name="fox",
    )(q, k, v, frow)


INT_MIN = -2 ** 31


IDX_SPLIT_K = ((0, 0), (0, 1), (1, 0), (0, 2), (1, 1), (2, 0))
IDX_K_COLS = 2 * LANES


def _split3_bf16(t):
    p0 = t.astype(BF16)
    r1 = t - p0.astype(F32)
    p1 = r1.astype(BF16)
    p2 = (r1 - p1.astype(F32)).astype(BF16)
    return p0, p1, p2


def _tree_sum(parts):
    while len(parts) > 1:
        parts = [parts[i] + parts[i + 1] for i in range(0, len(parts) - 1, 2)] + (parts[-1:] if len(parts) % 2 else [])
    return parts[0]


def _dsa_kernel(cqt_ref, wit_ref, ki6_ref, ckv_ref, ckvt_ref, wuqt_ref, wqit_ref, wuvt_ref, o_ref, key_s, *, ck):
    qb = pl.program_id(1)
    s = ki6_ref.shape[1]
    nq = cqt_ref.shape[2]
    topk = min(TOPK_MAX, s // 4)
    n_ck = (qb * nq + nq + ck - 1) // ck
    lat = DSA_KV_LATENT

    cq = cqt_ref[0]
    qi_t = _dot_hi(wqit_ref[...], cq)
    zeros_pad = jnp.zeros((IDX_K_COLS - len(IDX_SPLIT_K) * IDX_DIM, nq), BF16)
    q_blocks = []
    for hd in range(IDX_HEADS):
        pieces = _split3_bf16(qi_t[hd * IDX_DIM:(hd + 1) * IDX_DIM, :])
        q_blocks.append(jnp.concatenate([pieces[qp] for _, qp in IDX_SPLIT_K] + [zeros_pad], axis=0))
    q6 = jnp.concatenate(q_blocks, axis=1)
    wi = wit_ref[0] * IDX_DIM ** -0.5
    tpos = qb * nq + lax.broadcasted_iota(jnp.int32, (ck, nq), 1)
    row = lax.broadcasted_iota(jnp.int32, (ck, nq), 0)

    def chunk_start(c):
        return pl.multiple_of(c * ck, ck)

    def score_chunk(c, carry):
        c0 = chunk_start(c)
        ki6 = ki6_ref[0, pl.ds(c0, ck), :]
        score = None
        for hp in range(0, IDX_HEADS, 2):
            dots = _dot(ki6, q6[:, hp * nq:(hp + 2) * nq])
            for i in range(2):
                term = wi[hp + i:hp + i + 1, :] * jnp.maximum(dots[:, i * nq:(i + 1) * nq], 0.0)
                score = term if score is None else score + term
        bits = pltpu.bitcast(score, jnp.int32)
        key = jnp.where(bits < 0, bits ^ 0x7FFFFFFF, bits)
        key = jnp.where(score == 0.0, 0, key)
        key_s[pl.ds(c0, ck), :] = jnp.where(c0 + row <= tpos, key, INT_MIN)
        return carry

    n_pair = (n_ck + 1) // 2
    lax.fori_loop(0, n_pair, lambda j, carry: score_chunk(2 * j + 1, score_chunk(2 * j, carry)), 0)

    def count(mask_fn):
        def body(j, acc):
            parts = []
            for c in (2 * j, 2 * j + 1):
                c0 = chunk_start(c)
                ones = jnp.where(mask_fn(key_s[pl.ds(c0, ck), :], c0 + row), 1, 0)
                parts += [ones[i * SUBLANES:(i + 1) * SUBLANES] for i in range(ck // SUBLANES)]
            return acc + _tree_sum(parts)
        acc = lax.fori_loop(0, n_pair, body, jnp.zeros((SUBLANES, nq), jnp.int32))
        return jnp.sum(acc, axis=0, keepdims=True)

    def value_bit(i, carry):
        lo, n_lo = carry
        cand = lo + jnp.left_shift(jnp.int32(1), 31 - i)
        n_cand = count(lambda key, pos: key >= cand)
        ok = n_cand >= topk
        return jnp.where(ok, cand, lo), jnp.where(ok, n_cand, n_lo)

    thr, n_ge = lax.fori_loop(0, 32, value_bit, (jnp.full((1, nq), INT_MIN, jnp.int32),
                                                 jnp.full((1, nq), s, jnp.int32)))

    n_bits = s.bit_length()

    def tie_search():
        need = topk - count(lambda key, pos: key > thr)

        def index_bit(i, lo):
            cand = lo + jnp.left_shift(jnp.int32(1), n_bits - 1 - i)
            return jnp.where(count(lambda key, pos: (key == thr) & (pos < cand)) < need, cand, lo)

        return lax.fori_loop(0, n_bits, index_bit, jnp.zeros((1, nq), jnp.int32))

    surplus = jnp.max(jnp.where((n_ge > topk) & (thr > INT_MIN), 1, 0))
    last = lax.cond(surplus > 0, tie_search, lambda: jnp.full((1, nq), s, jnp.int32))

    q_t = (_dot(wuqt_ref[...], cq.astype(BF16)) * lat ** -0.5).astype(BF16)

    def attend(c, carry):
        c0 = chunk_start(c)
        key = key_s[pl.ds(c0, ck), :]
        pos = c0 + row
        sel = ((key > thr) | ((key == thr) & (pos <= last))) & (pos <= tpos)
        bias = jnp.where(sel, 0.0, NEG_BIG)
        ckv = ckv_ref[0, pl.ds(c0, ck), :]
        ckvt = ckvt_ref[0, :, pl.ds(c0, ck)]
        new = []
        for hd in range(DSA_HEADS):
            m, l, acc = carry[hd]
            lg = _dot(ckv, q_t[hd * lat:(hd + 1) * lat, :]) + bias
            m_new = jnp.maximum(m, jnp.max(lg, axis=0, keepdims=True))
            alpha = jnp.exp(m - m_new)
            p = jnp.exp(lg - m_new)
            new.append((m_new, alpha * l + jnp.sum(p, axis=0, keepdims=True),
                        alpha * acc + _dot(ckvt, p.astype(BF16))))
        return tuple(new)

    init = tuple((jnp.full((1, nq), NEG_BIG, F32), jnp.zeros((1, nq), F32), jnp.zeros((lat, nq), F32))
                 for _ in range(DSA_HEADS))
    final = lax.fori_loop(0, n_pair, lambda j, carry: attend(2 * j + 1, attend(2 * j, carry)), init)
    outs = [_dot(wuvt_ref[hd], (acc / l).astype(BF16)) for hd, (_, l, acc) in enumerate(final)]
    o_ref[0] = jnp.concatenate(outs, axis=0).T.astype(o_ref.dtype)


def _dsa(cqt, wit, ki6, ckv, ckvt, w_uq, w_qidx, w_uv, ck):
    b, c, s = cqt.shape
    nq = DSA_Q_BLOCK
    wuqt = w_uq.reshape(c, -1).T.astype(BF16)
    wqit = w_qidx.reshape(c, -1).T
    wuvt = jnp.transpose(w_uv, (0, 2, 1)).astype(BF16)
    return pl.pallas_call(
        functools.partial(_dsa_kernel, ck=ck),
        out_shape=jax.ShapeDtypeStruct((b, s, DSA_HEADS * DSA_HEAD_DIM), BF16),
        grid=(b, s // nq),
        in_specs=[pl.BlockSpec((1, c, nq), lambda i, j: (i, 0, j)),
                  pl.BlockSpec((1, IDX_HEADS, nq), lambda i, j: (i, 0, j)),
                  pl.BlockSpec((1, s, IDX_K_COLS), lambda i, j: (i, 0, 0)),
                  pl.BlockSpec((1, s, DSA_KV_LATENT), lambda i, j: (i, 0, 0)),
                  pl.BlockSpec((1, DSA_KV_LATENT, s), lambda i, j: (i, 0, 0)),
                  _full(wuqt.shape), _full(wqit.shape), _full(wuvt.shape)],
        out_specs=pl.BlockSpec((1, nq, DSA_HEADS * DSA_HEAD_DIM), lambda i, j: (i, j, 0)),
        scratch_shapes=[pltpu.VMEM((s, nq), jnp.int32)],
        compiler_params=_cparams("arbitrary", "arbitrary"),
        name="dsa",
    )(cqt, wit, ki6, ckv, ckvt, wuqt, wqit, wuvt)


Y_PAD = 8


def _head_ones(n, dtype):
    r_i = lax.broadcasted_iota(jnp.int32, (n, n), 0) // RWKV_HEAD_DIM
    c_i = lax.broadcasted_iota(jnp.int32, (n, n), 1) // RWKV_HEAD_DIM
    return jnp.where(r_i == c_i, 1.0, 0.0).astype(dtype)


def _split_bf16(t):
    hi = t.astype(BF16)
    lo = (t - hi.astype(F32)).astype(BF16)
    return hi, lo


def _rwkv_kernel(z_ref, mu_ref, w0_ref, wup_ref, a0_ref, aup_ref, gup_ref, kk_ref, ka_ref, rk_ref, lng_ref, lnb_ref,
                 o_ref, st_s, prev_s, step_s, y_s, bonus_s, gate_s, *, grp):
    c = pl.program_id(1)
    n_g, tc, _ = z_ref.shape
    hd, wd_ = RWKV_HEAD_DIM, RWKV_WIDTH

    @pl.when(c == 0)
    def _():
        st_s[...] = jnp.zeros_like(st_s)
        prev_s[...] = jnp.zeros_like(prev_s)

    ones_b = _head_ones(wd_, BF16)
    diag = jnp.where(lax.broadcasted_iota(jnp.int32, (hd, wd_), 0)
                     == lax.broadcasted_iota(jnp.int32, (hd, wd_), 1) % hd, 1.0, 0.0).astype(F32)

    def head_sum(t):
        hi, lo = _split_bf16(t)
        return _dot(hi, ones_b) + _dot(lo, ones_b)

    zs = []
    for g in range(n_g):
        z = z_ref[g]
        row = lax.broadcasted_iota(jnp.int32, z.shape, 0)
        z_prev = jnp.where(row == 0, prev_s[g], pltpu.roll(z, 1, 0))
        prev_s[g] = z[tc - 1:tc, :]
        zs.append(z + mu_ref[...] * (z_prev - z))
    z = jnp.concatenate(zs, axis=0)
    r = z[:, 0:wd_]
    k = z[:, wd_:2 * wd_]
    v = z[:, 2 * wd_:3 * wd_]
    o = 3 * wd_
    w_lora = z[:, o:o + RWKV_LORA_W]
    a_lora = z[:, o + RWKV_LORA_W:o + RWKV_LORA_W + RWKV_LORA_A]
    g_lora = z[:, o + RWKV_LORA_W + RWKV_LORA_A:]
    w_log = _log_sigmoid(w0_ref[...] + _dot_hi(jnp.tanh(w_lora), wup_ref[...])) - 0.5
    a = _sigmoid(a0_ref[...] + _dot_hi(a_lora, aup_ref[...]))
    gate_s[...] = _dot_hi(_sigmoid(g_lora), gup_ref[...])
    kk = k * kk_ref[...]
    kk = kk / jnp.maximum(jnp.sqrt(head_sum(kk * kk)), 1e-12)
    k = k * (1.0 + (a - 1.0) * ka_ref[...])
    step_s[0] = jnp.exp(-jnp.exp(w_log))
    step_s[1] = -kk
    step_s[2] = kk * a
    step_s[3] = k
    step_s[4] = r
    step_s[5] = v
    bonus_s[...] = head_sum(r * k * rk_ref[...]) * v

    n_grp = n_g // grp

    def rows(kind, g0, t):
        return jnp.concatenate(
            [jnp.broadcast_to(step_s[kind, pl.ds((g0 + i) * tc + t, 1), :], (hd, wd_)) for i in range(grp)], axis=0)

    def store_y(yb, g0, t_write):
        yb = yb * diag_g
        for i in range(grp):
            tile = jnp.sum(yb[i * hd:(i + 1) * hd].reshape(hd // SUBLANES, SUBLANES, wd_), axis=0)
            y_s[pl.ds(Y_PAD + (g0 + i) * tc + t_write, 1), :] = jnp.sum(tile, axis=0, keepdims=True)

    def rows_b(kind, g0, t):
        tiles = []
        for i in range(grp):
            r16 = jnp.broadcast_to(step_s[kind, pl.ds((g0 + i) * tc + t, 1), :], (BF16_SUBLANES, wd_)).astype(BF16)
            tiles.append(jnp.broadcast_to(r16[None], (hd // BF16_SUBLANES, BF16_SUBLANES, wd_)).reshape(hd, wd_))
        return jnp.concatenate(tiles, axis=0)

    def step(t, carry):
        for q in range(n_grp):
            g0 = q * grp
            st = st_s[q]
            st_b = st.astype(BF16)
            sa = _dot(st_b * rows_b(1, g0, t), ones_b)
            store_y(_dot(st_b * rows_b(4, g0, jnp.maximum(t - 1, 0)), ones_b), g0, t - 1)
            vb = _dot(rows_b(5, g0, t) * diag_b, ones_b)
            st_s[q] = st * rows(0, g0, t) + sa * rows(2, g0, t) + vb * rows(3, g0, t)
        return carry

    diag_g = jnp.concatenate([diag] * grp, axis=0)
    diag_b = diag_g.astype(BF16)
    lax.fori_loop(0, tc, step, 0, unroll=8)
    for q in range(n_grp):
        store_y(_dot(st_s[q].astype(BF16) * rows_b(4, q * grp, tc - 1), ones_b), q * grp, tc - 1)

    y = y_s[pl.ds(Y_PAD, n_g * tc), :]
    mean = head_sum(y) * (1.0 / hd)
    yc = y - mean
    var = head_sum(yc * yc) * (1.0 / hd)
    yn = yc * lax.rsqrt(var + RWKV_GN_EPS) * lng_ref[...] + lnb_ref[...]
    out = ((yn + bonus_s[...]) * gate_s[...]).astype(o_ref.dtype)
    for g in range(n_g):
        o_ref[g] = out[g * tc:(g + 1) * tc]


def _rwkv(zr, mu, w0, w_up, a0, a_up, g_up, k_k, k_a, r_k, ln_g, ln_b, n_g, grp, tc):
    b, s, zin = zr.shape
    wd_ = RWKV_WIDTH
    vec = lambda p: p.reshape(1, -1)
    params = [vec(mu), vec(w0), w_up, vec(a0), a_up, g_up, vec(k_k), vec(k_a), vec(r_k), vec(ln_g), vec(ln_b)]
    return pl.pallas_call(
        functools.partial(_rwkv_kernel, grp=grp),
        out_shape=jax.ShapeDtypeStruct((b, s, wd_), BF16),
        grid=(b // n_g, s // tc),
        in_specs=[pl.BlockSpec((n_g, tc, zin), lambda i, c: (i, c, 0))] + [_full(p.shape) for p in params],
        out_specs=pl.BlockSpec((n_g, tc, wd_), lambda i, c: (i, c, 0)),
        scratch_shapes=[pltpu.VMEM((n_g // grp, grp * RWKV_HEAD_DIM, wd_), F32), pltpu.VMEM((n_g, 1, zin), F32),
                        pltpu.VMEM((6, n_g * tc, wd_), F32), pltpu.VMEM((Y_PAD + n_g * tc, wd_), F32),
                        pltpu.VMEM((n_g * tc, wd_), F32), pltpu.VMEM((n_g * tc, wd_), F32)],
        compiler_params=_cparams("arbitrary", "arbitrary"),
        name="rwkv",
    )(zr, *params)


def _softcap(t):
    return GATE_SOFTCAP * jnp.tanh(t / GATE_SOFTCAP)


def _mlstm_kernel(zm_ref, mkt_ref, mvt_ref, mgt_ref, cw_row_ref, cb_row_ref, cw_col_ref, cb_col_ref,
                  bg_row_ref, bg_col_ref, ng_ref, o_ref, q_s, k_s, kt_s, gc_s, gr_s, yt_s):
    s = zm_ref.shape[1]
    nh, dk, dv, lc = MLSTM_HEADS, MLSTM_QK_DIM, MLSTM_V_DIM, MLSTM_CHUNK
    pair = 2 * lc

    qk = zm_ref[0, :, 0:2 * MLSTM_QK]
    pos_r = lax.broadcasted_iota(jnp.int32, qk.shape, 0)
    acc = cb_row_ref[...] + qk * cw_row_ref[CONV_WIDTH - 1:CONV_WIDTH, :]
    for r in range(1, CONV_WIDTH):
        sh = jnp.where(pos_r >= r, pltpu.roll(qk, r, 0), 0.0)
        acc = acc + sh * cw_row_ref[CONV_WIDTH - 1 - r:CONV_WIDTH - r, :]
    acc = _silu(acc)
    q_s[...] = acc[:, :MLSTM_QK] * dk ** -0.5
    k_s[...] = acc[:, MLSTM_QK:]
    kt = mkt_ref[0]
    pos_c = lax.broadcasted_iota(jnp.int32, kt.shape, 1)
    acc_t = cb_col_ref[...] + kt * cw_col_ref[:, CONV_WIDTH - 1:CONV_WIDTH]
    for r in range(1, CONV_WIDTH):
        sh = jnp.where(pos_c >= r, pltpu.roll(kt, r, 1), 0.0)
        acc_t = acc_t + sh * cw_col_ref[:, CONV_WIDTH - 1 - r:CONV_WIDTH - r]
    kt_s[...] = _silu(acc_t)

    gcol = _softcap(zm_ref[0, :, 2 * MLSTM_QK + 2 * MLSTM_V:] + bg_row_ref[...])
    lane = lax.broadcasted_iota(jnp.int32, gcol.shape, 1)
    gc_s[...] = jnp.where(lane < nh, gcol, _log_sigmoid(gcol))
    grow = _softcap(mgt_ref[0] + bg_col_ref[...])
    sub = lax.broadcasted_iota(jnp.int32, grow.shape, 0)
    gr_s[...] = jnp.where(sub < nh, grow, _log_sigmoid(grow))

    r_i = lax.broadcasted_iota(jnp.int32, (lc, lc), 0)
    c_i = lax.broadcasted_iota(jnp.int32, (lc, lc), 1)
    tri = jnp.where(c_i <= r_i, 1.0, 0.0).astype(F32)
    causal_t = r_i <= c_i
    ones_rows = jnp.ones((8, lc), F32)

    def chunk_pair(p, carry):
        base = pl.multiple_of(p * pair, pair)
        gr_slab = gr_s[:, pl.ds(base, pair)]
        kt_slab = kt_s[:, pl.ds(base, pair)]
        vt_slab = mvt_ref[0, :, pl.ds(base, pair)]
        outs = [[] for _ in range(nh)]
        for sc in range(2):
            r0 = base + sc * lc
            gcc = gc_s[pl.ds(r0, lc), :]
            grc = gr_slab[:, sc * lc:(sc + 1) * lc]
            bcum_col = _dot_hi(tri, gcc)
            bcum_row = _dot_nt_hi(grc, tri)
            qc = q_s[pl.ds(r0, lc), :]
            kc = k_s[pl.ds(r0, lc), :]
            new_carry = []
            for hd in range(nh):
                c_aug, m_prev = carry[hd]
                bc_c = bcum_col[:, nh + hd:nh + hd + 1]
                li_c = gcc[:, hd:hd + 1]
                bc_r = bcum_row[nh + hd:nh + hd + 1, :]
                li_r = grc[hd:hd + 1, :]
                gtot = bc_r[:, lc - 1:lc]
                d_t = jnp.where(causal_t, bc_r - bc_c + li_c, NEG_BIG)
                m_inter = bc_r + m_prev
                m_t = jnp.maximum(m_inter, jnp.max(d_t, axis=0, keepdims=True))
                q_h = qc[:, hd * dk:(hd + 1) * dk].astype(BF16)
                k_h = kc[:, hd * dk:(hd + 1) * dk].astype(BF16)
                kt_h = kt_slab[hd * dk:(hd + 1) * dk, sc * lc:(sc + 1) * lc].astype(BF16)
                vt_h = vt_slab[hd * dv:(hd + 1) * dv, sc * lc:(sc + 1) * lc]
                vt_aug = jnp.concatenate([vt_h, ones_rows], axis=0)
                s_t = _dot_nt(k_h, q_h)
                w_t = jnp.exp(d_t - m_t) * s_t
                s_inter = jnp.exp(m_inter - m_t)
                numden = _dot(vt_aug.astype(BF16), w_t.astype(BF16)) + s_inter * _dot_nt(c_aug.astype(BF16), q_h)
                den = numden[dv:dv + 1, :]
                outs[hd].append(numden[:dv, :] / jnp.maximum(jnp.abs(den), jnp.exp(-m_t)))
                a_log = gtot - bc_r + li_r
                a_max = jnp.max(a_log, axis=-1, keepdims=True)
                a_w = jnp.exp(a_log - a_max)
                kvn = _dot_nt((vt_aug * a_w).astype(BF16), kt_h)
                m_new = jnp.maximum(gtot + m_prev, a_max)
                s_old = jnp.exp(gtot + m_prev - m_new)
                s_new = jnp.exp(a_max - m_new)
                new_carry.append((s_old * c_aug + s_new * kvn, m_new))
            carry = tuple(new_carry)
        for hd in range(nh):
            yt_s[hd * dv:(hd + 1) * dv, pl.ds(base, pair)] = jnp.concatenate(outs[hd], axis=-1)
        return carry

    init = tuple((jnp.zeros((dv + 8, dk), F32), jnp.full((1, 1), NEG_BIG, F32)) for _ in range(nh))
    lax.fori_loop(0, s // pair, chunk_pair, init)

    parts = []
    for hd in range(nh):
        blk = yt_s[hd * dv:(hd + 1) * dv, :]
        parts.append(blk * lax.rsqrt(jnp.mean(blk * blk, axis=0, keepdims=True) + EPS))
    y = jnp.concatenate(parts, axis=0).T
    o_gate = _sigmoid(zm_ref[0, :, 2 * MLSTM_QK + MLSTM_V:2 * MLSTM_QK + 2 * MLSTM_V])
    o_ref[0] = (y * ng_ref[...] * o_gate).astype(o_ref.dtype)


def _mlstm(zm, mkt, mvt, mgt, conv_w, conv_b, b_i, b_f, norm_g):
    b, s, _ = zm.shape
    cw_col = conv_w[:, MLSTM_QK:].T
    cb_col = conv_b[MLSTM_QK:].reshape(-1, 1)
    bg = jnp.concatenate([b_i, b_f])
    bg_row = jnp.pad(bg, (0, LANES - 2 * MLSTM_HEADS)).reshape(1, LANES)
    bg_col = bg.reshape(-1, 1)
    per_b = lambda r, c: pl.BlockSpec((1, r, c), lambda i: (i, 0, 0))
    return pl.pallas_call(
        _mlstm_kernel,
        out_shape=jax.ShapeDtypeStruct((b, s, MLSTM_V), BF16),
        grid=(b,),
        in_specs=[per_b(s, M_COLS), per_b(MLSTM_QK, s), per_b(MLSTM_V, s), per_b(8, s),
                  _full(conv_w.shape), _full((1, 2 * MLSTM_QK)), _full(cw_col.shape), _full(cb_col.shape),
                  _full((1, LANES)), _full((8, 1)), _full((1, MLSTM_V))],
        out_specs=per_b(s, MLSTM_V),
        scratch_shapes=[pltpu.VMEM((s, MLSTM_QK), F32), pltpu.VMEM((s, MLSTM_QK), F32), pltpu.VMEM((MLSTM_QK, s), F32),
                        pltpu.VMEM((s, LANES), F32), pltpu.VMEM((8, s), F32), pltpu.VMEM((MLSTM_V, s), F32)],
        compiler_params=_cparams("arbitrary"),
        name="mlstm",
    )(zm, mkt, mvt, mgt, conv_w, conv_b.reshape(1, -1), cw_col, cb_col, bg_row, bg_col, norm_g.reshape(1, -1))


def _merge_kernel(x_ref, g_ref, y0_ref, y1_ref, y2_ref, y3_ref, wg_ref, wb_ref, wo_ref, o_ref):
    x = x_ref[...]
    d = x.shape[1]
    h = _rms_rows(x, g_ref[...]).astype(BF16)
    merged = None
    for n, y_ref in enumerate((y0_ref, y1_ref, y2_ref, y3_ref)):
        gate = _sigmoid(_dot_nt(h, wg_ref[n * d:(n + 1) * d, :]))
        term = gate * _dot(y_ref[...], wb_ref[n])
        merged = term if merged is None else merged + term
    o_ref[...] = x + _dot(merged.astype(BF16), wo_ref[...])


def _merge(x2, g, ys, w_gate, w_branch, w_out, tm):
    t, d = x2.shape
    row = lambda w: pl.BlockSpec((tm, w), lambda i: (i, 0))
    return pl.pallas_call(
        _merge_kernel,
        out_shape=jax.ShapeDtypeStruct((t, d), F32),
        grid=(t // tm,),
        in_specs=[row(d), _full((1, d))] + [row(BRANCH_WIDTH)] * N_BRANCH
                 + [_full(w_gate.shape), _full(w_branch.shape), _full(w_out.shape)],
        out_specs=row(d),
        compiler_params=_cparams("arbitrary"),
        name="merge",
    )(x2, g.reshape(1, d), *ys, w_gate, w_branch, w_out)


def _ffn_kernel(x_ref, g_ref, w1_ref, w3_ref, w2_ref, o_ref):
    x = x_ref[...]
    h = _rms_rows(x, g_ref[...]).astype(BF16)
    u = _silu(_dot(h, w1_ref[...])) * _dot(h, w3_ref[...])
    o_ref[...] = x + _dot(u.astype(BF16), w2_ref[...])


def _ffn(x2, g, w1, w3, w2, tm):
    t, d = x2.shape
    row = pl.BlockSpec((tm, d), lambda i: (i, 0))
    return pl.pallas_call(
        _ffn_kernel,
        out_shape=jax.ShapeDtypeStruct((t, d), F32),
        grid=(t // tm,),
        in_specs=[row, _full((1, d)), _full(w1.shape), _full(w3.shape), _full(w2.shape)],
        out_specs=row,
        compiler_params=_cparams("arbitrary"),
        name="ffn",
    )(x2, g.reshape(1, d), w1, w3, w2)


MOE_CAP = 144


def _moe_router_kernel(x_ref, g_ref, wr_ref, h_ref, rank_ref, gate_ref, rrow_ref, cnt_ref):
    tm = x_ref.shape[0]
    h = _rms_rows(x_ref[...], g_ref[...])
    h_ref[...] = h.astype(BF16)
    logits = _dot_hi(h, wr_ref[...])
    lane = lax.broadcasted_iota(jnp.int32, logits.shape, 1)
    logits = jnp.where(lane < N_EXPERTS, logits, -jnp.inf)
    v1 = jnp.max(logits, axis=-1, keepdims=True)
    i1 = jnp.min(jnp.where(logits == v1, lane, LANES), axis=-1, keepdims=True)
    rest = jnp.where(lane == i1, -jnp.inf, logits)
    v2 = jnp.max(rest, axis=-1, keepdims=True)
    i2 = jnp.min(jnp.where(rest == v2, lane, LANES), axis=-1, keepdims=True)
    e2 = jnp.exp(v2 - v1)
    gate_ref[...] = jnp.where(lane == i1, 1.0 / (1.0 + e2), 0.0) + jnp.where(lane == i2, e2 / (1.0 + e2), 0.0)
    routed = jnp.where((lane == i1) | (lane == i2), 1.0, 0.0)
    r_i = lax.broadcasted_iota(jnp.int32, (tm, tm), 0)
    c_i = lax.broadcasted_iota(jnp.int32, (tm, tm), 1)
    rank = _dot(jnp.where(c_i < r_i, 1.0, 0.0).astype(BF16), routed.astype(BF16))
    rank = jnp.where(routed > 0.0, rank, -1.0)
    rank_ref[...] = rank
    eye = jnp.where(lax.broadcasted_iota(jnp.int32, (LANES, LANES), 0)
                    == lax.broadcasted_iota(jnp.int32, (LANES, LANES), 1), 1.0, 0.0)
    rrow_ref[0] = _dot_nt_hi(eye, rank)[0:N_EXPERTS, :]
    cnt_ref[0] = jnp.broadcast_to(jnp.sum(routed, axis=0, keepdims=True), (8, LANES))


def _moe_expert_kernel(cnt_ref, y_ref, h_ref, rrow_ref, rank_ref, gate_ref, w1_ref, w3_ref, w2_ref, o_ref, acc_s):
    e = pl.program_id(0)
    i = pl.program_id(1)
    tm = h_ref.shape[0]
    cap = MOE_CAP
    lane = lax.broadcasted_iota(jnp.int32, (tm, LANES), 1)
    rank_col = jnp.sum(jnp.where(lane == e, rank_ref[...], 0.0), axis=-1, keepdims=True)
    gate_col = jnp.sum(jnp.where(lane == e, gate_ref[...], 0.0), axis=-1, keepdims=True)
    rank_row = rrow_ref[0, pl.ds(e, 1), :]
    slot_r = lax.broadcasted_iota(jnp.int32, (cap, tm), 0).astype(F32)
    slot_c = lax.broadcasted_iota(jnp.int32, (tm, cap), 1).astype(F32)
    acc_s[...] = jnp.zeros_like(acc_s)

    def one_pass(p, carry):
        base = (p * cap).astype(F32)
        gather = jnp.where(rank_row - base == slot_r, 1.0, 0.0).astype(BF16)
        xc = _dot(gather, h_ref[...]).astype(BF16)
        u = _silu(_dot(xc, w1_ref[0])) * _dot(xc, w3_ref[0])
        yc_hi, yc_lo = _split_bf16(_dot(u.astype(BF16), w2_ref[0]))
        scatter = jnp.where(rank_col - base == slot_c, 1.0, 0.0).astype(BF16)
        acc_s[...] += _dot(scatter, yc_hi) + _dot(scatter, yc_lo)
        return carry

    n_pass = (cnt_ref[i, e] + cap - 1) // cap
    lax.fori_loop(0, n_pass, one_pass, 0)
    o_ref[...] = y_ref[...] + gate_col * acc_s[...]


def _moe(x2, g, router, w1, w3, w2, tm):
    t, d = x2.shape
    n_e, _, f = w1.shape
    n_t = t // tm
    router_p = jnp.pad(router, ((0, 0), (0, LANES - n_e)))
    row = lambda w: pl.BlockSpec((tm, w), lambda i: (i, 0))
    h, rank, gate, rrow, cnt = pl.pallas_call(
        _moe_router_kernel,
        out_shape=[jax.ShapeDtypeStruct((t, d), BF16), jax.ShapeDtypeStruct((t, LANES), F32),
                   jax.ShapeDtypeStruct((t, LANES), F32), jax.ShapeDtypeStruct((n_t, N_EXPERTS, tm), F32),
                   jax.ShapeDtypeStruct((n_t, 8, LANES), F32)],
        grid=(n_t,),
        in_specs=[row(d), _full((1, d)), _full((d, LANES))],
        out_specs=[row(d), row(LANES), row(LANES), pl.BlockSpec((1, N_EXPERTS, tm), lambda i: (i, 0, 0)),
                   pl.BlockSpec((1, 8, LANES), lambda i: (i, 0, 0))],
        compiler_params=_cparams("arbitrary"),
        name="moe_router",
    )(x2, g.reshape(1, d), router_p)
    counts = cnt[:, 0, :n_e].astype(jnp.int32)
    tile = lambda w: pl.BlockSpec((tm, w), lambda e, i, c: (i, 0))
    weight = lambda r, c_: pl.BlockSpec((1, r, c_), lambda e, i, c: (e, 0, 0), pipeline_mode=pl.Buffered(1))
    return pl.pallas_call(
        _moe_expert_kernel,
        out_shape=jax.ShapeDtypeStruct((t, d), F32),
        grid_spec=pltpu.PrefetchScalarGridSpec(
            num_scalar_prefetch=1,
            grid=(n_e, n_t),
            in_specs=[tile(d), tile(d), pl.BlockSpec((1, N_EXPERTS, tm), lambda e, i, c: (i, 0, 0)),
                      tile(LANES), tile(LANES), weight(d, f), weight(d, f), weight(f, d)],
            out_specs=tile(d),
            scratch_shapes=[pltpu.VMEM((tm, d), F32)]),
        input_output_aliases={1: 0},
        compiler_params=_cparams("arbitrary", "arbitrary"),
        name="moe_experts",
    )(counts, x2, h, rrow, rank, gate, w1, w3, w2)


def _final_norm_kernel(x_ref, g_ref, o_ref):
    o_ref[...] = _rms_rows(x_ref[...], g_ref[...])


def _final_norm(x2, g, tm):
    t, d = x2.shape
    row = pl.BlockSpec((tm, d), lambda i: (i, 0))
    return pl.pallas_call(
        _final_norm_kernel,
        out_shape=jax.ShapeDtypeStruct((t, d), F32),
        grid=(t // tm,),
        in_specs=[row, _full((1, d))],
        out_specs=row,
        compiler_params=_cparams("arbitrary"),
        name="final_norm",
    )(x2, g.reshape(1, d))


def _tiles(b, s):
    tiles = dict(
        rows=min(512, s),
        fox_q=min(1024, s),
        dsa_keys=min(256, s // 2),
        rwkv_seqs=min(16, b),
        rwkv_group=min(4, b),
        rwkv_steps=64,
    )
    assert s % tiles['rows'] == 0 and s % tiles['fox_q'] == 0 and s % (2 * tiles['dsa_keys']) == 0
    assert s % DSA_Q_BLOCK == 0 and s % (2 * MLSTM_CHUNK) == 0 and s % tiles['rwkv_steps'] == 0
    assert b % tiles['rwkv_seqs'] == 0 and tiles['rwkv_seqs'] % tiles['rwkv_group'] == 0
    return tiles


def kernel(x, norm_mix_g, w_in, dsa_g_cq, dsa_g_ckv, dsa_g_kidx, dsa_w_uq, dsa_w_qidx, dsa_w_uv, rwkv_mu, rwkv_w0, rwkv_w_up, rwkv_a0, rwkv_a_up, rwkv_g_up, rwkv_k_k, rwkv_k_a, rwkv_r_k, rwkv_ln_g, rwkv_ln_b, mlstm_conv_w, mlstm_conv_b, mlstm_b_i, mlstm_b_f, mlstm_norm_g, fox_b_f, w_branch, w_out, norm_ffn_g, ffn_w1, ffn_w3, ffn_w2, moe_router, moe_w1, moe_w3, moe_w2, final_norm_g):
    b, s, d = x.shape
    depth = w_in.shape[0]
    t = b * s
    tl = _tiles(b, s)
    for l in range(depth):
        wts = _inproj_weights(w_in[l])
        (zr, zm, fq, fk, fv, ckv, ki6, cqt, ckvt, wit, frow, mkt, mvt, mgt) = _inproj(
            x, norm_mix_g[l], wts, dsa_g_cq[l], dsa_g_ckv[l], dsa_g_kidx[l], fox_b_f[l], tl['rows'])
        y_fox = _fox(fq, fk, fv, frow, tl['fox_q'])
        y_dsa = _dsa(cqt, wit, ki6, ckv, ckvt, dsa_w_uq[l], dsa_w_qidx[l], dsa_w_uv[l], tl['dsa_keys'])
        y_rwkv = _rwkv(zr, rwkv_mu[l], rwkv_w0[l], rwkv_w_up[l], rwkv_a0[l], rwkv_a_up[l], rwkv_g_up[l], rwkv_k_k[l],
                       rwkv_k_a[l], rwkv_r_k[l], rwkv_ln_g[l], rwkv_ln_b[l], tl['rwkv_seqs'], tl['rwkv_group'],
                       tl['rwkv_steps'])
        y_mlstm = _mlstm(zm, mkt, mvt, mgt, mlstm_conv_w[l], mlstm_conv_b[l], mlstm_b_i[l], mlstm_b_f[l],
                         mlstm_norm_g[l])
        ys = tuple(y.reshape(t, BRANCH_WIDTH) for y in (y_dsa, y_rwkv, y_mlstm, y_fox))
        x2 = _merge(x.reshape(t, d), norm_mix_g[l], ys, wts['w_gate'], w_branch[l].astype(BF16),
                    w_out[l].astype(BF16), tl['rows'])
        j = l // 2
        if l % 2 == 0:
            x2 = _ffn(x2, norm_ffn_g[l], ffn_w1[j].astype(BF16), ffn_w3[j].astype(BF16), ffn_w2[j].astype(BF16),
                      tl['rows'])
        else:
            x2 = _moe(x2, norm_ffn_g[l], moe_router[j], moe_w1[j].astype(BF16), moe_w3[j].astype(BF16),
                      moe_w2[j].astype(BF16), tl['rows'])
        x = x2.reshape(b, s, d)
    return _final_norm(x.reshape(t, d), final_norm_g, tl['rows']).reshape(b, s, d)
```

```python
import functools

import jax
import jax.numpy as jnp
from jax import lax
from jax.experimental import pallas as pl
from jax.experimental.pallas import tpu as pltpu

F32 = jnp.float32
BF16 = jnp.bfloat16
HIGHEST = lax.Precision.HIGHEST

EPS = 1e-6
NEG_BIG = -1e30

N_BRANCH = 4
BRANCH_WIDTH = 256
DSA_HEADS = 4
DSA_HEAD_DIM = 64
DSA_Q_LATENT = 128
DSA_KV_LATENT = 128
IDX_HEADS = 8
IDX_DIM = 32
TOPK_MAX = 256
DSA_Q_BLOCK = 128

RWKV_HEADS = 4
RWKV_HEAD_DIM = 64
RWKV_WIDTH = RWKV_HEADS * RWKV_HEAD_DIM
RWKV_LORA_W = 64
RWKV_LORA_A = 64
RWKV_LORA_G = 128
RWKV_GN_EPS = 64e-5
RWKV_IN = 3 * RWKV_WIDTH + RWKV_LORA_W + RWKV_LORA_A + RWKV_LORA_G

MLSTM_HEADS = 4
MLSTM_QK_DIM = 32
MLSTM_V_DIM = 64
MLSTM_CHUNK = 256
CONV_WIDTH = 4
GATE_SOFTCAP = 15.0
MLSTM_QK = MLSTM_HEADS * MLSTM_QK_DIM
MLSTM_V = MLSTM_HEADS * MLSTM_V_DIM

FOX_HEADS = 4
FOX_HEAD_DIM = 64
FOX_WIDTH = FOX_HEADS * FOX_HEAD_DIM

N_EXPERTS = 8

VMEM_LIMIT_BYTES = 56 * 1024 * 1024
LANES = 128
SUBLANES = 8
BF16_SUBLANES = 16


def _cparams(*sem):
    return pltpu.CompilerParams(dimension_semantics=sem, vmem_limit_bytes=VMEM_LIMIT_BYTES)


def _dot(a, b):
    return jnp.dot(a, b, preferred_element_type=F32)


def _dot_hi(a, b):
    return jnp.dot(a, b, preferred_element_type=F32, precision=HIGHEST)


def _dot_nt(a, b):
    return lax.dot_general(a, b, (((1,), (1,)), ((), ())), preferred_element_type=F32)


def _dot_nt_hi(a, b):
    return lax.dot_general(a, b, (((1,), (1,)), ((), ())), preferred_element_type=F32, precision=HIGHEST)


def _log_sigmoid(t):
    return jnp.minimum(t, 0.0) - jnp.log1p(jnp.exp(-jnp.abs(t)))


def _sigmoid(t):
    return 1.0 / (1.0 + jnp.exp(-t))


def _silu(t):
    return t * _sigmoid(t)


def _rms_rows(t, g_row):
    return t * lax.rsqrt(jnp.mean(t * t, axis=-1, keepdims=True) + EPS) * g_row


def _rms_cols(t, g_col):
    return t * lax.rsqrt(jnp.mean(t * t, axis=0, keepdims=True) + EPS) * g_col


def _full(shape):
    n = len(shape)
    return pl.BlockSpec(shape, lambda *_: (0,) * n)


T_CQ = 0
T_CKV = T_CQ + DSA_Q_LATENT
T_WIDX = T_CKV + DSA_KV_LATENT
T_FOXF = T_WIDX + IDX_HEADS
T_MK = T_FOXF + SUBLANES
T_MV = T_MK + MLSTM_QK
T_MG = T_MV + MLSTM_V
T_ROWS = T_MG + 2 * MLSTM_HEADS
M_COLS = 2 * MLSTM_QK + 2 * MLSTM_V + LANES


def _inproj_kernel(x_ref, g_ref, wr_ref, wm_ref, wf_ref, wd_ref, wt_ref,
                   gcq_ref, gckv_col_ref, gckv_row_ref, gki_ref, bf_col_ref,
                   zr_ref, zm_ref, q_ref, k_ref, v_ref, ckv_ref, ki_ref,
                   cqt_ref, ckvt_ref, wit_ref, frow_ref, mkt_ref, mvt_ref, mgt_ref,
                   carry_col):
    j = pl.program_id(1)
    tm = x_ref.shape[1]

    @pl.when(j == 0)
    def _():
        carry_col[...] = jnp.zeros_like(carry_col)

    x = x_ref[0]
    h = _rms_rows(x, g_ref[...]).astype(BF16)

    zr_ref[0] = _dot_nt(h, wr_ref[...])
    zm_ref[0] = _dot_nt(h, wm_ref[...])

    zf = _dot_nt(h, wf_ref[...])
    for hd in range(FOX_HEADS):
        lo = hd * FOX_HEAD_DIM
        q_ref[0, hd] = (zf[:, lo:lo + FOX_HEAD_DIM] * FOX_HEAD_DIM ** -0.5).astype(BF16)
        k_ref[0, hd] = zf[:, FOX_WIDTH + lo:FOX_WIDTH + lo + FOX_HEAD_DIM].astype(BF16)
        v_ref[0, hd] = zf[:, 2 * FOX_WIDTH + lo:2 * FOX_WIDTH + lo + FOX_HEAD_DIM].astype(BF16)

    zd = _dot_nt(h, wd_ref[...])
    ckv_ref[0] = _rms_rows(zd[:, :LANES], gckv_row_ref[...]).astype(BF16)
    ki_pieces = [p.astype(F32) for p in _split3_bf16(_rms_rows(zd[:, LANES:LANES + IDX_DIM], gki_ref[...]))]
    ki_pad = jnp.zeros((tm, IDX_K_COLS - len(IDX_SPLIT_K) * IDX_DIM), F32)
    ki_ref[0] = jnp.concatenate([ki_pieces[kp] for kp, _ in IDX_SPLIT_K] + [ki_pad], axis=1).astype(BF16)

    zt = _dot_nt(wt_ref[...], h)
    cqt_ref[0] = _rms_cols(zt[T_CQ:T_CKV], gcq_ref[...])
    ckvt_ref[0] = _rms_cols(zt[T_CKV:T_WIDX], gckv_col_ref[...]).astype(BF16)
    wit_ref[0] = zt[T_WIDX:T_FOXF] * IDX_HEADS ** -0.5
    lf_row = _log_sigmoid(zt[T_FOXF:T_MK] + bf_col_ref[...])
    r_i = lax.broadcasted_iota(jnp.int32, (tm, tm), 0)
    c_i = lax.broadcasted_iota(jnp.int32, (tm, tm), 1)
    tri = jnp.where(c_i <= r_i, 1.0, 0.0).astype(BF16)
    cum_row = sum(_dot_nt(p, tri) for p in _split3_bf16(lf_row)) + carry_col[:, 0:1]
    frow_ref[0] = cum_row
    carry_col[...] = jnp.broadcast_to(cum_row[:, tm - 1:tm], carry_col.shape)
    mkt_ref[0] = zt[T_MK:T_MV]
    mvt_ref[0] = zt[T_MV:T_MG]
    mgt_ref[0] = zt[T_MG:T_ROWS]


def _inproj_weights(w_in_l):
    wt = w_in_l.T
    o = 0
    w_cq = wt[o:o + DSA_Q_LATENT]; o += DSA_Q_LATENT
    w_ckv = wt[o:o + DSA_KV_LATENT]; o += DSA_KV_LATENT
    w_kidx = wt[o:o + IDX_DIM]; o += IDX_DIM
    w_widx = wt[o:o + IDX_HEADS]; o += IDX_HEADS
    w_rwkv = wt[o:o + RWKV_IN]; o += RWKV_IN
    w_mq = wt[o:o + MLSTM_QK]; o += MLSTM_QK
    w_mk = wt[o:o + MLSTM_QK]; o += MLSTM_QK
    w_mv = wt[o:o + MLSTM_V]; o += MLSTM_V
    w_mo = wt[o:o + MLSTM_V]; o += MLSTM_V
    w_mi = wt[o:o + MLSTM_HEADS]; o += MLSTM_HEADS
    w_mf = wt[o:o + MLSTM_HEADS]; o += MLSTM_HEADS
    w_fox = wt[o:o + 3 * FOX_WIDTH]; o += 3 * FOX_WIDTH
    w_ff = wt[o:o + FOX_HEADS]; o += FOX_HEADS
    w_gate = wt[o:]

    def padr(w, n):
        return jnp.pad(w, ((0, n - w.shape[0]), (0, 0)))

    w_d = jnp.concatenate([w_ckv, padr(w_kidx, LANES)], axis=0)
    w_m = jnp.concatenate([w_mq, w_mk, w_mv, w_mo, padr(jnp.concatenate([w_mi, w_mf], axis=0), LANES)], axis=0)
    w_t = jnp.concatenate([w_cq, w_ckv, w_widx, padr(w_ff, 8), w_mk, w_mv, w_mi, w_mf], axis=0)
    assert w_t.shape[0] == T_ROWS
    cast = lambda w: w.astype(BF16)
    return dict(w_r=cast(w_rwkv), w_m=cast(w_m), w_f=cast(w_fox), w_d=cast(w_d), w_t=cast(w_t), w_gate=cast(w_gate))


def _inproj(x, g, wts, g_cq, g_ckv, g_kidx, fox_b_f, tm):
    b, s, d = x.shape
    nj = s // tm
    bf_col = jnp.pad(fox_b_f, (0, SUBLANES - FOX_HEADS)).reshape(SUBLANES, 1)
    row = lambda w: pl.BlockSpec((1, tm, w), lambda i, j: (i, j, 0))
    head = pl.BlockSpec((1, FOX_HEADS, tm, FOX_HEAD_DIM), lambda i, j: (i, 0, j, 0))
    col = lambda r: pl.BlockSpec((1, r, tm), lambda i, j: (i, 0, j))
    out_shape = [
        jax.ShapeDtypeStruct((b, s, RWKV_IN), F32),
        jax.ShapeDtypeStruct((b, s, M_COLS), F32),
        jax.ShapeDtypeStruct((b, FOX_HEADS, s, FOX_HEAD_DIM), BF16),
        jax.ShapeDtypeStruct((b, FOX_HEADS, s, FOX_HEAD_DIM), BF16),
        jax.ShapeDtypeStruct((b, FOX_HEADS, s, FOX_HEAD_DIM), BF16),
        jax.ShapeDtypeStruct((b, s, DSA_KV_LATENT), BF16),
        jax.ShapeDtypeStruct((b, s, IDX_K_COLS), BF16),
        jax.ShapeDtypeStruct((b, DSA_Q_LATENT, s), F32),
        jax.ShapeDtypeStruct((b, DSA_KV_LATENT, s), BF16),
        jax.ShapeDtypeStruct((b, IDX_HEADS, s), F32),
        jax.ShapeDtypeStruct((b, 8, s), F32),
        jax.ShapeDtypeStruct((b, MLSTM_QK, s), F32),
        jax.ShapeDtypeStruct((b, MLSTM_V, s), F32),
        jax.ShapeDtypeStruct((b, 8, s), F32),
    ]
    out_specs = [row(RWKV_IN), row(M_COLS), head, head, head, row(DSA_KV_LATENT), row(IDX_K_COLS),
                 col(DSA_Q_LATENT), col(DSA_KV_LATENT), col(IDX_HEADS), col(8), col(MLSTM_QK), col(MLSTM_V), col(8)]
    in_specs = [row(d), _full((1, d)), _full(wts['w_r'].shape), _full(wts['w_m'].shape), _full(wts['w_f'].shape),
                _full(wts['w_d'].shape), _full(wts['w_t'].shape),
                _full((DSA_Q_LATENT, 1)), _full((DSA_KV_LATENT, 1)), _full((1, DSA_KV_LATENT)), _full((1, IDX_DIM)),
                _full((SUBLANES, 1))]
    return pl.pallas_call(
        _inproj_kernel,
        out_shape=out_shape,
        grid=(b, nj),
        in_specs=in_specs,
        out_specs=out_specs,
        scratch_shapes=[pltpu.VMEM((SUBLANES, LANES), F32)],
        compiler_params=_cparams("arbitrary", "arbitrary"),
        name="inproj",
    )(x, g.reshape(1, d), wts['w_r'], wts['w_m'], wts['w_f'], wts['w_d'], wts['w_t'],
      g_cq.reshape(-1, 1), g_ckv.reshape(-1, 1), g_ckv.reshape(1, -1), g_kidx.reshape(1, -1), bf_col)


def _fox_kernel(q_ref, k_ref, v_ref, frow_ref, o_ref):
    qi = pl.program_id(1)
    tq = q_ref.shape[2]
    outs = []
    for hd in range(FOX_HEADS):
        q = q_ref[0, hd]

        def scores(j):
            start = pl.multiple_of(j * tq, tq)
            k = k_ref[0, hd, pl.ds(start, tq), :]
            v = v_ref[0, hd, pl.ds(start, tq), :]
            fk = frow_ref[0, hd:hd + 1, pl.ds(start, tq)]
            return _dot_nt(q, k) - fk, v

        def update(carry, sc, v):
            m, l, acc = carry
            m_new = jnp.maximum(m, jnp.max(sc, axis=-1, keepdims=True))
            alpha = jnp.exp(m - m_new)
            p = jnp.exp(sc - m_new)
            l = alpha * l + jnp.sum(p, axis=-1, keepdims=True)
            acc = alpha * acc + _dot(p.astype(BF16), v)
            return m_new, l, acc

        def body(j, carry):
            sc, v = scores(j)
            return update(carry, sc, v)

        init = (jnp.full((tq, 1), NEG_BIG, F32), jnp.zeros((tq, 1), F32), jnp.zeros((tq, FOX_HEAD_DIM), F32))
        carry = lax.fori_loop(0, qi, body, init)
        sc, v = scores(qi)
        r_i = lax.broadcasted_iota(jnp.int32, (tq, tq), 0)
        c_i = lax.broadcasted_iota(jnp.int32, (tq, tq), 1)
        sc = jnp.where(c_i <= r_i, sc, NEG_BIG)
        m, l, acc = update(carry, sc, v)
        outs.append(acc / l)
    o_ref[0] = jnp.concatenate(outs, axis=-1).astype(o_ref.dtype)


def _fox(q, k, v, frow, tq):
    b, nh, s, d = q.shape
    return pl.pallas_call(
        _fox_kernel,
        out_shape=jax.ShapeDtypeStruct((b, s, nh * d), BF16),
        grid=(b, s // tq),
        in_specs=[
            pl.BlockSpec((1, nh, tq, d), lambda i, j: (i, 0, j, 0)),
            pl.BlockSpec((1, nh, s, d), lambda i, j: (i, 0, 0, 0)),
            pl.BlockSpec((1, nh, s, d), lambda i, j: (i, 0, 0, 0)),
            pl.BlockSpec((1, SUBLANES, s), lambda i, j: (i, 0, 0)),
        ],
        out_specs=pl.BlockSpec((1, tq, nh * d), lambda i, j: (i, j, 0)),
        compiler_params=_cparams("arbitrary", "arbitrary"),
        name="fox",
    )(q, k, v, frow)


INT_MIN = -2 ** 31


IDX_SPLIT_K = ((0, 0), (0, 1), (1, 0), (0, 2), (1, 1), (2, 0))
IDX_K_COLS = 2 * LANES


def _split3_bf16(t):
    p0 = t.astype(BF16)
    r1 = t - p0.astype(F32)
    p1 = r1.astype(BF16)
    p2 = (r1 - p1.astype(F32)).astype(BF16)
    return p0, p1, p2


def _tree_sum(parts):
    while len(parts) > 1:
        parts = [parts[i] + parts[i + 1] for i in range(0, len(parts) - 1, 2)] + (parts[-1:] if len(parts) % 2 else [])
    return parts[0]


def _dsa_kernel(cqt_ref, wit_ref, ki6_ref, ckv_ref, ckvt_ref, wuqt_ref, wqit_ref, wuvt_ref, o_ref, key_s, *, ck):
    qb = pl.program_id(1)
    s = ki6_ref.shape[1]
    nq = cqt_ref.shape[2]
    topk = min(TOPK_MAX, s // 4)
    n_ck = (qb * nq + nq + ck - 1) // ck
    lat = DSA_KV_LATENT

    cq = cqt_ref[0]
    qi_t = _dot_hi(wqit_ref[...], cq)
    zeros_pad = jnp.zeros((IDX_K_COLS - len(IDX_SPLIT_K) * IDX_DIM, nq), BF16)
    q_blocks = []
    for hd in range(IDX_HEADS):
        pieces = _split3_bf16(qi_t[hd * IDX_DIM:(hd + 1) * IDX_DIM, :])
        q_blocks.append(jnp.concatenate([pieces[qp] for _, qp in IDX_SPLIT_K] + [zeros_pad], axis=0))
    q6 = jnp.concatenate(q_blocks, axis=1)
    wi = wit_ref[0] * IDX_DIM ** -0.5
    tpos = qb * nq + lax.broadcasted_iota(jnp.int32, (ck, nq), 1)
    row = lax.broadcasted_iota(jnp.int32, (ck, nq), 0)

    def chunk_start(c):
        return pl.multiple_of(c * ck, ck)

    def score_chunk(c, carry):
        c0 = chunk_start(c)
        ki6 = ki6_ref[0, pl.ds(c0, ck), :]
        score = None
        for hp in range(0, IDX_HEADS, 2):
            dots = _dot(ki6, q6[:, hp * nq:(hp + 2) * nq])
            for i in range(2):
                term = wi[hp + i:hp + i + 1, :] * jnp.maximum(dots[:, i * nq:(i + 1) * nq], 0.0)
                score = term if score is None else score + term
        bits = pltpu.bitcast(score, jnp.int32)
        key = jnp.where(bits < 0, bits ^ 0x7FFFFFFF, bits)
        key = jnp.where(score == 0.0, 0, key)
        key_s[pl.ds(c0, ck), :] = jnp.where(c0 + row <= tpos, key, INT_MIN)
        return carry

    n_pair = (n_ck + 1) // 2
    lax.fori_loop(0, n_pair, lambda j, carry: score_chunk(2 * j + 1, score_chunk(2 * j, carry)), 0)

    def count(mask_fn):
        def body(j, acc):
            parts = []
            for c in (2 * j, 2 * j + 1):
                c0 = chunk_start(c)
                ones = jnp.where(mask_fn(key_s[pl.ds(c0, ck), :], c0 + row), 1, 0)
                parts += [ones[i * SUBLANES:(i + 1) * SUBLANES] for i in range(ck // SUBLANES)]
            return acc + _tree_sum(parts)
        acc = lax.fori_loop(0, n_pair, body, jnp.zeros((SUBLANES, nq), jnp.int32))
        return jnp.sum(acc, axis=0, keepdims=True)

    def value_bit(i, carry):
        lo, n_lo = carry
        cand = lo + jnp.left_shift(jnp.int32(1), 31 - i)
        n_cand = count(lambda key, pos: key >= cand)
        ok = n_cand >= topk
        return jnp.where(ok, cand, lo), jnp.where(ok, n_cand, n_lo)

    thr, n_ge = lax.fori_loop(0, 32, value_bit, (jnp.full((1, nq), INT_MIN, jnp.int32),
                                                 jnp.full((1, nq), s, jnp.int32)))

    n_bits = s.bit_length()

    def tie_search():
        need = topk - count(lambda key, pos: key > thr)

        def index_bit(i, lo):
            cand = lo + jnp.left_shift(jnp.int32(1), n_bits - 1 - i)
            return jnp.where(count(lambda key, pos: (key == thr) & (pos < cand)) < need, cand, lo)

        return lax.fori_loop(0, n_bits, index_bit, jnp.zeros((1, nq), jnp.int32))

    surplus = jnp.max(jnp.where((n_ge > topk) & (thr > INT_MIN), 1, 0))
    last = lax.cond(surplus > 0, tie_search, lambda: jnp.full((1, nq), s, jnp.int32))

    q_t = (_dot(wuqt_ref[...], cq.astype(BF16)) * lat ** -0.5).astype(BF16)

    def attend(c, carry):
        c0 = chunk_start(c)
        key = key_s[pl.ds(c0, ck), :]
        pos = c0 + row
        sel = ((key > thr) | ((key == thr) & (pos <= last))) & (pos <= tpos)
        bias = jnp.where(sel, 0.0, NEG_BIG)
        ckv = ckv_ref[0, pl.ds(c0, ck), :]
        ckvt = ckvt_ref[0, :, pl.ds(c0, ck)]
        new = []
        for hd in range(DSA_HEADS):
            m, l, acc = carry[hd]
            lg = _dot(ckv, q_t[hd * lat:(hd + 1) * lat, :]) + bias
            m_new = jnp.maximum(m, jnp.max(lg, axis=0, keepdims=True))
            alpha = jnp.exp(m - m_new)
            p = jnp.exp(lg - m_new)
            new.append((m_new, alpha * l + jnp.sum(p, axis=0, keepdims=True),
                        alpha * acc + _dot(ckvt, p.astype(BF16))))
        return tuple(new)

    init = tuple((jnp.full((1, nq), NEG_BIG, F32), jnp.zeros((1, nq), F32), jnp.zeros((lat, nq), F32))
                 for _ in range(DSA_HEADS))
    final = lax.fori_loop(0, n_pair, lambda j, carry: attend(2 * j + 1, attend(2 * j, carry)), init)
    outs = [_dot(wuvt_ref[hd], (acc / l).astype(BF16)) for hd, (_, l, acc) in enumerate(final)]
    o_ref[0] = jnp.concatenate(outs, axis=0).T.astype(o_ref.dtype)


def _dsa(cqt, wit, ki6, ckv, ckvt, w_uq, w_qidx, w_uv, ck):
    b, c, s = cqt.shape
    nq = DSA_Q_BLOCK
    wuqt = w_uq.reshape(c, -1).T.astype(BF16)
    wqit = w_qidx.reshape(c, -1).T
    wuvt = jnp.transpose(w_uv, (0, 2, 1)).astype(BF16)
    return pl.pallas_call(
        functools.partial(_dsa_kernel, ck=ck),
        out_shape=jax.ShapeDtypeStruct((b, s, DSA_HEADS * DSA_HEAD_DIM), BF16),
        grid=(b, s // nq),
        in_specs=[pl.BlockSpec((1, c, nq), lambda i, j: (i, 0, j)),
                  pl.BlockSpec((1, IDX_HEADS, nq), lambda i, j: (i, 0, j)),
                  pl.BlockSpec((1, s, IDX_K_COLS), lambda i, j: (i, 0, 0)),
                  pl.BlockSpec((1, s, DSA_KV_LATENT), lambda i, j: (i, 0, 0)),
                  pl.BlockSpec((1, DSA_KV_LATENT, s), lambda i, j: (i, 0, 0)),
                  _full(wuqt.shape), _full(wqit.shape), _full(wuvt.shape)],
        out_specs=pl.BlockSpec((1, nq, DSA_HEADS * DSA_HEAD_DIM), lambda i, j: (i, j, 0)),
        scratch_shapes=[pltpu.VMEM((s, nq), jnp.int32)],
        compiler_params=_cparams("arbitrary", "arbitrary"),
        name="dsa",
    )(cqt, wit, ki6, ckv, ckvt, wuqt, wqit, wuvt)


Y_PAD = 8


def _head_ones(n, dtype):
    r_i = lax.broadcasted_iota(jnp.int32, (n, n), 0) // RWKV_HEAD_DIM
    c_i = lax.broadcasted_iota(jnp.int32, (n, n), 1) // RWKV_HEAD_DIM
    return jnp.where(r_i == c_i, 1.0, 0.0).astype(dtype)


def _split_bf16(t):
    hi = t.astype(BF16)
    lo = (t - hi.astype(F32)).astype(BF16)
    return hi, lo


def _rwkv_kernel(z_ref, mu_ref, w0_ref, wup_ref, a0_ref, aup_ref, gup_ref, kk_ref, ka_ref, rk_ref, lng_ref, lnb_ref,
                 o_ref, st_s, prev_s, step_s, y_s, bonus_s, gate_s, *, grp):
    c = pl.program_id(1)
    n_g, tc, _ = z_ref.shape
    hd, wd_ = RWKV_HEAD_DIM, RWKV_WIDTH

    @pl.when(c == 0)
    def _():
        st_s[...] = jnp.zeros_like(st_s)
        prev_s[...] = jnp.zeros_like(prev_s)

    ones_b = _head_ones(wd_, BF16)
    diag = jnp.where(lax.broadcasted_iota(jnp.int32, (hd, wd_), 0)
                     == lax.broadcasted_iota(jnp.int32, (hd, wd_), 1) % hd, 1.0, 0.0).astype(F32)

    def head_sum(t):
        hi, lo = _split_bf16(t)
        return _dot(hi, ones_b) + _dot(lo, ones_b)

    zs = []
    for g in range(n_g):
        z = z_ref[g]
        row = lax.broadcasted_iota(jnp.int32, z.shape, 0)
        z_prev = jnp.where(row == 0, prev_s[g], pltpu.roll(z, 1, 0))
        prev_s[g] = z[tc - 1:tc, :]
        zs.append(z + mu_ref[...] * (z_prev - z))
    z = jnp.concatenate(zs, axis=0)
    r = z[:, 0:wd_]
    k = z[:, wd_:2 * wd_]
    v = z[:, 2 * wd_:3 * wd_]
    o = 3 * wd_
    w_lora = z[:, o:o + RWKV_LORA_W]
    a_lora = z[:, o + RWKV_LORA_W:o + RWKV_LORA_W + RWKV_LORA_A]
    g_lora = z[:, o + RWKV_LORA_W + RWKV_LORA_A:]
    w_log = _log_sigmoid(w0_ref[...] + _dot_hi(jnp.tanh(w_lora), wup_ref[...])) - 0.5
    a = _sigmoid(a0_ref[...] + _dot_hi(a_lora, aup_ref[...]))
    gate_s[...] = _dot_hi(_sigmoid(g_lora), gup_ref[...])
    kk = k * kk_ref[...]
    kk = kk / jnp.maximum(jnp.sqrt(head_sum(kk * kk)), 1e-12)
    k = k * (1.0 + (a - 1.0) * ka_ref[...])
    step_s[0] = jnp.exp(-jnp.exp(w_log))
    step_s[1] = -kk
    step_s[2] = kk * a
    step_s[3] = k
    step_s[4] = r
    step_s[5] = v
    bonus_s[...] = head_sum(r * k * rk_ref[...]) * v

    n_grp = n_g // grp

    def rows(kind, g0, t):
        return jnp.concatenate(
            [jnp.broadcast_to(step_s[kind, pl.ds((g0 + i) * tc + t, 1), :], (hd, wd_)) for i in range(grp)], axis=0)

    def store_y(yb, g0, t_write):
        yb = yb * diag_g
        for i in range(grp):
            tile = jnp.sum(yb[i * hd:(i + 1) * hd].reshape(hd // SUBLANES, SUBLANES, wd_), axis=0)
            y_s[pl.ds(Y_PAD + (g0 + i) * tc + t_write, 1), :] = jnp.sum(tile, axis=0, keepdims=True)

    def rows_b(kind, g0, t):
        tiles = []
        for i in range(grp):
            r16 = jnp.broadcast_to(step_s[kind, pl.ds((g0 + i) * tc + t, 1), :], (BF16_SUBLANES, wd_)).astype(BF16)
            tiles.append(jnp.broadcast_to(r16[None], (hd // BF16_SUBLANES, BF16_SUBLANES, wd_)).reshape(hd, wd_))
        return jnp.concatenate(tiles, axis=0)

    def step(t, carry):
        for q in range(n_grp):
            g0 = q * grp
            st = st_s[q]
            st_b = st.astype(BF16)
            sa = _dot(st_b * rows_b(1, g0, t), ones_b)
            store_y(_dot(st_b * rows_b(4, g0, jnp.maximum(t - 1, 0)), ones_b), g0, t - 1)
            vb = _dot(rows_b(5, g0, t) * diag_b, ones_b)
            st_s[q] = st * rows(0, g0, t) + sa * rows(2, g0, t) + vb * rows(3, g0, t)
        return carry

    diag_g = jnp.concatenate([diag] * grp, axis=0)
    diag_b = diag_g.astype(BF16)
    lax.fori_loop(0, tc, step, 0, unroll=8)
    for q in range(n_grp):
        store_y(_dot(st_s[q].astype(BF16) * rows_b(4, q * grp, tc - 1), ones_b), q * grp, tc - 1)

    y = y_s[pl.ds(Y_PAD, n_g * tc), :]
    mean = head_sum(y) * (1.0 / hd)
    yc = y - mean
    var = head_sum(yc * yc) * (1.0 / hd)
    yn = yc * lax.rsqrt(var + RWKV_GN_EPS) * lng_ref[...] + lnb_ref[...]
    out = ((yn + bonus_s[...]) * gate_s[...]).astype(o_ref.dtype)
    for g in range(n_g):
        o_ref[g] = out[g * tc:(g + 1) * tc]


def _rwkv(zr, mu, w0, w_up, a0, a_up, g_up, k_k, k_a, r_k, ln_g, ln_b, n_g, grp, tc):
    b, s, zin = zr.shape
    wd_ = RWKV_WIDTH
    vec = lambda p: p.reshape(1, -1)
    params = [vec(mu), vec(w0), w_up, vec(a0), a_up, g_up, vec(k_k), vec(k_a), vec(r_k), vec(ln_g), vec(ln_b)]
    return pl.pallas_call(
        functools.partial(_rwkv_kernel, grp=grp),
        out_shape=jax.ShapeDtypeStruct((b, s, wd_), BF16),
        grid=(b // n_g, s // tc),
        in_specs=[pl.BlockSpec((n_g, tc, zin), lambda i, c: (i, c, 0))] + [_full(p.shape) for p in params],
        out_specs=pl.BlockSpec((n_g, tc, wd_), lambda i, c: (i, c, 0)),
        scratch_shapes=[pltpu.VMEM((n_g // grp, grp * RWKV_HEAD_DIM, wd_), F32), pltpu.VMEM((n_g, 1, zin), F32),
                        pltpu.VMEM((6, n_g * tc, wd_), F32), pltpu.VMEM((Y_PAD + n_g * tc, wd_), F32),
                        pltpu.VMEM((n_g * tc, wd_), F32), pltpu.VMEM((n_g * tc, wd_), F32)],
        compiler_params=_cparams("arbitrary", "arbitrary"),
        name="rwkv",
    )(zr, *params)


def _softcap(t):
    return GATE_SOFTCAP * jnp.tanh(t / GATE_SOFTCAP)


def _mlstm_kernel(zm_ref, mkt_ref, mvt_ref, mgt_ref, cw_row_ref, cb_row_ref, cw_col_ref, cb_col_ref,
                  bg_row_ref, bg_col_ref, ng_ref, o_ref, q_s, k_s, kt_s, gc_s, gr_s, yt_s):
    s = zm_ref.shape[1]
    nh, dk, dv, lc = MLSTM_HEADS, MLSTM_QK_DIM, MLSTM_V_DIM, MLSTM_CHUNK
    pair = 2 * lc

    qk = zm_ref[0, :, 0:2 * MLSTM_QK]
    pos_r = lax.broadcasted_iota(jnp.int32, qk.shape, 0)
    acc = cb_row_ref[...] + qk * cw_row_ref[CONV_WIDTH - 1:CONV_WIDTH, :]
    for r in range(1, CONV_WIDTH):
        sh = jnp.where(pos_r >= r, pltpu.roll(qk, r, 0), 0.0)
        acc = acc + sh * cw_row_ref[CONV_WIDTH - 1 - r:CONV_WIDTH - r, :]
    acc = _silu(acc)
    q_s[...] = acc[:, :MLSTM_QK] * dk ** -0.5
    k_s[...] = acc[:, MLSTM_QK:]
    kt = mkt_ref[0]
    pos_c = lax.broadcasted_iota(jnp.int32, kt.shape, 1)
    acc_t = cb_col_ref[...] + kt * cw_col_ref[:, CONV_WIDTH - 1:CONV_WIDTH]
    for r in range(1, CONV_WIDTH):
        sh = jnp.where(pos_c >= r, pltpu.roll(kt, r, 1), 0.0)
        acc_t = acc_t + sh * cw_col_ref[:, CONV_WIDTH - 1 - r:CONV_WIDTH - r]
    kt_s[...] = _silu(acc_t)

    gcol = _softcap(zm_ref[0, :, 2 * MLSTM_QK + 2 * MLSTM_V:] + bg_row_ref[...])
    lane = lax.broadcasted_iota(jnp.int32, gcol.shape, 1)
    gc_s[...] = jnp.where(lane < nh, gcol, _log_sigmoid(gcol))
    grow = _softcap(mgt_ref[0] + bg_col_ref[...])
    sub = lax.broadcasted_iota(jnp.int32, grow.shape, 0)
    gr_s[...] = jnp.where(sub < nh, grow, _log_sigmoid(grow))

    r_i = lax.broadcasted_iota(jnp.int32, (lc, lc), 0)
    c_i = lax.broadcasted_iota(jnp.int32, (lc, lc), 1)
    tri = jnp.where(c_i <= r_i, 1.0, 0.0).astype(F32)
    causal_t = r_i <= c_i
    ones_rows = jnp.ones((8, lc), F32)

    def chunk_pair(p, carry):
        base = pl.multiple_of(p * pair, pair)
        gr_slab = gr_s[:, pl.ds(base, pair)]
        kt_slab = kt_s[:, pl.ds(base, pair)]
        vt_slab = mvt_ref[0, :, pl.ds(base, pair)]
        outs = [[] for _ in range(nh)]
        for sc in range(2):
            r0 = base + sc * lc
            gcc = gc_s[pl.ds(r0, lc), :]
            grc = gr_slab[:, sc * lc:(sc + 1) * lc]
            bcum_col = _dot_hi(tri, gcc)
            bcum_row = _dot_nt_hi(grc, tri)
            qc = q_s[pl.ds(r0, lc), :]
            kc = k_s[pl.ds(r0, lc), :]
            new_carry = []
            for hd in range(nh):
                c_aug, m_prev = carry[hd]
                bc_c = bcum_col[:, nh + hd:nh + hd + 1]
                li_c = gcc[:, hd:hd + 1]
                bc_r = bcum_row[nh + hd:nh + hd + 1, :]
                li_r = grc[hd:hd + 1, :]
                gtot = bc_r[:, lc - 1:lc]
                d_t = jnp.where(causal_t, bc_r - bc_c + li_c, NEG_BIG)
                m_inter = bc_r + m_prev
                m_t = jnp.maximum(m_inter, jnp.max(d_t, axis=0, keepdims=True))
                q_h = qc[:, hd * dk:(hd + 1) * dk].astype(BF16)
                k_h = kc[:, hd * dk:(hd + 1) * dk].astype(BF16)
                kt_h = kt_slab[hd * dk:(hd + 1) * dk, sc * lc:(sc + 1) * lc].astype(BF16)
                vt_h = vt_slab[hd * dv:(hd + 1) * dv, sc * lc:(sc + 1) * lc]
                vt_aug = jnp.concatenate([vt_h, ones_rows], axis=0)
                s_t = _dot_nt(k_h, q_h)
                w_t = jnp.exp(d_t - m_t) * s_t
                s_inter = jnp.exp(m_inter - m_t)
                numden = _dot(vt_aug.astype(BF16), w_t.astype(BF16)) + s_inter * _dot_nt(c_aug.astype(BF16), q_h)
                den = numden[dv:dv + 1, :]
                outs[hd].append(numden[:dv, :] / jnp.maximum(jnp.abs(den), jnp.exp(-m_t)))
                a_log = gtot - bc_r + li_r
                a_max = jnp.max(a_log, axis=-1, keepdims=True)
                a_w = jnp.exp(a_log - a_max)
                kvn = _dot_nt((vt_aug * a_w).astype(BF16), kt_h)
                m_new = jnp.maximum(gtot + m_prev, a_max)
                s_old = jnp.exp(gtot + m_prev - m_new)
                s_new = jnp.exp(a_max - m_new)
                new_carry.append((s_old * c_aug + s_new * kvn, m_new))
            carry = tuple(new_carry)
        for hd in range(nh):
            yt_s[hd * dv:(hd + 1) * dv, pl.ds(base, pair)] = jnp.concatenate(outs[hd], axis=-1)
        return carry

    init = tuple((jnp.zeros((dv + 8, dk), F32), jnp.full((1, 1), NEG_BIG, F32)) for _ in range(nh))
    lax.fori_loop(0, s // pair, chunk_pair, init)

    parts = []
    for hd in range(nh):
        blk = yt_s[hd * dv:(hd + 1) * dv, :]
        parts.append(blk * lax.rsqrt(jnp.mean(blk * blk, axis=0, keepdims=True) + EPS))
    y = jnp.concatenate(parts, axis=0).T
    o_gate = _sigmoid(zm_ref[0, :, 2 * MLSTM_QK + MLSTM_V:2 * MLSTM_QK + 2 * MLSTM_V])
    o_ref[0] = (y * ng_ref[...] * o_gate).astype(o_ref.dtype)


def _mlstm(zm, mkt, mvt, mgt, conv_w, conv_b, b_i, b_f, norm_g):
    b, s, _ = zm.shape
    cw_col = conv_w[:, MLSTM_QK:].T
    cb_col = conv_b[MLSTM_QK:].reshape(-1, 1)
    bg = jnp.concatenate([b_i, b_f])
    bg_row = jnp.pad(bg, (0, LANES - 2 * MLSTM_HEADS)).reshape(1, LANES)
    bg_col = bg.reshape(-1, 1)
    per_b = lambda r, c: pl.BlockSpec((1, r, c), lambda i: (i, 0, 0))
    return pl.pallas_call(
        _mlstm_kernel,
        out_shape=jax.ShapeDtypeStruct((b, s, MLSTM_V), BF16),
        grid=(b,),
        in_specs=[per_b(s, M_COLS), per_b(MLSTM_QK, s), per_b(MLSTM_V, s), per_b(8, s),
                  _full(conv_w.shape), _full((1, 2 * MLSTM_QK)), _full(cw_col.shape), _full(cb_col.shape),
                  _full((1, LANES)), _full((8, 1)), _full((1, MLSTM_V))],
        out_specs=per_b(s, MLSTM_V),
        scratch_shapes=[pltpu.VMEM((s, MLSTM_QK), F32), pltpu.VMEM((s, MLSTM_QK), F32), pltpu.VMEM((MLSTM_QK, s), F32),
                        pltpu.VMEM((s, LANES), F32), pltpu.VMEM((8, s), F32), pltpu.VMEM((MLSTM_V, s), F32)],
        compiler_params=_cparams("arbitrary"),
        name="mlstm",
    )(zm, mkt, mvt, mgt, conv_w, conv_b.reshape(1, -1), cw_col, cb_col, bg_row, bg_col, norm_g.reshape(1, -1))


def _merge_kernel(x_ref, g_ref, y0_ref, y1_ref, y2_ref, y3_ref, wg_ref, wb_ref, wo_ref, o_ref):
    x = x_ref[...]
    d = x.shape[1]
    h = _rms_rows(x, g_ref[...]).astype(BF16)
    merged = None
    for n, y_ref in enumerate((y0_ref, y1_ref, y2_ref, y3_ref)):
        gate = _sigmoid(_dot_nt(h, wg_ref[n * d:(n + 1) * d, :]))
        term = gate * _dot(y_ref[...], wb_ref[n])
        merged = term if merged is None else merged + term
    o_ref[...] = x + _dot(merged.astype(BF16), wo_ref[...])


def _merge(x2, g, ys, w_gate, w_branch, w_out, tm):
    t, d = x2.shape
    row = lambda w: pl.BlockSpec((tm, w), lambda i: (i, 0))
    return pl.pallas_call(
        _merge_kernel,
        out_shape=jax.ShapeDtypeStruct((t, d), F32),
        grid=(t // tm,),
        in_specs=[row(d), _full((1, d))] + [row(BRANCH_WIDTH)] * N_BRANCH
                 + [_full(w_gate.shape), _full(w_branch.shape), _full(w_out.shape)],
        out_specs=row(d),
        compiler_params=_cparams("arbitrary"),
        name="merge",
    )(x2, g.reshape(1, d), *ys, w_gate, w_branch, w_out)


def _ffn_kernel(x_ref, g_ref, w1_ref, w3_ref, w2_ref, o_ref):
    x = x_ref[...]
    h = _rms_rows(x, g_ref[...]).astype(BF16)
    u = _silu(_dot(h, w1_ref[...])) * _dot(h, w3_ref[...])
    o_ref[...] = x + _dot(u.astype(BF16), w2_ref[...])


def _ffn(x2, g, w1, w3, w2, tm):
    t, d = x2.shape
    row = pl.BlockSpec((tm, d), lambda i: (i, 0))
    return pl.pallas_call(
        _ffn_kernel,
        out_shape=jax.ShapeDtypeStruct((t, d), F32),
        grid=(t // tm,),
        in_specs=[row, _full((1, d)), _full(w1.shape), _full(w3.shape), _full(w2.shape)],
        out_specs=row,
        compiler_params=_cparams("arbitrary"),
        name="ffn",
    )(x2, g.reshape(1, d), w1, w3, w2)


MOE_CAP = 160


def _moe_router_kernel(x_ref, g_ref, wr_ref, h_ref, rank_ref, gate_ref, rrow_ref, cnt_ref):
    tm = x_ref.shape[0]
    h = _rms_rows(x_ref[...], g_ref[...])
    h_ref[...] = h.astype(BF16)
    logits = _dot_hi(h, wr_ref[...])
    lane = lax.broadcasted_iota(jnp.int32, logits.shape, 1)
    logits = jnp.where(lane < N_EXPERTS, logits, -jnp.inf)
    v1 = jnp.max(logits, axis=-1, keepdims=True)
    i1 = jnp.min(jnp.where(logits == v1, lane, LANES), axis=-1, keepdims=True)
    rest = jnp.where(lane == i1, -jnp.inf, logits)
    v2 = jnp.max(rest, axis=-1, keepdims=True)
    i2 = jnp.min(jnp.where(rest == v2, lane, LANES), axis=-1, keepdims=True)
    e2 = jnp.exp(v2 - v1)
    gate_ref[...] = jnp.where(lane == i1, 1.0 / (1.0 + e2), 0.0) + jnp.where(lane == i2, e2 / (1.0 + e2), 0.0)
    routed = jnp.where((lane == i1) | (lane == i2), 1.0, 0.0)
    r_i = lax.broadcasted_iota(jnp.int32, (tm, tm), 0)
    c_i = lax.broadcasted_iota(jnp.int32, (tm, tm), 1)
    rank = _dot(jnp.where(c_i < r_i, 1.0, 0.0).astype(BF16), routed.astype(BF16))
    rank = jnp.where(routed > 0.0, rank, -1.0)
    rank_ref[...] = rank
    eye = jnp.where(lax.broadcasted_iota(jnp.int32, (LANES, LANES), 0)
                    == lax.broadcasted_iota(jnp.int32, (LANES, LANES), 1), 1.0, 0.0)
    rrow_ref[0] = _dot_nt_hi(eye, rank)[0:N_EXPERTS, :]
    cnt_ref[0] = jnp.broadcast_to(jnp.sum(routed, axis=0, keepdims=True), (8, LANES))


def _moe_expert_kernel(cnt_ref, y_ref, h_ref, rrow_ref, rank_ref, gate_ref, w1_ref, w3_ref, w2_ref, o_ref, acc_s):
    e = pl.program_id(0)
    i = pl.program_id(1)
    tm = h_ref.shape[0]
    cap = MOE_CAP
    lane = lax.broadcasted_iota(jnp.int32, (tm, LANES), 1)
    rank_col = jnp.sum(jnp.where(lane == e, rank_ref[...], 0.0), axis=-1, keepdims=True)
    gate_col = jnp.sum(jnp.where(lane == e, gate_ref[...], 0.0), axis=-1, keepdims=True)
    rank_row = rrow_ref[0, pl.ds(e, 1), :]
    slot_r = lax.broadcasted_iota(jnp.int32, (cap, tm), 0).astype(F32)
    slot_c = lax.broadcasted_iota(jnp.int32, (tm, cap), 1).astype(F32)
    acc_s[...] = jnp.zeros_like(acc_s)

    def one_pass(p, carry):
        base = (p * cap).astype(F32)
        gather = jnp.where(rank_row - base == slot_r, 1.0, 0.0).astype(BF16)
        xc = _dot(gather, h_ref[...]).astype(BF16)
        u = _silu(_dot(xc, w1_ref[0])) * _dot(xc, w3_ref[0])
        yc_hi, yc_lo = _split_bf16(_dot(u.astype(BF16), w2_ref[0]))
        scatter = jnp.where(rank_col - base == slot_c, 1.0, 0.0).astype(BF16)
        acc_s[...] += _dot(scatter, yc_hi) + _dot(scatter, yc_lo)
        return carry

    n_pass = (cnt_ref[i, e] + cap - 1) // cap
    lax.fori_loop(0, n_pass, one_pass, 0)
    o_ref[...] = y_ref[...] + gate_col * acc_s[...]


def _moe(x2, g, router, w1, w3, w2, tm):
    t, d = x2.shape
    n_e, _, f = w1.shape
    n_t = t // tm
    router_p = jnp.pad(router, ((0, 0), (0, LANES - n_e)))
    row = lambda w: pl.BlockSpec((tm, w), lambda i: (i, 0))
    h, rank, gate, rrow, cnt = pl.pallas_call(
        _moe_router_kernel,
        out_shape=[jax.ShapeDtypeStruct((t, d), BF16), jax.ShapeDtypeStruct((t, LANES), F32),
                   jax.ShapeDtypeStruct((t, LANES), F32), jax.ShapeDtypeStruct((n_t, N_EXPERTS, tm), F32),
                   jax.ShapeDtypeStruct((n_t, 8, LANES), F32)],
        grid=(n_t,),
        in_specs=[row(d), _full((1, d)), _full((d, LANES))],
        out_specs=[row(d), row(LANES), row(LANES), pl.BlockSpec((1, N_EXPERTS, tm), lambda i: (i, 0, 0)),
                   pl.BlockSpec((1, 8, LANES), lambda i: (i, 0, 0))],
        compiler_params=_cparams("arbitrary"),
        name="moe_router",
    )(x2, g.reshape(1, d), router_p)
    counts = cnt[:, 0, :n_e].astype(jnp.int32)
    tile = lambda w: pl.BlockSpec((tm, w), lambda e, i, c: (i, 0))
    weight = lambda r, c_: pl.BlockSpec((1, r, c_), lambda e, i, c: (e, 0, 0), pipeline_mode=pl.Buffered(1))
    return pl.pallas_call(
        _moe_expert_kernel,
        out_shape=jax.ShapeDtypeStruct((t, d), F32),
        grid_spec=pltpu.PrefetchScalarGridSpec(
            num_scalar_prefetch=1,
            grid=(n_e, n_t),
            in_specs=[tile(d), tile(d), pl.BlockSpec((1, N_EXPERTS, tm), lambda e, i, c: (i, 0, 0)),
                      tile(LANES), tile(LANES), weight(d, f), weight(d, f), weight(f, d)],
            out_specs=tile(d),
            scratch_shapes=[pltpu.VMEM((tm, d), F32)]),
        input_output_aliases={1: 0},
        compiler_params=_cparams("arbitrary", "arbitrary"),
        name="moe_experts",
    )(counts, x2, h, rrow, rank, gate, w1, w3, w2)


def _final_norm_kernel(x_ref, g_ref, o_ref):
    o_ref[...] = _rms_rows(x_ref[...], g_ref[...])


def _final_norm(x2, g, tm):
    t, d = x2.shape
    row = pl.BlockSpec((tm, d), lambda i: (i, 0))
    return pl.pallas_call(
        _final_norm_kernel,
        out_shape=jax.ShapeDtypeStruct((t, d), F32),
        grid=(t // tm,),
        in_specs=[row, _full((1, d))],
        out_specs=row,
        compiler_params=_cparams("arbitrary"),
        name="final_norm",
    )(x2, g.reshape(1, d))


def _tiles(b, s):
    tiles = dict(
        rows=min(512, s),
        fox_q=min(1024, s),
        dsa_keys=min(256, s // 2),
        rwkv_seqs=min(16, b),
        rwkv_group=min(4, b),
        rwkv_steps=64,
    )
    assert s % tiles['rows'] == 0 and s % tiles['fox_q'] == 0 and s % (2 * tiles['dsa_keys']) == 0
    assert s % DSA_Q_BLOCK == 0 and s % (2 * MLSTM_CHUNK) == 0 and s % tiles['rwkv_steps'] == 0
    assert b % tiles['rwkv_seqs'] == 0 and tiles['rwkv_seqs'] % tiles['rwkv_group'] == 0
    return tiles


def kernel(x, norm_mix_g, w_in, dsa_g_cq, dsa_g_ckv, dsa_g_kidx, dsa_w_uq, dsa_w_qidx, dsa_w_uv, rwkv_mu, rwkv_w0, rwkv_w_up, rwkv_a0, rwkv_a_up, rwkv_g_up, rwkv_k_k, rwkv_k_a, rwkv_r_k, rwkv_ln_g, rwkv_ln_b, mlstm_conv_w, mlstm_conv_b, mlstm_b_i, mlstm_b_f, mlstm_norm_g, fox_b_f, w_branch, w_out, norm_ffn_g, ffn_w1, ffn_w3, ffn_w2, moe_router, moe_w1, moe_w3, moe_w2, final_norm_g):
    b, s, d = x.shape
    depth = w_in.shape[0]
    t = b * s
    tl = _tiles(b, s)
    for l in range(depth):
        wts = _inproj_weights(w_in[l])
        (zr, zm, fq, fk, fv, ckv, ki6, cqt, ckvt, wit, frow, mkt, mvt, mgt) = _inproj(
            x, norm_mix_g[l], wts, dsa_g_cq[l], dsa_g_ckv[l], dsa_g_kidx[l], fox_b_f[l], tl['rows'])
        y_fox = _fox(fq, fk, fv, frow, tl['fox_q'])
        y_dsa = _dsa(cqt, wit, ki6, ckv, ckvt, dsa_w_uq[l], dsa_w_qidx[l], dsa_w_uv[l], tl['dsa_keys'])
        y_rwkv = _rwkv(zr, rwkv_mu[l], rwkv_w0[l], rwkv_w_up[l], rwkv_a0[l], rwkv_a_up[l], rwkv_g_up[l], rwkv_k_k[l],
                       rwkv_k_a[l], rwkv_r_k[l], rwkv_ln_g[l], rwkv_ln_b[l], tl['rwkv_seqs'], tl['rwkv_group'],
                       tl['rwkv_steps'])
        y_mlstm = _mlstm(zm, mkt, mvt, mgt, mlstm_conv_w[l], mlstm_conv_b[l], mlstm_b_i[l], mlstm_b_f[l],
                         mlstm_norm_g[l])
        ys = tuple(y.reshape(t, BRANCH_WIDTH) for y in (y_dsa, y_rwkv, y_mlstm, y_fox))
        x2 = _merge(x.reshape(t, d), norm_mix_g[l], ys, wts['w_gate'], w_branch[l].astype(BF16),
                    w_out[l].astype(BF16), tl['rows'])
        j = l // 2
        if l % 2 == 0:
            x2 = _ffn(x2, norm_ffn_g[l], ffn_w1[j].astype(BF16), ffn_w3[j].astype(BF16), ffn_w2[j].astype(BF16),
                      tl['rows'])
        else:
            x2 = _moe(x2, norm_ffn_g[l], moe_router[j], moe_w1[j].astype(BF16), moe_w3[j].astype(BF16),
                      moe_w2[j].astype(BF16), tl['rows'])
        x = x2.reshape(b, s, d)
    return _final_norm(x.reshape(t, d), final_norm_g, tl['rows']).reshape(b, s, d)
```

```python
import functools

import jax
import jax.numpy as jnp
from jax import lax
from jax.experimental import pallas as pl
from jax.experimental.pallas import tpu as pltpu

F32 = jnp.float32
BF16 = jnp.bfloat16
HIGHEST = lax.Precision.HIGHEST

EPS = 1e-6
NEG_BIG = -1e30

N_BRANCH = 4
BRANCH_WIDTH = 256
DSA_HEADS = 4
DSA_HEAD_DIM = 64
DSA_Q_LATENT = 128
DSA_KV_LATENT = 128
IDX_HEADS = 8
IDX_DIM = 32
TOPK_MAX = 256
DSA_Q_BLOCK = 128

RWKV_HEADS = 4
RWKV_HEAD_DIM = 64
RWKV_WIDTH = RWKV_HEADS * RWKV_HEAD_DIM
RWKV_LORA_W = 64
RWKV_LORA_A = 64
RWKV_LORA_G = 128
RWKV_GN_EPS = 64e-5
RWKV_IN = 3 * RWKV_WIDTH + RWKV_LORA_W + RWKV_LORA_A + RWKV_LORA_G

MLSTM_HEADS = 4
MLSTM_QK_DIM = 32
MLSTM_V_DIM = 64
MLSTM_CHUNK = 256
CONV_WIDTH = 4
GATE_SOFTCAP = 15.0
MLSTM_QK = MLSTM_HEADS * MLSTM_QK_DIM
MLSTM_V = MLSTM_HEADS * MLSTM_V_DIM

FOX_HEADS = 4
FOX_HEAD_DIM = 64
FOX_WIDTH = FOX_HEADS * FOX_HEAD_DIM

N_EXPERTS = 8

VMEM_LIMIT_BYTES = 56 * 1024 * 1024
LANES = 128
SUBLANES = 8
BF16_SUBLANES = 16


def _cparams(*sem):
    return pltpu.CompilerParams(dimension_semantics=sem, vmem_limit_bytes=VMEM_LIMIT_BYTES)


def _dot(a, b):
    return jnp.dot(a, b, preferred_element_type=F32)


def _dot_hi(a, b):
    return jnp.dot(a, b, preferred_element_type=F32, precision=HIGHEST)


def _dot_nt(a, b):
    return lax.dot_general(a, b, (((1,), (1,)), ((), ())), preferred_element_type=F32)


def _dot_nt_hi(a, b):
    return lax.dot_general(a, b, (((1,), (1,)), ((), ())), preferred_element_type=F32, precision=HIGHEST)


def _log_sigmoid(t):
    return jnp.minimum(t, 0.0) - jnp.log1p(jnp.exp(-jnp.abs(t)))


def _sigmoid(t):
    return 1.0 / (1.0 + jnp.exp(-t))


def _silu(t):
    return t * _sigmoid(t)


def _rms_rows(t, g_row):
    return t * lax.rsqrt(jnp.mean(t * t, axis=-1, keepdims=True) + EPS) * g_row


def _rms_cols(t, g_col):
    return t * lax.rsqrt(jnp.mean(t * t, axis=0, keepdims=True) + EPS) * g_col


def _full(shape):
    n = len(shape)
    return pl.BlockSpec(shape, lambda *_: (0,) * n)


T_CQ = 0
T_CKV = T_CQ + DSA_Q_LATENT
T_WIDX = T_CKV + DSA_KV_LATENT
T_FOXF = T_WIDX + IDX_HEADS
T_MK = T_FOXF + SUBLANES
T_MV = T_MK + MLSTM_QK
T_MG = T_MV + MLSTM_V
T_ROWS = T_MG + 2 * MLSTM_HEADS
M_COLS = 2 * MLSTM_QK + 2 * MLSTM_V + LANES


def _inproj_kernel(x_ref, g_ref, wr_ref, wm_ref, wf_ref, wd_ref, wt_ref,
                   gcq_ref, gckv_col_ref, gckv_row_ref, gki_ref, bf_col_ref,
                   zr_ref, zm_ref, q_ref, k_ref, v_ref, ckv_ref, ki_ref,
                   cqt_ref, ckvt_ref, wit_ref, frow_ref, mkt_ref, mvt_ref, mgt_ref,
                   carry_col):
    j = pl.program_id(1)
    tm = x_ref.shape[1]

    @pl.when(j == 0)
    def _():
        carry_col[...] = jnp.zeros_like(carry_col)

    x = x_ref[0]
    h = _rms_rows(x, g_ref[...]).astype(BF16)

    zr_ref[0] = _dot_nt(h, wr_ref[...])
    zm_ref[0] = _dot_nt(h, wm_ref[...])

    zf = _dot_nt(h, wf_ref[...])
    for hd in range(FOX_HEADS):
        lo = hd * FOX_HEAD_DIM
        q_ref[0, hd] = (zf[:, lo:lo + FOX_HEAD_DIM] * FOX_HEAD_DIM ** -0.5).astype(BF16)
        k_ref[0, hd] = zf[:, FOX_WIDTH + lo:FOX_WIDTH + lo + FOX_HEAD_DIM].astype(BF16)
        v_ref[0, hd] = zf[:, 2 * FOX_WIDTH + lo:2 * FOX_WIDTH + lo + FOX_HEAD_DIM].astype(BF16)

    zd = _dot_nt(h, wd_ref[...])
    ckv_ref[0] = _rms_rows(zd[:, :LANES], gckv_row_ref[...]).astype(BF16)
    ki_pieces = [p.astype(F32) for p in _split3_bf16(_rms_rows(zd[:, LANES:LANES + IDX_DIM], gki_ref[...]))]
    ki_pad = jnp.zeros((tm, IDX_K_COLS - len(IDX_SPLIT_K) * IDX_DIM), F32)
    ki_ref[0] = jnp.concatenate([ki_pieces[kp] for kp, _ in IDX_SPLIT_K] + [ki_pad], axis=1).astype(BF16)

    zt = _dot_nt(wt_ref[...], h)
    cqt_ref[0] = _rms_cols(zt[T_CQ:T_CKV], gcq_ref[...])
    ckvt_ref[0] = _rms_cols(zt[T_CKV:T_WIDX], gckv_col_ref[...]).astype(BF16)
    wit_ref[0] = zt[T_WIDX:T_FOXF] * IDX_HEADS ** -0.5
    lf_row = _log_sigmoid(zt[T_FOXF:T_MK] + bf_col_ref[...])
    r_i = lax.broadcasted_iota(jnp.int32, (tm, tm), 0)
    c_i = lax.broadcasted_iota(jnp.int32, (tm, tm), 1)
    tri = jnp.where(c_i <= r_i, 1.0, 0.0).astype(BF16)
    cum_row = sum(_dot_nt(p, tri) for p in _split3_bf16(lf_row)) + carry_col[:, 0:1]
    frow_ref[0] = cum_row
    carry_col[...] = jnp.broadcast_to(cum_row[:, tm - 1:tm], carry_col.shape)
    mkt_ref[0] = zt[T_MK:T_MV]
    mvt_ref[0] = zt[T_MV:T_MG]
    mgt_ref[0] = zt[T_MG:T_ROWS]


def _inproj_weights(w_in_l):
    wt = w_in_l.T
    o = 0
    w_cq = wt[o:o + DSA_Q_LATENT]; o += DSA_Q_LATENT
    w_ckv = wt[o:o + DSA_KV_LATENT]; o += DSA_KV_LATENT
    w_kidx = wt[o:o + IDX_DIM]; o += IDX_DIM
    w_widx = wt[o:o + IDX_HEADS]; o += IDX_HEADS
    w_rwkv = wt[o:o + RWKV_IN]; o += RWKV_IN
    w_mq = wt[o:o + MLSTM_QK]; o += MLSTM_QK
    w_mk = wt[o:o + MLSTM_QK]; o += MLSTM_QK
    w_mv = wt[o:o + MLSTM_V]; o += MLSTM_V
    w_mo = wt[o:o + MLSTM_V]; o += MLSTM_V
    w_mi = wt[o:o + MLSTM_HEADS]; o += MLSTM_HEADS
    w_mf = wt[o:o + MLSTM_HEADS]; o += MLSTM_HEADS
    w_fox = wt[o:o + 3 * FOX_WIDTH]; o += 3 * FOX_WIDTH
    w_ff = wt[o:o + FOX_HEADS]; o += FOX_HEADS
    w_gate = wt[o:]

    def padr(w, n):
        return jnp.pad(w, ((0, n - w.shape[0]), (0, 0)))

    w_d = jnp.concatenate([w_ckv, padr(w_kidx, LANES)], axis=0)
    w_m = jnp.concatenate([w_mq, w_mk, w_mv, w_mo, padr(jnp.concatenate([w_mi, w_mf], axis=0), LANES)], axis=0)
    w_t = jnp.concatenate([w_cq, w_ckv, w_widx, padr(w_ff, 8), w_mk, w_mv, w_mi, w_mf], axis=0)
    assert w_t.shape[0] == T_ROWS
    cast = lambda w: w.astype(BF16)
    return dict(w_r=cast(w_rwkv), w_m=cast(w_m), w_f=cast(w_fox), w_d=cast(w_d), w_t=cast(w_t), w_gate=cast(w_gate))


def _inproj(x, g, wts, g_cq, g_ckv, g_kidx, fox_b_f, tm):
    b, s, d = x.shape
    nj = s // tm
    bf_col = jnp.pad(fox_b_f, (0, SUBLANES - FOX_HEADS)).reshape(SUBLANES, 1)
    row = lambda w: pl.BlockSpec((1, tm, w), lambda i, j: (i, j, 0))
    head = pl.BlockSpec((1, FOX_HEADS, tm, FOX_HEAD_DIM), lambda i, j: (i, 0, j, 0))
    col = lambda r: pl.BlockSpec((1, r, tm), lambda i, j: (i, 0, j))
    out_shape = [
        jax.ShapeDtypeStruct((b, s, RWKV_IN), F32),
        jax.ShapeDtypeStruct((b, s, M_COLS), F32),
        jax.ShapeDtypeStruct((b, FOX_HEADS, s, FOX_HEAD_DIM), BF16),
        jax.ShapeDtypeStruct((b, FOX_HEADS, s, FOX_HEAD_DIM), BF16),
        jax.ShapeDtypeStruct((b, FOX_HEADS, s, FOX_HEAD_DIM), BF16),
        jax.ShapeDtypeStruct((b, s, DSA_KV_LATENT), BF16),
        jax.ShapeDtypeStruct((b, s, IDX_K_COLS), BF16),
        jax.ShapeDtypeStruct((b, DSA_Q_LATENT, s), F32),
        jax.ShapeDtypeStruct((b, DSA_KV_LATENT, s), BF16),
        jax.ShapeDtypeStruct((b, IDX_HEADS, s), F32),
        jax.ShapeDtypeStruct((b, 8, s), F32),
        jax.ShapeDtypeStruct((b, MLSTM_QK, s), F32),
        jax.ShapeDtypeStruct((b, MLSTM_V, s), F32),
        jax.ShapeDtypeStruct((b, 8, s), F32),
    ]
    out_specs = [row(RWKV_IN), row(M_COLS), head, head, head, row(DSA_KV_LATENT), row(IDX_K_COLS),
                 col(DSA_Q_LATENT), col(DSA_KV_LATENT), col(IDX_HEADS), col(8), col(MLSTM_QK), col(MLSTM_V), col(8)]
    in_specs = [row(d), _full((1, d)), _full(wts['w_r'].shape), _full(wts['w_m'].shape), _full(wts['w_f'].shape),
                _full(wts['w_d'].shape), _full(wts['w_t'].shape),
                _full((DSA_Q_LATENT, 1)), _full((DSA_KV_LATENT, 1)), _full((1, DSA_KV_LATENT)), _full((1, IDX_DIM)),
                _full((SUBLANES, 1))]
    return pl.pallas_call(
        _inproj_kernel,
        out_shape=out_shape,
        grid=(b, nj),
        in_specs=in_specs,
        out_specs=out_specs,
        scratch_shapes=[pltpu.VMEM((SUBLANES, LANES), F32)],
        compiler_params=_cparams("arbitrary", "arbitrary"),
        name="inproj",
    )(x, g.reshape(1, d), wts['w_r'], wts['w_m'], wts['w_f'], wts['w_d'], wts['w_t'],
      g_cq.reshape(-1, 1), g_ckv.reshape(-1, 1), g_ckv.reshape(1, -1), g_kidx.reshape(1, -1), bf_col)


def _fox_kernel(q_ref, k_ref, v_ref, frow_ref, o_ref):
    qi = pl.program_id(1)
    tq = q_ref.shape[2]
    outs = []
    for hd in range(FOX_HEADS):
        q = q_ref[0, hd]

        def scores(j):
            start = pl.multiple_of(j * tq, tq)
            k = k_ref[0, hd, pl.ds(start, tq), :]
            v = v_ref[0, hd, pl.ds(start, tq), :]
            fk = frow_ref[0, hd:hd + 1, pl.ds(start, tq)]
            return _dot_nt(q, k) - fk, v

        def update(carry, sc, v):
            m, l, acc = carry
            m_new = jnp.maximum(m, jnp.max(sc, axis=-1, keepdims=True))
            alpha = jnp.exp(m - m_new)
            p = jnp.exp(sc - m_new)
            l = alpha * l + jnp.sum(p, axis=-1, keepdims=True)
            acc = alpha * acc + _dot(p.astype(BF16), v)
            return m_new, l, acc

        def body(j, carry):
            sc, v = scores(j)
            return update(carry, sc, v)

        init = (jnp.full((tq, 1), NEG_BIG, F32), jnp.zeros((tq, 1), F32), jnp.zeros((tq, FOX_HEAD_DIM), F32))
        carry = lax.fori_loop(0, qi, body, init)
        sc, v = scores(qi)
        r_i = lax.broadcasted_iota(jnp.int32, (tq, tq), 0)
        c_i = lax.broadcasted_iota(jnp.int32, (tq, tq), 1)
        sc = jnp.where(c_i <= r_i, sc, NEG_BIG)
        m, l, acc = update(carry, sc, v)
        outs.append(acc / l)
    o_ref[0] = jnp.concatenate(outs, axis=-1).astype(o_ref.dtype)


def _fox(q, k, v, frow, tq):
    b, nh, s, d = q.shape
    return pl.pallas_call(
        _fox_kernel,
        out_shape=jax.ShapeDtypeStruct((b, s, nh * d), BF16),
        grid=(b, s // tq),
        in_specs=[
            pl.BlockSpec((1, nh, tq, d), lambda i, j: (i, 0, j, 0)),
            pl.BlockSpec((1, nh, s, d), lambda i, j: (i, 0, 0, 0)),
            pl.BlockSpec((1, nh, s, d), lambda i, j: (i, 0, 0, 0)),
            pl.BlockSpec((1, SUBLANES, s), lambda i, j: (i, 0, 0)),
        ],
        out_specs=pl.BlockSpec((1, tq, nh * d), lambda i, j: (i, j, 0)),
        compiler_params=_cparams("arbitrary", "arbitrary"),
        name="fox",
    )(q, k, v, frow)


INT_MIN = -2 ** 31


IDX_SPLIT_K = ((0, 0), (0, 1), (1, 0), (0, 2), (1, 1), (2, 0))
IDX_K_COLS = 2 * LANES


def _split3_bf16(t):
    p0 = t.astype(BF16)
    r1 = t - p0.astype(F32)
    p1 = r1.astype(BF16)
    p2 = (r1 - p1.astype(F32)).astype(BF16)
    return p0, p1, p2


def _tree_sum(parts):
    while len(parts) > 1:
        parts = [parts[i] + parts[i + 1] for i in range(0, len(parts) - 1, 2)] + (parts[-1:] if len(parts) % 2 else [])
    return parts[0]


def _dsa_kernel(cqt_ref, wit_ref, ki6_ref, ckv_ref, ckvt_ref, wuqt_ref, wqit_ref, wuvt_ref, o_ref, key_s, *, ck):
    qb = pl.program_id(1)
    s = ki6_ref.shape[1]
    nq = cqt_ref.shape[2]
    topk = min(TOPK_MAX, s // 4)
    n_ck = (qb * nq + nq + ck - 1) // ck
    lat = DSA_KV_LATENT

    cq = cqt_ref[0]
    qi_t = _dot_hi(wqit_ref[...], cq)
    zeros_pad = jnp.zeros((IDX_K_COLS - len(IDX_SPLIT_K) * IDX_DIM, nq), BF16)
    q_blocks = []
    for hd in range(IDX_HEADS):
        pieces = _split3_bf16(qi_t[hd * IDX_DIM:(hd + 1) * IDX_DIM, :])
        q_blocks.append(jnp.concatenate([pieces[qp] for _, qp in IDX_SPLIT_K] + [zeros_pad], axis=0))
    q6 = jnp.concatenate(q_blocks, axis=1)
    wi = wit_ref[0] * IDX_DIM ** -0.5
    tpos = qb * nq + lax.broadcasted_iota(jnp.int32, (ck, nq), 1)
    row = lax.broadcasted_iota(jnp.int32, (ck, nq), 0)

    def chunk_start(c):
        return pl.multiple_of(c * ck, ck)

    def score_chunk(c, carry):
        c0 = chunk_start(c)
        ki6 = ki6_ref[0, pl.ds(c0, ck), :]
        score = None
        for hp in range(0, IDX_HEADS, 2):
            dots = _dot(ki6, q6[:, hp * nq:(hp + 2) * nq])
            for i in range(2):
                term = wi[hp + i:hp + i + 1, :] * jnp.maximum(dots[:, i * nq:(i + 1) * nq], 0.0)
                score = term if score is None else score + term
        bits = pltpu.bitcast(score, jnp.int32)
        key = jnp.where(bits < 0, bits ^ 0x7FFFFFFF, bits)
        key = jnp.where(score == 0.0, 0, key)
        key_s[pl.ds(c0, ck), :] = jnp.where(c0 + row <= tpos, key, INT_MIN)
        return carry

    n_pair = (n_ck + 1) // 2
    lax.fori_loop(0, n_pair, lambda j, carry: score_chunk(2 * j + 1, score_chunk(2 * j, carry)), 0)

    def count(mask_fn):
        def body(j, acc):
            parts = []
            for c in (2 * j, 2 * j + 1):
                c0 = chunk_start(c)
                ones = jnp.where(mask_fn(key_s[pl.ds(c0, ck), :], c0 + row), 1, 0)
                parts += [ones[i * SUBLANES:(i + 1) * SUBLANES] for i in range(ck // SUBLANES)]
            return acc + _tree_sum(parts)
        acc = lax.fori_loop(0, n_pair, body, jnp.zeros((SUBLANES, nq), jnp.int32))
        return jnp.sum(acc, axis=0, keepdims=True)

    def value_bit(i, carry):
        lo, n_lo = carry
        cand = lo + jnp.left_shift(jnp.int32(1), 31 - i)
        n_cand = count(lambda key, pos: key >= cand)
        ok = n_cand >= topk
        return jnp.where(ok, cand, lo), jnp.where(ok, n_cand, n_lo)

    start = (jnp.full((1, nq), INT_MIN, jnp.int32), jnp.full((1, nq), s, jnp.int32))
    thr, n_ge = lax.cond(qb * nq + nq <= topk, lambda: start,
                         lambda: lax.fori_loop(0, 32, value_bit, start))

    n_bits = s.bit_length()

    def tie_search():
        need = topk - count(lambda key, pos: key > thr)

        def index_bit(i, lo):
            cand = lo + jnp.left_shift(jnp.int32(1), n_bits - 1 - i)
            return jnp.where(count(lambda key, pos: (key == thr) & (pos < cand)) < need, cand, lo)

        return lax.fori_loop(0, n_bits, index_bit, jnp.zeros((1, nq), jnp.int32))

    surplus = jnp.max(jnp.where((n_ge > topk) & (thr > INT_MIN), 1, 0))
    last = lax.cond(surplus > 0, tie_search, lambda: jnp.full((1, nq), s, jnp.int32))

    q_t = (_dot(wuqt_ref[...], cq.astype(BF16)) * lat ** -0.5).astype(BF16)

    def attend(c, carry):
        c0 = chunk_start(c)
        key = key_s[pl.ds(c0, ck), :]
        pos = c0 + row
        sel = ((key > thr) | ((key == thr) & (pos <= last))) & (pos <= tpos)
        bias = jnp.where(sel, 0.0, NEG_BIG)
        ckv = ckv_ref[0, pl.ds(c0, ck), :]
        ckvt = ckvt_ref[0, :, pl.ds(c0, ck)]
        new = []
        for hd in range(DSA_HEADS):
            m, l, acc = carry[hd]
            lg = _dot(ckv, q_t[hd * lat:(hd + 1) * lat, :]) + bias
            m_new = jnp.maximum(m, jnp.max(lg, axis=0, keepdims=True))
            alpha = jnp.exp(m - m_new)
            p = jnp.exp(lg - m_new)
            new.append((m_new, alpha * l + jnp.sum(p, axis=0, keepdims=True),
                        alpha * acc + _dot(ckvt, p.astype(BF16))))
        return tuple(new)

    init = tuple((jnp.full((1, nq), NEG_BIG, F32), jnp.zeros((1, nq), F32), jnp.zeros((lat, nq), F32))
                 for _ in range(DSA_HEADS))
    final = lax.fori_loop(0, n_pair, lambda j, carry: attend(2 * j + 1, attend(2 * j, carry)), init)
    outs = [_dot(wuvt_ref[hd], (acc / l).astype(BF16)) for hd, (_, l, acc) in enumerate(final)]
    o_ref[0] = jnp.concatenate(outs, axis=0).T.astype(o_ref.dtype)


def _dsa(cqt, wit, ki6, ckv, ckvt, w_uq, w_qidx, w_uv, ck):
    b, c, s = cqt.shape
    nq = DSA_Q_BLOCK
    wuqt = w_uq.reshape(c, -1).T.astype(BF16)
    wqit = w_qidx.reshape(c, -1).T
    wuvt = jnp.transpose(w_uv, (0, 2, 1)).astype(BF16)
    return pl.pallas_call(
        functools.partial(_dsa_kernel, ck=ck),
        out_shape=jax.ShapeDtypeStruct((b, s, DSA_HEADS * DSA_HEAD_DIM), BF16),
        grid=(b, s // nq),
        in_specs=[pl.BlockSpec((1, c, nq), lambda i, j: (i, 0, j)),
                  pl.BlockSpec((1, IDX_HEADS, nq), lambda i, j: (i, 0, j)),
                  pl.BlockSpec((1, s, IDX_K_COLS), lambda i, j: (i, 0, 0)),
                  pl.BlockSpec((1, s, DSA_KV_LATENT), lambda i, j: (i, 0, 0)),
                  pl.BlockSpec((1, DSA_KV_LATENT, s), lambda i, j: (i, 0, 0)),
                  _full(wuqt.shape), _full(wqit.shape), _full(wuvt.shape)],
        out_specs=pl.BlockSpec((1, nq, DSA_HEADS * DSA_HEAD_DIM), lambda i, j: (i, j, 0)),
        scratch_shapes=[pltpu.VMEM((s, nq), jnp.int32)],
        compiler_params=_cparams("arbitrary", "arbitrary"),
        name="dsa",
    )(cqt, wit, ki6, ckv, ckvt, wuqt, wqit, wuvt)


Y_PAD = 8


def _head_ones(n, dtype):
    r_i = lax.broadcasted_iota(jnp.int32, (n, n), 0) // RWKV_HEAD_DIM
    c_i = lax.broadcasted_iota(jnp.int32, (n, n), 1) // RWKV_HEAD_DIM
    return jnp.where(r_i == c_i, 1.0, 0.0).astype(dtype)


def _split_bf16(t):
    hi = t.astype(BF16)
    lo = (t - hi.astype(F32)).astype(BF16)
    return hi, lo


def _rwkv_kernel(z_ref, mu_ref, w0_ref, wup_ref, a0_ref, aup_ref, gup_ref, kk_ref, ka_ref, rk_ref, lng_ref, lnb_ref,
                 o_ref, st_s, prev_s, step_s, y_s, bonus_s, gate_s, *, grp):
    c = pl.program_id(1)
    n_g, tc, _ = z_ref.shape
    hd, wd_ = RWKV_HEAD_DIM, RWKV_WIDTH

    @pl.when(c == 0)
    def _():
        st_s[...] = jnp.zeros_like(st_s)
        prev_s[...] = jnp.zeros_like(prev_s)

    ones_b = _head_ones(wd_, BF16)
    diag = jnp.where(lax.broadcasted_iota(jnp.int32, (hd, wd_), 0)
                     == lax.broadcasted_iota(jnp.int32, (hd, wd_), 1) % hd, 1.0, 0.0).astype(F32)

    def head_sum(t):
        hi, lo = _split_bf16(t)
        return _dot(hi, ones_b) + _dot(lo, ones_b)

    zs = []
    for g in range(n_g):
        z = z_ref[g]
        row = lax.broadcasted_iota(jnp.int32, z.shape, 0)
        z_prev = jnp.where(row == 0, prev_s[g], pltpu.roll(z, 1, 0))
        prev_s[g] = z[tc - 1:tc, :]
        zs.append(z + mu_ref[...] * (z_prev - z))
    z = jnp.concatenate(zs, axis=0)
    r = z[:, 0:wd_]
    k = z[:, wd_:2 * wd_]
    v = z[:, 2 * wd_:3 * wd_]
    o = 3 * wd_
    w_lora = z[:, o:o + RWKV_LORA_W]
    a_lora = z[:, o + RWKV_LORA_W:o + RWKV_LORA_W + RWKV_LORA_A]
    g_lora = z[:, o + RWKV_LORA_W + RWKV_LORA_A:]
    w_log = _log_sigmoid(w0_ref[...] + _dot_hi(jnp.tanh(w_lora), wup_ref[...])) - 0.5
    a = _sigmoid(a0_ref[...] + _dot_hi(a_lora, aup_ref[...]))
    gate_s[...] = _dot_hi(_sigmoid(g_lora), gup_ref[...])
    kk = k * kk_ref[...]
    kk = kk / jnp.maximum(jnp.sqrt(head_sum(kk * kk)), 1e-12)
    k = k * (1.0 + (a - 1.0) * ka_ref[...])
    step_s[0] = jnp.exp(-jnp.exp(w_log))
    step_s[1] = -kk
    step_s[2] = kk * a
    step_s[3] = k
    step_s[4] = r
    step_s[5] = v
    bonus_s[...] = head_sum(r * k * rk_ref[...]) * v

    n_grp = n_g // grp

    def rows(kind, g0, t):
        return jnp.concatenate(
            [jnp.broadcast_to(step_s[kind, pl.ds((g0 + i) * tc + t, 1), :], (hd, wd_)) for i in range(grp)], axis=0)

    def store_y(yb, g0, t_write):
        yb = yb * diag_g
        for i in range(grp):
            tile = jnp.sum(yb[i * hd:(i + 1) * hd].reshape(hd // SUBLANES, SUBLANES, wd_), axis=0)
            y_s[pl.ds(Y_PAD + (g0 + i) * tc + t_write, 1), :] = jnp.sum(tile, axis=0, keepdims=True)

    def rows_b(kind, g0, t):
        tiles = []
        for i in range(grp):
            r16 = jnp.broadcast_to(step_s[kind, pl.ds((g0 + i) * tc + t, 1), :], (BF16_SUBLANES, wd_)).astype(BF16)
            tiles.append(jnp.broadcast_to(r16[None], (hd // BF16_SUBLANES, BF16_SUBLANES, wd_)).reshape(hd, wd_))
        return jnp.concatenate(tiles, axis=0)

    def step(t, carry):
        for q in range(n_grp):
            g0 = q * grp
            st = st_s[q]
            st_b = st.astype(BF16)
            sa = _dot(st_b * rows_b(1, g0, t), ones_b)
            store_y(_dot(st_b * rows_b(4, g0, jnp.maximum(t - 1, 0)), ones_b), g0, t - 1)
            vb = _dot(rows_b(5, g0, t) * diag_b, ones_b)
            st_s[q] = st * rows(0, g0, t) + sa * rows(2, g0, t) + vb * rows(3, g0, t)
        return carry

    diag_g = jnp.concatenate([diag] * grp, axis=0)
    diag_b = diag_g.astype(BF16)
    lax.fori_loop(0, tc, step, 0, unroll=8)
    for q in range(n_grp):
        store_y(_dot(st_s[q].astype(BF16) * rows_b(4, q * grp, tc - 1), ones_b), q * grp, tc - 1)

    y = y_s[pl.ds(Y_PAD, n_g * tc), :]
    mean = head_sum(y) * (1.0 / hd)
    yc = y - mean
    var = head_sum(yc * yc) * (1.0 / hd)
    yn = yc * lax.rsqrt(var + RWKV_GN_EPS) * lng_ref[...] + lnb_ref[...]
    out = ((yn + bonus_s[...]) * gate_s[...]).astype(o_ref.dtype)
    for g in range(n_g):
        o_ref[g] = out[g * tc:(g + 1) * tc]


def _rwkv(zr, mu, w0, w_up, a0, a_up, g_up, k_k, k_a, r_k, ln_g, ln_b, n_g, grp, tc):
    b, s, zin = zr.shape
    wd_ = RWKV_WIDTH
    vec = lambda p: p.reshape(1, -1)
    params = [vec(mu), vec(w0), w_up, vec(a0), a_up, g_up, vec(k_k), vec(k_a), vec(r_k), vec(ln_g), vec(ln_b)]
    return pl.pallas_call(
        functools.partial(_rwkv_kernel, grp=grp),
        out_shape=jax.ShapeDtypeStruct((b, s, wd_), BF16),
        grid=(b // n_g, s // tc),
        in_specs=[pl.BlockSpec((n_g, tc, zin), lambda i, c: (i, c, 0))] + [_full(p.shape) for p in params],
        out_specs=pl.BlockSpec((n_g, tc, wd_), lambda i, c: (i, c, 0)),
        scratch_shapes=[pltpu.VMEM((n_g // grp, grp * RWKV_HEAD_DIM, wd_), F32), pltpu.VMEM((n_g, 1, zin), F32),
                        pltpu.VMEM((6, n_g * tc, wd_), F32), pltpu.VMEM((Y_PAD + n_g * tc, wd_), F32),
                        pltpu.VMEM((n_g * tc, wd_), F32), pltpu.VMEM((n_g * tc, wd_), F32)],
        compiler_params=_cparams("arbitrary", "arbitrary"),
        name="rwkv",
    )(zr, *params)


def _softcap(t):
    return GATE_SOFTCAP * jnp.tanh(t / GATE_SOFTCAP)


def _mlstm_kernel(zm_ref, mkt_ref, mvt_ref, mgt_ref, cw_row_ref, cb_row_ref, cw_col_ref, cb_col_ref,
                  bg_row_ref, bg_col_ref, ng_ref, o_ref, q_s, k_s, kt_s, gc_s, gr_s, yt_s):
    s = zm_ref.shape[1]
    nh, dk, dv, lc = MLSTM_HEADS, MLSTM_QK_DIM, MLSTM_V_DIM, MLSTM_CHUNK
    pair = 2 * lc

    qk = zm_ref[0, :, 0:2 * MLSTM_QK]
    pos_r = lax.broadcasted_iota(jnp.int32, qk.shape, 0)
    acc = cb_row_ref[...] + qk * cw_row_ref[CONV_WIDTH - 1:CONV_WIDTH, :]
    for r in range(1, CONV_WIDTH):
        sh = jnp.where(pos_r >= r, pltpu.roll(qk, r, 0), 0.0)
        acc = acc + sh * cw_row_ref[CONV_WIDTH - 1 - r:CONV_WIDTH - r, :]
    acc = _silu(acc)
    q_s[...] = acc[:, :MLSTM_QK] * dk ** -0.5
    k_s[...] = acc[:, MLSTM_QK:]
    kt = mkt_ref[0]
    pos_c = lax.broadcasted_iota(jnp.int32, kt.shape, 1)
    acc_t = cb_col_ref[...] + kt * cw_col_ref[:, CONV_WIDTH - 1:CONV_WIDTH]
    for r in range(1, CONV_WIDTH):
        sh = jnp.where(pos_c >= r, pltpu.roll(kt, r, 1), 0.0)
        acc_t = acc_t + sh * cw_col_ref[:, CONV_WIDTH - 1 - r:CONV_WIDTH - r]
    kt_s[...] = _silu(acc_t)

    gcol = _softcap(zm_ref[0, :, 2 * MLSTM_QK + 2 * MLSTM_V:] + bg_row_ref[...])
    lane = lax.broadcasted_iota(jnp.int32, gcol.shape, 1)
    gc_s[...] = jnp.where(lane < nh, gcol, _log_sigmoid(gcol))
    grow = _softcap(mgt_ref[0] + bg_col_ref[...])
    sub = lax.broadcasted_iota(jnp.int32, grow.shape, 0)
    gr_s[...] = jnp.where(sub < nh, grow, _log_sigmoid(grow))

    r_i = lax.broadcasted_iota(jnp.int32, (lc, lc), 0)
    c_i = lax.broadcasted_iota(jnp.int32, (lc, lc), 1)
    tri = jnp.where(c_i <= r_i, 1.0, 0.0).astype(F32)
    causal_t = r_i <= c_i
    ones_rows = jnp.ones((8, lc), F32)

    def chunk_pair(p, carry):
        base = pl.multiple_of(p * pair, pair)
        gr_slab = gr_s[:, pl.ds(base, pair)]
        kt_slab = kt_s[:, pl.ds(base, pair)]
        vt_slab = mvt_ref[0, :, pl.ds(base, pair)]
        outs = [[] for _ in range(nh)]
        for sc in range(2):
            r0 = base + sc * lc
            gcc = gc_s[pl.ds(r0, lc), :]
            grc = gr_slab[:, sc * lc:(sc + 1) * lc]
            bcum_col = _dot_hi(tri, gcc)
            bcum_row = _dot_nt_hi(grc, tri)
            qc = q_s[pl.ds(r0, lc), :]
            kc = k_s[pl.ds(r0, lc), :]
            new_carry = []
            for hd in range(nh):
                c_aug, m_prev = carry[hd]
                bc_c = bcum_col[:, nh + hd:nh + hd + 1]
                li_c = gcc[:, hd:hd + 1]
                bc_r = bcum_row[nh + hd:nh + hd + 1, :]
                li_r = grc[hd:hd + 1, :]
                gtot = bc_r[:, lc - 1:lc]
                d_t = jnp.where(causal_t, bc_r - bc_c + li_c, NEG_BIG)
                m_inter = bc_r + m_prev
                m_t = jnp.maximum(m_inter, jnp.max(d_t, axis=0, keepdims=True))
                q_h = qc[:, hd * dk:(hd + 1) * dk].astype(BF16)
                k_h = kc[:, hd * dk:(hd + 1) * dk].astype(BF16)
                kt_h = kt_slab[hd * dk:(hd + 1) * dk, sc * lc:(sc + 1) * lc].astype(BF16)
                vt_h = vt_slab[hd * dv:(hd + 1) * dv, sc * lc:(sc + 1) * lc]
                vt_aug = jnp.concatenate([vt_h, ones_rows], axis=0)
                s_t = _dot_nt(k_h, q_h)
                w_t = jnp.exp(d_t - m_t) * s_t
                s_inter = jnp.exp(m_inter - m_t)
                numden = _dot(vt_aug.astype(BF16), w_t.astype(BF16)) + s_inter * _dot_nt(c_aug.astype(BF16), q_h)
                den = numden[dv:dv + 1, :]
                outs[hd].append(numden[:dv, :] / jnp.maximum(jnp.abs(den), jnp.exp(-m_t)))
                a_log = gtot - bc_r + li_r
                a_max = jnp.max(a_log, axis=-1, keepdims=True)
                a_w = jnp.exp(a_log - a_max)
                kvn = _dot_nt((vt_aug * a_w).astype(BF16), kt_h)
                m_new = jnp.maximum(gtot + m_prev, a_max)
                s_old = jnp.exp(gtot + m_prev - m_new)
                s_new = jnp.exp(a_max - m_new)
                new_carry.append((s_old * c_aug + s_new * kvn, m_new))
            carry = tuple(new_carry)
        for hd in range(nh):
            yt_s[hd * dv:(hd + 1) * dv, pl.ds(base, pair)] = jnp.concatenate(outs[hd], axis=-1)
        return carry

    init = tuple((jnp.zeros((dv + 8, dk), F32), jnp.full((1, 1), NEG_BIG, F32)) for _ in range(nh))
    lax.fori_loop(0, s // pair, chunk_pair, init)

    parts = []
    for hd in range(nh):
        blk = yt_s[hd * dv:(hd + 1) * dv, :]
        parts.append(blk * lax.rsqrt(jnp.mean(blk * blk, axis=0, keepdims=True) + EPS))
    y = jnp.concatenate(parts, axis=0).T
    o_gate = _sigmoid(zm_ref[0, :, 2 * MLSTM_QK + MLSTM_V:2 * MLSTM_QK + 2 * MLSTM_V])
    o_ref[0] = (y * ng_ref[...] * o_gate).astype(o_ref.dtype)


def _mlstm(zm, mkt, mvt, mgt, conv_w, conv_b, b_i, b_f, norm_g):
    b, s, _ = zm.shape
    cw_col = conv_w[:, MLSTM_QK:].T
    cb_col = conv_b[MLSTM_QK:].reshape(-1, 1)
    bg = jnp.concatenate([b_i, b_f])
    bg_row = jnp.pad(bg, (0, LANES - 2 * MLSTM_HEADS)).reshape(1, LANES)
    bg_col = bg.reshape(-1, 1)
    per_b = lambda r, c: pl.BlockSpec((1, r, c), lambda i: (i, 0, 0))
    return pl.pallas_call(
        _mlstm_kernel,
        out_shape=jax.ShapeDtypeStruct((b, s, MLSTM_V), BF16),
        grid=(b,),
        in_specs=[per_b(s, M_COLS), per_b(MLSTM_QK, s), per_b(MLSTM_V, s), per_b(8, s),
                  _full(conv_w.shape), _full((1, 2 * MLSTM_QK)), _full(cw_col.shape), _full(cb_col.shape),
                  _full((1, LANES)), _full((8, 1)), _full((1, MLSTM_V))],
        out_specs=per_b(s, MLSTM_V),
        scratch_shapes=[pltpu.VMEM((s, MLSTM_QK), F32), pltpu.VMEM((s, MLSTM_QK), F32), pltpu.VMEM((MLSTM_QK, s), F32),
                        pltpu.VMEM((s, LANES), F32), pltpu.VMEM((8, s), F32), pltpu.VMEM((MLSTM_V, s), F32)],
        compiler_params=_cparams("arbitrary"),
        name="mlstm",
    )(zm, mkt, mvt, mgt, conv_w, conv_b.reshape(1, -1), cw_col, cb_col, bg_row, bg_col, norm_g.reshape(1, -1))


def _merge_kernel(x_ref, g_ref, y0_ref, y1_ref, y2_ref, y3_ref, wg_ref, wb_ref, wo_ref, o_ref):
    x = x_ref[...]
    d = x.shape[1]
    h = _rms_rows(x, g_ref[...]).astype(BF16)
    merged = None
    for n, y_ref in enumerate((y0_ref, y1_ref, y2_ref, y3_ref)):
        gate = _sigmoid(_dot_nt(h, wg_ref[n * d:(n + 1) * d, :]))
        term = gate * _dot(y_ref[...], wb_ref[n])
        merged = term if merged is None else merged + term
    o_ref[...] = x + _dot(merged.astype(BF16), wo_ref[...])


def _merge(x2, g, ys, w_gate, w_branch, w_out, tm):
    t, d = x2.shape
    row = lambda w: pl.BlockSpec((tm, w), lambda i: (i, 0))
    return pl.pallas_call(
        _merge_kernel,
        out_shape=jax.ShapeDtypeStruct((t, d), F32),
        grid=(t // tm,),
        in_specs=[row(d), _full((1, d))] + [row(BRANCH_WIDTH)] * N_BRANCH
                 + [_full(w_gate.shape), _full(w_branch.shape), _full(w_out.shape)],
        out_specs=row(d),
        compiler_params=_cparams("arbitrary"),
        name="merge",
    )(x2, g.reshape(1, d), *ys, w_gate, w_branch, w_out)


def _ffn_kernel(x_ref, g_ref, w1_ref, w3_ref, w2_ref, o_ref):
    x = x_ref[...]
    h = _rms_rows(x, g_ref[...]).astype(BF16)
    u = _silu(_dot(h, w1_ref[...])) * _dot(h, w3_ref[...])
    o_ref[...] = x + _dot(u.astype(BF16), w2_ref[...])


def _ffn(x2, g, w1, w3, w2, tm):
    t, d = x2.shape
    row = pl.BlockSpec((tm, d), lambda i: (i, 0))
    return pl.pallas_call(
        _ffn_kernel,
        out_shape=jax.ShapeDtypeStruct((t, d), F32),
        grid=(t // tm,),
        in_specs=[row, _full((1, d)), _full(w1.shape), _full(w3.shape), _full(w2.shape)],
        out_specs=row,
        compiler_params=_cparams("arbitrary"),
        name="ffn",
    )(x2, g.reshape(1, d), w1, w3, w2)


MOE_CAP = 160


def _moe_router_kernel(x_ref, g_ref, wr_ref, h_ref, rank_ref, gate_ref, rrow_ref, cnt_ref):
    tm = x_ref.shape[0]
    h = _rms_rows(x_ref[...], g_ref[...])
    h_ref[...] = h.astype(BF16)
    logits = _dot_hi(h, wr_ref[...])
    lane = lax.broadcasted_iota(jnp.int32, logits.shape, 1)
    logits = jnp.where(lane < N_EXPERTS, logits, -jnp.inf)
    v1 = jnp.max(logits, axis=-1, keepdims=True)
    i1 = jnp.min(jnp.where(logits == v1, lane, LANES), axis=-1, keepdims=True)
    rest = jnp.where(lane == i1, -jnp.inf, logits)
    v2 = jnp.max(rest, axis=-1, keepdims=True)
    i2 = jnp.min(jnp.where(rest == v2, lane, LANES), axis=-1, keepdims=True)
    e2 = jnp.exp(v2 - v1)
    gate_ref[...] = jnp.where(lane == i1, 1.0 / (1.0 + e2), 0.0) + jnp.where(lane == i2, e2 / (1.0 + e2), 0.0)
    routed = jnp.where((lane == i1) | (lane == i2), 1.0, 0.0)
    r_i = lax.broadcasted_iota(jnp.int32, (tm, tm), 0)
    c_i = lax.broadcasted_iota(jnp.int32, (tm, tm), 1)
    rank = _dot(jnp.where(c_i < r_i, 1.0, 0.0).astype(BF16), routed.astype(BF16))
    rank = jnp.where(routed > 0.0, rank, -1.0)
    rank_ref[...] = rank
    eye = jnp.where(lax.broadcasted_iota(jnp.int32, (LANES, LANES), 0)
                    == lax.broadcasted_iota(jnp.int32, (LANES, LANES), 1), 1.0, 0.0)
    rrow_ref[0] = _dot_nt_hi(eye, rank)[0:N_EXPERTS, :]
    cnt_ref[0] = jnp.broadcast_to(jnp.sum(routed, axis=0, keepdims=True), (8, LANES))


def _moe_expert_kernel(cnt_ref, y_ref, h_ref, rrow_ref, rank_ref, gate_ref, w1_ref, w3_ref, w2_ref, o_ref, acc_s):
    e = pl.program_id(0)
    i = pl.program_id(1)
    tm = h_ref.shape[0]
    cap = MOE_CAP
    lane = lax.broadcasted_iota(jnp.int32, (tm, LANES), 1)
    rank_col = jnp.sum(jnp.where(lane == e, rank_ref[...], 0.0), axis=-1, keepdims=True)
    gate_col = jnp.sum(jnp.where(lane == e, gate_ref[...], 0.0), axis=-1, keepdims=True)
    rank_row = rrow_ref[0, pl.ds(e, 1), :]
    slot_r = lax.broadcasted_iota(jnp.int32, (cap, tm), 0).astype(F32)
    slot_c = lax.broadcasted_iota(jnp.int32, (tm, cap), 1).astype(F32)
    acc_s[...] = jnp.zeros_like(acc_s)

    def one_pass(p, carry):
        base = (p * cap).astype(F32)
        gather = jnp.where(rank_row - base == slot_r, 1.0, 0.0).astype(BF16)
        xc = _dot(gather, h_ref[...]).astype(BF16)
        u = _silu(_dot(xc, w1_ref[0])) * _dot(xc, w3_ref[0])
        yc_hi, yc_lo = _split_bf16(_dot(u.astype(BF16), w2_ref[0]))
        scatter = jnp.where(rank_col - base == slot_c, 1.0, 0.0).astype(BF16)
        acc_s[...] += _dot(scatter, yc_hi) + _dot(scatter, yc_lo)
        return carry

    n_pass = (cnt_ref[i, e] + cap - 1) // cap
    lax.fori_loop(0, n_pass, one_pass, 0)
    o_ref[...] = y_ref[...] + gate_col * acc_s[...]


def _moe(x2, g, router, w1, w3, w2, tm):
    t, d = x2.shape
    n_e, _, f = w1.shape
    n_t = t // tm
    router_p = jnp.pad(router, ((0, 0), (0, LANES - n_e)))
    row = lambda w: pl.BlockSpec((tm, w), lambda i: (i, 0))
    h, rank, gate, rrow, cnt = pl.pallas_call(
        _moe_router_kernel,
        out_shape=[jax.ShapeDtypeStruct((t, d), BF16), jax.ShapeDtypeStruct((t, LANES), F32),
                   jax.ShapeDtypeStruct((t, LANES), F32), jax.ShapeDtypeStruct((n_t, N_EXPERTS, tm), F32),
                   jax.ShapeDtypeStruct((n_t, 8, LANES), F32)],
        grid=(n_t,),
        in_specs=[row(d), _full((1, d)), _full((d, LANES))],
        out_specs=[row(d), row(LANES), row(LANES), pl.BlockSpec((1, N_EXPERTS, tm), lambda i: (i, 0, 0)),
                   pl.BlockSpec((1, 8, LANES), lambda i: (i, 0, 0))],
        compiler_params=_cparams("arbitrary"),
        name="moe_router",
    )(x2, g.reshape(1, d), router_p)
    counts = cnt[:, 0, :n_e].astype(jnp.int32)
    tile = lambda w: pl.BlockSpec((tm, w), lambda e, i, c: (i, 0))
    weight = lambda r, c_: pl.BlockSpec((1, r, c_), lambda e, i, c: (e, 0, 0), pipeline_mode=pl.Buffered(1))
    return pl.pallas_call(
        _moe_expert_kernel,
        out_shape=jax.ShapeDtypeStruct((t, d), F32),
        grid_spec=pltpu.PrefetchScalarGridSpec(
            num_scalar_prefetch=1,
            grid=(n_e, n_t),
            in_specs=[tile(d), tile(d), pl.BlockSpec((1, N_EXPERTS, tm), lambda e, i, c: (i, 0, 0)),
                      tile(LANES), tile(LANES), weight(d, f), weight(d, f), weight(f, d)],
            out_specs=tile(d),
            scratch_shapes=[pltpu.VMEM((tm, d), F32)]),
        input_output_aliases={1: 0},
        compiler_params=_cparams("arbitrary", "arbitrary"),
        name="moe_experts",
    )(counts, x2, h, rrow, rank, gate, w1, w3, w2)


def _final_norm_kernel(x_ref, g_ref, o_ref):
    o_ref[...] = _rms_rows(x_ref[...], g_ref[...])


def _final_norm(x2, g, tm):
    t, d = x2.shape
    row = pl.BlockSpec((tm, d), lambda i: (i, 0))
    return pl.pallas_call(
        _final_norm_kernel,
        out_shape=jax.ShapeDtypeStruct((t, d), F32),
        grid=(t // tm,),
        in_specs=[row, _full((1, d))],
        out_specs=row,
        compiler_params=_cparams("arbitrary"),
        name="final_norm",
    )(x2, g.reshape(1, d))


def _tiles(b, s):
    tiles = dict(
        rows=min(512, s),
        fox_q=min(1024, s),
        dsa_keys=min(256, s // 2),
        rwkv_seqs=min(16, b),
        rwkv_group=min(4, b),
        rwkv_steps=64,
    )
    assert s % tiles['rows'] == 0 and s % tiles['fox_q'] == 0 and s % (2 * tiles['dsa_keys']) == 0
    assert s % DSA_Q_BLOCK == 0 and s % (2 * MLSTM_CHUNK) == 0 and s % tiles['rwkv_steps'] == 0
    assert b % tiles['rwkv_seqs'] == 0 and tiles['rwkv_seqs'] % tiles['rwkv_group'] == 0
    return tiles


def kernel(x, norm_mix_g, w_in, dsa_g_cq, dsa_g_ckv, dsa_g_kidx, dsa_w_uq, dsa_w_qidx, dsa_w_uv, rwkv_mu, rwkv_w0, rwkv_w_up, rwkv_a0, rwkv_a_up, rwkv_g_up, rwkv_k_k, rwkv_k_a, rwkv_r_k, rwkv_ln_g, rwkv_ln_b, mlstm_conv_w, mlstm_conv_b, mlstm_b_i, mlstm_b_f, mlstm_norm_g, fox_b_f, w_branch, w_out, norm_ffn_g, ffn_w1, ffn_w3, ffn_w2, moe_router, moe_w1, moe_w3, moe_w2, final_norm_g):
    b, s, d = x.shape
    depth = w_in.shape[0]
    t = b * s
    tl = _tiles(b, s)
    for l in range(depth):
        wts = _inproj_weights(w_in[l])
        (zr, zm, fq, fk, fv, ckv, ki6, cqt, ckvt, wit, frow, mkt, mvt, mgt) = _inproj(
            x, norm_mix_g[l], wts, dsa_g_cq[l], dsa_g_ckv[l], dsa_g_kidx[l], fox_b_f[l], tl['rows'])
        y_fox = _fox(fq, fk, fv, frow, tl['fox_q'])
        y_dsa = _dsa(cqt, wit, ki6, ckv, ckvt, dsa_w_uq[l], dsa_w_qidx[l], dsa_w_uv[l], tl['dsa_keys'])
        y_rwkv = _rwkv(zr, rwkv_mu[l], rwkv_w0[l], rwkv_w_up[l], rwkv_a0[l], rwkv_a_up[l], rwkv_g_up[l], rwkv_k_k[l],
                       rwkv_k_a[l], rwkv_r_k[l], rwkv_ln_g[l], rwkv_ln_b[l], tl['rwkv_seqs'], tl['rwkv_group'],
                       tl['rwkv_steps'])
        y_mlstm = _mlstm(zm, mkt, mvt, mgt, mlstm_conv_w[l], mlstm_conv_b[l], mlstm_b_i[l], mlstm_b_f[l],
                         mlstm_norm_g[l])
        ys = tuple(y.reshape(t, BRANCH_WIDTH) for y in (y_dsa, y_rwkv, y_mlstm, y_fox))
        x2 = _merge(x.reshape(t, d), norm_mix_g[l], ys, wts['w_gate'], w_branch[l].astype(BF16),
                    w_out[l].astype(BF16), tl['rows'])
        j = l // 2
        if l % 2 == 0:
            x2 = _ffn(x2, norm_ffn_g[l], ffn_w1[j].astype(BF16), ffn_w3[j].astype(BF16), ffn_w2[j].astype(BF16),
                      tl['rows'])
        else:
            x2 = _moe(x2, norm_ffn_g[l], moe_router[j], moe_w1[j].astype(BF16), moe_w3[j].astype(BF16),
                      moe_w2[j].astype(BF16), tl['rows'])
        x = x2.reshape(b, s, d)
    return _final_norm(x.reshape(t, d), final_norm_g, tl['rows']).reshape(b, s, d)
```
